```python
import math
import jax
import jax.numpy as jnp
from jax import lax
import numpy as np

D_MODEL = 2048
BATCH = 4
SEQ = 2048
DEPTH = 2

GRID_W = 64
CTX_LEN = 256
HEAD_DIM = 128
DIFF_HEADS = 4
DIFF_V_DIM = 2 * HEAD_DIM
GQA_Q_HEADS = 8
GQA_KV_HEADS = 2
GQA_GROUP = GQA_Q_HEADS // GQA_KV_HEADS
Q_BLOCK = 128
ROPE_THETA = 10000.0
ROPE_FREQS = HEAD_DIM // 4
DIFF_QK_WIDTH = DIFF_HEADS * 2 * HEAD_DIM
DIFF_V_WIDTH = DIFF_HEADS * DIFF_V_DIM
GQA_Q_WIDTH = GQA_Q_HEADS * HEAD_DIM
GQA_KV_WIDTH = GQA_KV_HEADS * HEAD_DIM
ATTN_IN_WIDTH = 2 * DIFF_QK_WIDTH + DIFF_V_WIDTH + GQA_Q_WIDTH + 2 * GQA_KV_WIDTH
ATTN_OUT_WIDTH = DIFF_V_WIDTH + GQA_Q_WIDTH
ATTN_SPLITS = (DIFF_QK_WIDTH, 2 * DIFF_QK_WIDTH, 2 * DIFF_QK_WIDTH + DIFF_V_WIDTH,
               2 * DIFF_QK_WIDTH + DIFF_V_WIDTH + GQA_Q_WIDTH,
               2 * DIFF_QK_WIDTH + DIFF_V_WIDTH + GQA_Q_WIDTH + GQA_KV_WIDTH)
SSM_GROUP = 16
SSM_STATE = 64
SSM_GROUPS = D_MODEL // SSM_GROUP
N_EXPERTS = 32
TOP_K = 4
D_FF = D_MODEL
SWIGLU_LIMIT = 7.0
SWIGLU_ALPHA = 1.702
EXPERT_BLOCK = 128
RMS_EPS = 1e-6

kernel_name = 'hybrid_diffattn_gqa_s5_moe_dit'


def _rms_norm(x, g):
    x32 = x.astype(jnp.float32)
    y = x32 * lax.rsqrt(jnp.mean(x32 * x32, axis=-1, keepdims=True) + RMS_EPS)
    return (y * g.astype(jnp.float32)).astype(x.dtype)


def _modulation(cond, w_mod, b_mod):
    m = jax.nn.silu(cond) @ w_mod + b_mod
    return jnp.split(m[..., None, :], 6, axis=-1)


def _modulate(h, shift, scale):
    return h * (1.0 + scale) + shift


def _axial_rope_tables(rows):
    row_id, col_id = jnp.meshgrid(jnp.arange(rows), jnp.arange(GRID_W), indexing='ij')
    inv_freq = ROPE_THETA ** (-jnp.arange(ROPE_FREQS, dtype=jnp.float32) / ROPE_FREQS)
    ang = jnp.concatenate([row_id.reshape(-1, 1) * inv_freq, col_id.reshape(-1, 1) * inv_freq], axis=-1)
    return jnp.cos(ang), jnp.sin(ang)


def _rope_2d(x, cos, sin):
    half = HEAD_DIM // 2
    x1, x2 = x[..., :half], x[..., half:]
    return jnp.concatenate([x1 * cos - x2 * sin, x1 * sin + x2 * cos], axis=-1).astype(x.dtype)


def _to_blocks(q):
    *lead, s, d = q.shape
    return jnp.moveaxis(q.reshape(*lead, s // Q_BLOCK, Q_BLOCK, d), -3, 0)


def _from_blocks(o):
    o = jnp.moveaxis(o, 0, -3)
    *lead, nb, qb, d = o.shape
    return o.reshape(*lead, nb * qb, d)


def _diff_attention(q1, q2, k1, k2, v, lam):
    scale = HEAD_DIM ** -0.5
    p1 = jax.nn.softmax(jnp.einsum('bhqd,bhkd->bhqk', q1, k1).astype(jnp.float32) * scale, axis=-1)
    p2 = jax.nn.softmax(jnp.einsum('bhqd,bhkd->bhqk', q2, k2).astype(jnp.float32) * scale, axis=-1)
    return jnp.einsum('bhqk,bhkv->bhqv', (p1 - lam * p2).astype(v.dtype), v)


def _gqa_attention(q, k, v):
    s = jnp.einsum('bngqd,bnkd->bngqk', q, k).astype(jnp.float32) * HEAD_DIM ** -0.5
    p = jax.nn.softmax(s, axis=-1)
    return jnp.einsum('bngqk,bnkd->bngqd', p.astype(v.dtype), v)


def _split_attention_projection(p):
    b, t, _ = p.shape
    dq, dk, dv, gq, gk, gv = jnp.split(p, ATTN_SPLITS, axis=-1)
    dq = dq.reshape(b, t, DIFF_HEADS, 2, HEAD_DIM).transpose(3, 0, 2, 1, 4)
    dk = dk.reshape(b, t, DIFF_HEADS, 2, HEAD_DIM).transpose(3, 0, 2, 1, 4)
    dv = dv.reshape(b, t, DIFF_HEADS, DIFF_V_DIM).transpose(0, 2, 1, 3)
    gq = gq.reshape(b, t, GQA_KV_HEADS, GQA_GROUP, HEAD_DIM).transpose(0, 2, 3, 1, 4)
    gk = gk.reshape(b, t, GQA_KV_HEADS, HEAD_DIM).transpose(0, 2, 1, 3)
    gv = gv.reshape(b, t, GQA_KV_HEADS, HEAD_DIM).transpose(0, 2, 1, 3)
    return dq, dk, dv, gq, gk, gv


def _attention_mixer(h_lat, h_ctx, cos, sin, w_in, w_out, lambda_q1, lambda_k1, lambda_q2, lambda_k2,
                     g_subln, g_qnorm, g_knorm, lambda_init):
    dq_l, dk_l, dv_l, gq_l, gk_l, gv_l = _split_attention_projection(h_lat @ w_in)
    dq_c, dk_c, dv_c, gq_c, gk_c, gv_c = _split_attention_projection(h_ctx @ w_in)
    gq_l, gk_l = _rms_norm(gq_l, g_qnorm), _rms_norm(gk_l, g_knorm)
    gq_c, gk_c = _rms_norm(gq_c, g_qnorm), _rms_norm(gk_c, g_knorm)
    dq_l, dk_l = _rope_2d(dq_l, cos, sin), _rope_2d(dk_l, cos, sin)
    gq_l, gk_l = _rope_2d(gq_l, cos, sin), _rope_2d(gk_l, cos, sin)
    lam = (jnp.exp(jnp.sum(lambda_q1 * lambda_k1).astype(jnp.float32))
           - jnp.exp(jnp.sum(lambda_q2 * lambda_k2).astype(jnp.float32)) + lambda_init)
    dk_all = jnp.concatenate([dk_c, dk_l], axis=-2)
    dv_all = jnp.concatenate([dv_c, dv_l], axis=-2)
    gk_all = jnp.concatenate([gk_c, gk_l], axis=-2)
    gv_all = jnp.concatenate([gv_c, gv_l], axis=-2)

    def latent_block(qs):
        q1, q2, qg = qs
        return (_diff_attention(q1, q2, dk_all[0], dk_all[1], dv_all, lam),
                _gqa_attention(qg, gk_all, gv_all))

    od_l, og_l = lax.map(latent_block, (_to_blocks(dq_l[0]), _to_blocks(dq_l[1]), _to_blocks(gq_l)))
    od_l, og_l = _from_blocks(od_l), _from_blocks(og_l)
    od_c = _diff_attention(dq_c[0], dq_c[1], dk_c[0], dk_c[1], dv_c, lam)
    og_c = _gqa_attention(gq_c, gk_c, gv_c)

    def merge(od, og):
        b, _, t, _ = od.shape
        od = _rms_norm(od, g_subln) * (1.0 - lambda_init)
        od = od.transpose(0, 2, 1, 3).reshape(b, t, DIFF_V_WIDTH)
        og = og.transpose(0, 3, 1, 2, 4).reshape(b, t, GQA_Q_WIDTH)
        return jnp.concatenate([od, og], axis=-1) @ w_out

    return merge(od_l, og_l), merge(od_c, og_c)


def _zoh(a_re, a_im, b_re, b_im, log_dt):
    dt = jnp.exp(log_dt)[:, None]
    mag = jnp.exp(a_re * dt)
    ab_re, ab_im = mag * jnp.cos(a_im * dt), mag * jnp.sin(a_im * dt)
    den = a_re * a_re + a_im * a_im
    f_re = ((ab_re - 1.0) * a_re + ab_im * a_im) / den
    f_im = (ab_im * a_re - (ab_re - 1.0) * a_im) / den
    bb_re = f_re[..., None] * b_re - f_im[..., None] * b_im
    bb_im = f_re[..., None] * b_im + f_im[..., None] * b_re
    return ab_re, ab_im, bb_re, bb_im


def _complex_affine_combine(e1, e2):
    a1r, a1i, b1r, b1i = e1
    a2r, a2i, b2r, b2i = e2
    return (a1r * a2r - a1i * a2i, a1r * a2i + a1i * a2r,
            a2r * b1r - a2i * b1i + b2r, a2r * b1i + a2i * b1r + b2i)


def _ssm_scan(u, ab_re, ab_im, bb_re, bb_im, h0_re, h0_im):
    bu_re = jnp.einsum('btgc,gpc->tbgp', u, bb_re)
    bu_im = jnp.einsum('btgc,gpc->tbgp', u, bb_im)
    bu_re = bu_re.at[0].add(ab_re * h0_re - ab_im * h0_im)
    bu_im = bu_im.at[0].add(ab_re * h0_im + ab_im * h0_re)
    t = u.shape[1]
    a_re = jnp.broadcast_to(ab_re, (t, 1) + ab_re.shape)
    a_im = jnp.broadcast_to(ab_im, (t, 1) + ab_im.shape)
    _, _, x_re, x_im = lax.associative_scan(_complex_affine_combine, (a_re, a_im, bu_re, bu_im), axis=0)
    return x_re, x_im


def _ssm_readout(x_re, x_im, c_re, c_im):
    return jnp.einsum('tbgp,gcp->btgc', x_re, c_re) - jnp.einsum('tbgp,gcp->btgc', x_im, c_im)


def _s5_mixer(h_lat, h_ctx, a_re, a_im, b_re, b_im, c_re, c_im, log_dt, d_skip, w_glu, with_ctx_out):
    b, s, d = h_lat.shape
    l = h_ctx.shape[1]
    u_lat = h_lat.reshape(b, s, SSM_GROUPS, SSM_GROUP)
    u_ctx = h_ctx.reshape(b, l, SSM_GROUPS, SSM_GROUP)
    zeros = jnp.zeros((b, SSM_GROUPS, SSM_STATE), h_lat.dtype)
    y_lat = d_skip * h_lat
    y_ctx = d_skip * h_ctx if with_ctx_out else None
    for direction in range(2):
        ab_re, ab_im, bb_re, bb_im = _zoh(a_re[direction], a_im[direction], b_re[direction], b_im[direction],
                                          log_dt[direction])
        ul = u_lat if direction == 0 else jnp.flip(u_lat, axis=1)
        uc = u_ctx if direction == 0 else jnp.flip(u_ctx, axis=1)
        xc_re, xc_im = _ssm_scan(uc, ab_re, ab_im, bb_re, bb_im, zeros, zeros)
        xl_re, xl_im = _ssm_scan(ul, ab_re, ab_im, bb_re, bb_im, xc_re[-1], xc_im[-1])
        yl = _ssm_readout(xl_re, xl_im, c_re[direction], c_im[direction])
        yl = yl if direction == 0 else jnp.flip(yl, axis=1)
        y_lat = y_lat + yl.reshape(b, s, d)
        if with_ctx_out:
            yc = _ssm_readout(xc_re, xc_im, c_re[direction], c_im[direction])
            yc = yc if direction == 0 else jnp.flip(yc, axis=1)
            y_ctx = y_ctx + yc.reshape(b, l, d)

    def glu(y):
        z = jax.nn.gelu(y)
        val, gate = jnp.split(z @ w_glu, 2, axis=-1)
        return val * jax.nn.sigmoid(gate)

    return glu(y_lat), (glu(y_ctx) if with_ctx_out else None)


def _clamped_swiglu(gu):
    glu, lin = gu[..., :D_FF], gu[..., D_FF:]
    glu = jnp.minimum(glu, SWIGLU_LIMIT)
    lin = jnp.clip(lin, -SWIGLU_LIMIT, SWIGLU_LIMIT)
    return glu * jax.nn.sigmoid(SWIGLU_ALPHA * glu) * (lin + 1.0)


def _moe_ffn(h, w_router, b_router, w_gate_up, b_gate_up, w_down, b_down):
    n_tok = h.shape[0]
    logits = (h @ w_router + b_router).astype(jnp.float32)
    top_logit, top_e = lax.top_k(logits, TOP_K)
    gate = jax.nn.softmax(top_logit, axis=-1)
    n_assign = n_tok * TOP_K
    flat_e = top_e.reshape(-1)
    order = jnp.argsort(flat_e)
    e_sorted = flat_e[order]
    tok_sorted = (order // TOP_K).astype(jnp.int32)
    gate_sorted = gate.reshape(-1)[order]
    counts = jnp.bincount(flat_e, length=N_EXPERTS)
    padded = (counts + EXPERT_BLOCK - 1) // EXPERT_BLOCK * EXPERT_BLOCK
    pad_end = jnp.cumsum(padded)
    pad_start = pad_end - padded
    start = jnp.cumsum(counts) - counts
    dest = pad_start[e_sorted] + jnp.arange(n_assign) - start[e_sorted]
    n_blocks = -(-(n_assign + N_EXPERTS * (EXPERT_BLOCK - 1)) // EXPERT_BLOCK)
    n_rows = n_blocks * EXPERT_BLOCK
    row_tok = jnp.zeros((n_rows,), jnp.int32).at[dest].set(tok_sorted)
    row_gate = jnp.zeros((n_rows,), jnp.float32).at[dest].set(gate_sorted)
    block_e = jnp.minimum(jnp.searchsorted(pad_end, jnp.arange(n_blocks) * EXPERT_BLOCK, side='right'),
                          N_EXPERTS - 1)

    def expert_block(args):
        tok, e = args
        act = _clamped_swiglu(h[tok] @ w_gate_up[e] + b_gate_up[e])
        return act @ w_down[e] + b_down[e]

    out = lax.map(expert_block, (row_tok.reshape(n_blocks, EXPERT_BLOCK), block_e))
    out = out.reshape(n_rows, -1) * row_gate[:, None].astype(h.dtype)
    return jax.ops.segment_sum(out, row_tok, num_segments=n_tok)


def setup_inputs(seed: int = 0) -> dict:
    key = jax.random.key(seed)
    keys = iter(jax.random.split(key, 64))
    D, G, P, CG, E, F = D_MODEL, SSM_GROUPS, SSM_STATE, SSM_GROUP, N_EXPERTS, D_FF

    def normal(shape, scale):
        return scale * jax.random.normal(next(keys), shape, jnp.float32)

    def gain(n):
        return 1.0 + normal((n,), 0.05)

    inputs = {
        'x': normal((BATCH, SEQ, D), 1.0),
        'c': normal((BATCH, D), 1.0),
        'ctx': normal((BATCH, CTX_LEN, D), 1.0),
        'c_ctx': normal((D,), 1.0),
    }
    for layer in range(DEPTH):
        pre = 'l%d_' % layer
        inputs[pre + 'w_mod'] = normal((D, 6 * D), 0.5 * D ** -0.5)
        inputs[pre + 'b_mod'] = normal((6 * D,), 0.02)
        inputs[pre + 'g_pre_mix'] = gain(D)
        inputs[pre + 'g_post_mix'] = gain(D)
        inputs[pre + 'g_pre_ffn'] = gain(D)
        inputs[pre + 'g_post_ffn'] = gain(D)
        if layer % 2 == 0:
            inputs[pre + 'w_in'] = normal((D, ATTN_IN_WIDTH), D ** -0.5)
            inputs[pre + 'w_out'] = normal((ATTN_OUT_WIDTH, D), ATTN_OUT_WIDTH ** -0.5)
            inputs[pre + 'lambda_q1'] = normal((HEAD_DIM,), 0.1)
            inputs[pre + 'lambda_k1'] = normal((HEAD_DIM,), 0.1)
            inputs[pre + 'lambda_q2'] = normal((HEAD_DIM,), 0.1)
            inputs[pre + 'lambda_k2'] = normal((HEAD_DIM,), 0.1)
            inputs[pre + 'g_subln'] = gain(DIFF_V_DIM)
            inputs[pre + 'g_qnorm'] = gain(HEAD_DIM)
            inputs[pre + 'g_knorm'] = gain(HEAD_DIM)
        else:
            inputs[pre + 'ssm_a_re'] = -0.5 + normal((2, G, P), 0.01)
            inputs[pre + 'ssm_a_im'] = jnp.pi * jnp.arange(P, dtype=jnp.float32) + normal((2, G, P), 0.01)
            inputs[pre + 'ssm_b_re'] = normal((2, G, P, CG), (2 * CG) ** -0.5)
            inputs[pre + 'ssm_b_im'] = normal((2, G, P, CG), (2 * CG) ** -0.5)
            inputs[pre + 'ssm_c_re'] = normal((2, G, CG, P), (2 * P) ** -0.5)
            inputs[pre + 'ssm_c_im'] = normal((2, G, CG, P), (2 * P) ** -0.5)
            inputs[pre + 'ssm_log_dt'] = jax.random.uniform(next(keys), (2, G), jnp.float32,
                                                            math.log(1e-3), math.log(1e-1))
            inputs[pre + 'ssm_d'] = normal((D,), 0.1)
            inputs[pre + 'w_glu'] = normal((D, 2 * D), D ** -0.5)
        inputs[pre + 'w_router'] = normal((D, E), D ** -0.5)
        inputs[pre + 'b_router'] = normal((E,), 0.01)
        inputs[pre + 'w_gate_up'] = normal((E, D, 2 * F), D ** -0.5)
        inputs[pre + 'b_gate_up'] = normal((E, 2 * F), 0.01)
        inputs[pre + 'w_down'] = normal((E, F, D), F ** -0.5)
        inputs[pre + 'b_down'] = normal((E, D), 0.01)
    return inputs


def reference(x, c, ctx, c_ctx,
              l0_w_mod, l0_b_mod, l0_g_pre_mix, l0_g_post_mix, l0_g_pre_ffn, l0_g_post_ffn,
              l0_w_in, l0_w_out, l0_lambda_q1, l0_lambda_k1, l0_lambda_q2, l0_lambda_k2,
              l0_g_subln, l0_g_qnorm, l0_g_knorm,
              l0_w_router, l0_b_router, l0_w_gate_up, l0_b_gate_up, l0_w_down, l0_b_down,
              l1_w_mod, l1_b_mod, l1_g_pre_mix, l1_g_post_mix, l1_g_pre_ffn, l1_g_post_ffn,
              l1_ssm_a_re, l1_ssm_a_im, l1_ssm_b_re, l1_ssm_b_im, l1_ssm_c_re, l1_ssm_c_im,
              l1_ssm_log_dt, l1_ssm_d, l1_w_glu,
              l1_w_router, l1_b_router, l1_w_gate_up, l1_b_gate_up, l1_w_down, l1_b_down):
    b, s, d = x.shape
    l = ctx.shape[1]
    rows = s // GRID_W
    cos, sin = _axial_rope_tables(rows)
    layer_params = [
        dict(w_mod=l0_w_mod, b_mod=l0_b_mod, g_pre_mix=l0_g_pre_mix, g_post_mix=l0_g_post_mix,
             g_pre_ffn=l0_g_pre_ffn, g_post_ffn=l0_g_post_ffn,
             w_in=l0_w_in, w_out=l0_w_out, lambda_q1=l0_lambda_q1, lambda_k1=l0_lambda_k1,
             lambda_q2=l0_lambda_q2, lambda_k2=l0_lambda_k2, g_subln=l0_g_subln,
             g_qnorm=l0_g_qnorm, g_knorm=l0_g_knorm,
             w_router=l0_w_router, b_router=l0_b_router, w_gate_up=l0_w_gate_up,
             b_gate_up=l0_b_gate_up, w_down=l0_w_down, b_down=l0_b_down),
        dict(w_mod=l1_w_mod, b_mod=l1_b_mod, g_pre_mix=l1_g_pre_mix, g_post_mix=l1_g_post_mix,
             g_pre_ffn=l1_g_pre_ffn, g_post_ffn=l1_g_post_ffn,
             a_re=l1_ssm_a_re, a_im=l1_ssm_a_im, b_re=l1_ssm_b_re, b_im=l1_ssm_b_im,
             c_re=l1_ssm_c_re, c_im=l1_ssm_c_im, log_dt=l1_ssm_log_dt, d_skip=l1_ssm_d, w_glu=l1_w_glu,
             w_router=l1_w_router, b_router=l1_b_router, w_gate_up=l1_w_gate_up,
             b_gate_up=l1_b_gate_up, w_down=l1_w_down, b_down=l1_b_down),
    ]
    x_lat, x_ctx = x, ctx
    for layer in range(DEPTH):
        p = layer_params[layer]
        last = layer == DEPTH - 1
        sh_m, sc_m, gt_m, sh_f, sc_f, gt_f = _modulation(c, p['w_mod'], p['b_mod'])
        csh_m, csc_m, cgt_m, csh_f, csc_f, cgt_f = _modulation(c_ctx, p['w_mod'], p['b_mod'])
        h_lat = _modulate(_rms_norm(x_lat, p['g_pre_mix']), sh_m, sc_m)
        h_ctx = _modulate(_rms_norm(x_ctx, p['g_pre_mix']), csh_m, csc_m)
        if layer % 2 == 0:
            y_lat, y_ctx = _attention_mixer(h_lat, h_ctx, cos, sin, p['w_in'], p['w_out'],
                                            p['lambda_q1'], p['lambda_k1'], p['lambda_q2'], p['lambda_k2'],
                                            p['g_subln'], p['g_qnorm'], p['g_knorm'],
                                            0.8 - 0.6 * math.exp(-0.3 * layer))
        else:
            y_lat, y_ctx = _s5_mixer(h_lat, h_ctx, p['a_re'], p['a_im'], p['b_re'], p['b_im'],
                                     p['c_re'], p['c_im'], p['log_dt'], p['d_skip'], p['w_glu'],
                                     with_ctx_out=not last)
        x_lat = x_lat + gt_m * _rms_norm(y_lat, p['g_post_mix'])
        if not last:
            x_ctx = x_ctx + cgt_m * _rms_norm(y_ctx, p['g_post_mix'])
        hf_lat = _modulate(_rms_norm(x_lat, p['g_pre_ffn']), sh_f, sc_f)
        if not last:
            hf_ctx = _modulate(_rms_norm(x_ctx, p['g_pre_ffn']), csh_f, csc_f)
            tokens = jnp.concatenate([hf_ctx, hf_lat], axis=1)
        else:
            tokens = hf_lat
        n_t = tokens.shape[1]
        f = _moe_ffn(tokens.reshape(b * n_t, d), p['w_router'], p['b_router'], p['w_gate_up'],
                     p['b_gate_up'], p['w_down'], p['b_down']).reshape(b, n_t, d)
        f = _rms_norm(f, p['g_post_ffn'])
        x_lat = x_lat + gt_f * f[:, n_t - s:]
        if not last:
            x_ctx = x_ctx + cgt_f * f[:, :l]
    return x_lat
```

```python
import functools
import math

import jax
import jax.numpy as jnp
from jax import lax
from jax.experimental import pallas as pl
from jax.experimental.pallas import tpu as pltpu

D = 2048
B = 4
SEQ = 2048
CTX = 256
TOK = CTX + SEQ
GRID_W = 64
HD = 128
DIFF_HEADS = 4
GQA_Q_HEADS = 8
GQA_KV_HEADS = 2
GQA_GROUP = GQA_Q_HEADS // GQA_KV_HEADS
ROPE_THETA = 10000.0
ROPE_FREQS = HD // 4
ATTN_IN = 4608
N_EXPERTS = 32
TOP_K = 4
D_FF = D
SWIGLU_LIMIT = 7.0
SWIGLU_ALPHA = 1.702
RMS_EPS = 1e-6
SSM_GROUP = 16
SSM_STATE = 64
SSM_GROUPS = D // SSM_GROUP
SSM_CHUNK = 16

ROW_TILE = 256
MOD_ROWS = 8
MOD_CTX_ROW = B

V7X_VMEM_BYTES = 64 * 1024 * 1024
VMEM_LIMIT = 56 * 1024 * 1024


def _cparams(*sem):
    return pltpu.CompilerParams(dimension_semantics=sem, vmem_limit_bytes=VMEM_LIMIT)


def _rms(x, g):
    return x * lax.rsqrt(jnp.mean(x * x, axis=-1, keepdims=True) + RMS_EPS) * g


def _sigmoid(x):
    return 1.0 / (1.0 + jnp.exp(-x))


def _mod_kernel(c_ref, w_ref, b_ref, o_ref):
    c = c_ref[...]
    a = (c * _sigmoid(c)).astype(jnp.bfloat16)
    o_ref[...] = jnp.dot(a, w_ref[...].astype(jnp.bfloat16),
                         preferred_element_type=jnp.float32) + b_ref[...]


def _modulation(cond, w_mod, b_mod):
    tn = 1024
    n = w_mod.shape[1]
    return pl.pallas_call(
        _mod_kernel,
        grid=(n // tn,),
        in_specs=[pl.BlockSpec((MOD_ROWS, D), lambda j: (0, 0)),
                  pl.BlockSpec((D, tn), lambda j: (0, j)),
                  pl.BlockSpec((1, tn), lambda j: (0, j))],
        out_specs=pl.BlockSpec((MOD_ROWS, tn), lambda j: (0, j)),
        out_shape=jax.ShapeDtypeStruct((MOD_ROWS, n), jnp.float32),
        compiler_params=_cparams("arbitrary"),
        name="adaln_modulation",
    )(cond, w_mod, b_mod.reshape(1, n))


def _mod_spec(which, n_ctx_blocks):
    def idx(b, i):
        r = jnp.where(i < n_ctx_blocks, MOD_CTX_ROW, b)
        return (r * 6 + which, 0, 0)
    return pl.BlockSpec((None, 1, D), idx)


def _norm_mod_kernel(x_ref, g_ref, sh_ref, sc_ref, o_ref):
    h = _rms(x_ref[...], g_ref[...]) * (1.0 + sc_ref[...]) + sh_ref[...]
    o_ref[...] = h.astype(o_ref.dtype)


def _norm_mod_router_kernel(x_ref, g_ref, sh_ref, sc_ref, wr_ref, br_ref, o_ref, l_ref):
    h = _rms(x_ref[...], g_ref[...]) * (1.0 + sc_ref[...]) + sh_ref[...]
    o_ref[...] = h.astype(o_ref.dtype)
    l_ref[...] = jnp.dot(h, wr_ref[...], preferred_element_type=jnp.float32,
                         precision=lax.Precision.HIGHEST) + br_ref[...]


def _norm_mod(x, g, mod, shift_idx, scale_idx, n_ctx_blocks, router=None):
    t = x.shape[1]
    grid = (B, t // ROW_TILE)
    row = pl.BlockSpec((None, ROW_TILE, D), lambda b, i: (b, i, 0))
    in_specs = [row, pl.BlockSpec((1, D), lambda b, i: (0, 0)),
                _mod_spec(shift_idx, n_ctx_blocks), _mod_spec(scale_idx, n_ctx_blocks)]
    args = [x, g.reshape(1, D), mod, mod]
    if router is None:
        return pl.pallas_call(
            _norm_mod_kernel, grid=grid, in_specs=in_specs, out_specs=row,
            out_shape=jax.ShapeDtypeStruct((B, t, D), jnp.bfloat16),
            compiler_params=_cparams("parallel", "parallel"), name="norm_mod",
        )(*args)
    w_router, b_router = router
    in_specs += [pl.BlockSpec((D, N_EXPERTS), lambda b, i: (0, 0)),
                 pl.BlockSpec((1, N_EXPERTS), lambda b, i: (0, 0))]
    args += [w_router, b_router.reshape(1, N_EXPERTS)]
    return pl.pallas_call(
        _norm_mod_router_kernel, grid=grid, in_specs=in_specs,
        out_specs=[row, pl.BlockSpec((None, ROW_TILE, N_EXPERTS), lambda b, i: (b, i, 0))],
        out_shape=[jax.ShapeDtypeStruct((B, t, D), jnp.bfloat16),
                   jax.ShapeDtypeStruct((B, t, N_EXPERTS), jnp.float32)],
        compiler_params=_cparams("parallel", "parallel"), name="norm_mod_router",
    )(*args)


def _post_norm_kernel(x_ref, y_ref, g_ref, gt_ref, o_ref):
    o_ref[...] = x_ref[...] + gt_ref[...] * _rms(y_ref[...], g_ref[...])


def _post_norm_residual(x, y, g, mod, gate_idx, n_ctx_blocks):
    t = x.shape[1]
    row = pl.BlockSpec((None, ROW_TILE, D), lambda b, i: (b, i, 0))
    return pl.pallas_call(
        _post_norm_kernel, grid=(B, t // ROW_TILE),
        in_specs=[row, row, pl.BlockSpec((1, D), lambda b, i: (0, 0)),
                  _mod_spec(gate_idx, n_ctx_blocks)],
        out_specs=row,
        out_shape=jax.ShapeDtypeStruct((B, t, D), jnp.float32),
        input_output_aliases={0: 0},
        compiler_params=_cparams("parallel", "parallel"), name="post_norm_residual",
    )(x, y, g.reshape(1, D), mod)


def _matmul_kernel(a_ref, w_ref, o_ref):
    o_ref[...] = jnp.dot(a_ref[...], w_ref[...].astype(jnp.bfloat16),
                         preferred_element_type=jnp.float32).astype(o_ref.dtype)


def _matmul(a, w, tm, tn, out_dtype=jnp.float32, name="matmul"):
    m, k = a.shape
    n = w.shape[1]
    return pl.pallas_call(
        _matmul_kernel, grid=(m // tm, n // tn),
        in_specs=[pl.BlockSpec((tm, k), lambda i, j: (i, 0)),
                  pl.BlockSpec((k, tn), lambda i, j: (0, j))],
        out_specs=pl.BlockSpec((tm, tn), lambda i, j: (i, j)),
        out_shape=jax.ShapeDtypeStruct((m, n), out_dtype),
        compiler_params=_cparams("parallel", "arbitrary"), name=name,
    )(a, w)


def _glu_matmul_kernel(a_ref, wv_ref, wg_ref, o_ref):
    a = a_ref[...]
    val = jnp.dot(a, wv_ref[...].astype(jnp.bfloat16), preferred_element_type=jnp.float32)
    gate = jnp.dot(a, wg_ref[...].astype(jnp.bfloat16), preferred_element_type=jnp.float32)
    o_ref[...] = val * _sigmoid(gate)


def _glu_matmul(a, w_glu, tm, tn):
    m, k = a.shape
    n = w_glu.shape[1] // 2
    nj = n // tn
    return pl.pallas_call(
        _glu_matmul_kernel, grid=(m // tm, nj),
        in_specs=[pl.BlockSpec((tm, k), lambda i, j: (i, 0)),
                  pl.BlockSpec((k, tn), lambda i, j: (0, j)),
                  pl.BlockSpec((k, tn), lambda i, j: (0, nj + j))],
        out_specs=pl.BlockSpec((tm, tn), lambda i, j: (i, j)),
        out_shape=jax.ShapeDtypeStruct((m, n), jnp.float32),
        compiler_params=_cparams("parallel", "arbitrary"), name="glu_matmul",
    )(a, w_glu, w_glu)


IN_TN = 256
IN_ROPE_END = 8
IN_DV_END = 12
IN_GQ_END = 16
IN_GK_TILE = 16


def _inproj_kernel(a_ref, w_ref, cos_ref, sin_ref, gq_ref, gk_ref, o_ref):
    j = pl.program_id(1)
    acc = jnp.dot(a_ref[...], w_ref[...].astype(jnp.bfloat16), preferred_element_type=jnp.float32)

    def rope(x):
        return x * cos_ref[...] + pltpu.roll(x, HD // 2, 1) * sin_ref[...]

    def store(fn):
        for c in range(IN_TN // HD):
            o_ref[:, c * HD:(c + 1) * HD] = fn(acc[:, c * HD:(c + 1) * HD]).astype(o_ref.dtype)

    @pl.when(j < IN_ROPE_END)
    def _():
        store(rope)

    @pl.when(jnp.logical_or(jnp.logical_and(j >= IN_ROPE_END, j < IN_DV_END), j > IN_GK_TILE))
    def _():
        store(lambda x: x)

    @pl.when(jnp.logical_and(j >= IN_DV_END, j < IN_GQ_END))
    def _():
        store(lambda x: rope(_rms(x, gq_ref[...])))

    @pl.when(j == IN_GK_TILE)
    def _():
        store(lambda x: rope(_rms(x, gk_ref[...])))


def _inproj(h, w_in, cos2, sin2, g_q, g_k):
    m = h.shape[0]
    const = lambda i, j: (0, 0)
    return pl.pallas_call(
        _inproj_kernel, grid=(m // TOK, ATTN_IN // IN_TN),
        in_specs=[pl.BlockSpec((TOK, D), lambda i, j: (i, 0)),
                  pl.BlockSpec((D, IN_TN), lambda i, j: (0, j)),
                  pl.BlockSpec((TOK, HD), const), pl.BlockSpec((TOK, HD), const),
                  pl.BlockSpec((1, HD), const), pl.BlockSpec((1, HD), const)],
        out_specs=pl.BlockSpec((TOK, IN_TN), lambda i, j: (i, j)),
        out_shape=jax.ShapeDtypeStruct((m, ATTN_IN), jnp.bfloat16),
        compiler_params=_cparams("parallel", "arbitrary"), name="attn_inproj",
    )(h, w_in, cos2, sin2, g_q.reshape(1, HD), g_k.reshape(1, HD))


ATT_TQ = 256
ATT_SCALE = HD ** -0.5


def _softmax_pv(q, k, v):
    s = lax.dot_general(q, k, (((1,), (1,)), ((), ())), preferred_element_type=jnp.float32)
    m = jnp.max(s, axis=-1, keepdims=True)
    e = jnp.exp((s - m) * ATT_SCALE)
    l = jnp.sum(e, axis=-1, keepdims=True)
    return jnp.dot(e.astype(jnp.bfloat16), v, preferred_element_type=jnp.float32), l


def _diff_attn_kernel(lam_ref, q_ref, k_ref, v_ref, g_ref, o_ref, *, lambda_init):
    lp = lam_ref[...]
    lam = (jnp.exp(jnp.sum(lp[0:1] * lp[1:2], axis=-1, keepdims=True))
           - jnp.exp(jnp.sum(lp[2:3] * lp[3:4], axis=-1, keepdims=True)) + lambda_init)

    def run(nk):
        q = q_ref[...]
        k = k_ref[0:nk, :]
        v = v_ref[0:nk, :]
        pv1, l1 = _softmax_pv(q[:, :HD], k[:, :HD], v)
        pv2, l2 = _softmax_pv(q[:, HD:], k[:, HD:], v)
        o = pv1 / l1 - lam * (pv2 / l2)
        o_ref[...] = (_rms(o, g_ref[...]) * (1.0 - lambda_init)).astype(o_ref.dtype)

    @pl.when(pl.program_id(2) == 0)
    def _():
        run(CTX)

    @pl.when(pl.program_id(2) > 0)
    def _():
        run(TOK)


def _gqa_attn_kernel(q_ref, k_ref, v_ref, o_ref):
    def run(nk):
        k = k_ref[0:nk, :]
        v = v_ref[0:nk, :]
        for g in range(GQA_GROUP):
            pv, l = _softmax_pv(q_ref[:, g * HD:(g + 1) * HD], k, v)
            o_ref[:, g * HD:(g + 1) * HD] = (pv / l).astype(o_ref.dtype)

    @pl.when(pl.program_id(2) == 0)
    def _():
        run(CTX)

    @pl.when(pl.program_id(2) > 0)
    def _():
        run(TOK)


def _attention(p, lam_params, g_subln, lambda_init):
    nq = TOK // ATT_TQ
    dv = 2 * HD
    od = pl.pallas_call(
        functools.partial(_diff_attn_kernel, lambda_init=lambda_init),
        grid=(B, DIFF_HEADS, nq),
        in_specs=[pl.BlockSpec((4, HD), lambda b, h, i: (0, 0)),
                  pl.BlockSpec((None, ATT_TQ, dv), lambda b, h, i: (b, i, h)),
                  pl.BlockSpec((None, TOK, dv), lambda b, h, i: (b, 0, DIFF_HEADS + h)),
                  pl.BlockSpec((None, TOK, dv), lambda b, h, i: (b, 0, 2 * DIFF_HEADS + h)),
                  pl.BlockSpec((1, dv), lambda b, h, i: (0, 0))],
        out_specs=pl.BlockSpec((None, ATT_TQ, dv), lambda b, h, i: (b, i, h)),
        out_shape=jax.ShapeDtypeStruct((B, TOK, DIFF_HEADS * dv), jnp.bfloat16),
        compiler_params=_cparams("parallel", "parallel", "arbitrary"), name="diff_attention",
    )(lam_params, p, p, p, g_subln.reshape(1, dv))
    gq_w = GQA_GROUP * HD
    gq0 = 3072 // gq_w
    gk0 = 4096 // HD
    gv0 = 4352 // HD
    og = pl.pallas_call(
        _gqa_attn_kernel,
        grid=(B, GQA_KV_HEADS, nq),
        in_specs=[pl.BlockSpec((None, ATT_TQ, gq_w), lambda b, n, i: (b, i, gq0 + n)),
                  pl.BlockSpec((None, TOK, HD), lambda b, n, i: (b, 0, gk0 + n)),
                  pl.BlockSpec((None, TOK, HD), lambda b, n, i: (b, 0, gv0 + n))],
        out_specs=pl.BlockSpec((None, ATT_TQ, gq_w), lambda b, n, i: (b, i, n)),
        out_shape=jax.ShapeDtypeStruct((B, TOK, GQA_Q_HEADS * HD), jnp.bfloat16),
        compiler_params=_cparams("parallel", "parallel", "arbitrary"), name="gqa_attention",
    )(p, p, p)
    return jnp.concatenate([od, og], axis=-1)


def _rope_tables():
    rows = SEQ // GRID_W
    row_id, col_id = jnp.meshgrid(jnp.arange(rows), jnp.arange(GRID_W), indexing="ij")
    inv_freq = ROPE_THETA ** (-jnp.arange(ROPE_FREQS, dtype=jnp.float32) / ROPE_FREQS)
    ang = jnp.concatenate([row_id.reshape(-1, 1) * inv_freq, col_id.reshape(-1, 1) * inv_freq], axis=-1)
    cos, sin = jnp.cos(ang), jnp.sin(ang)
    cos2 = jnp.concatenate([cos, cos], axis=-1)
    sin2 = jnp.concatenate([-sin, sin], axis=-1)
    cos2 = jnp.concatenate([jnp.ones((CTX, HD), jnp.float32), cos2], axis=0)
    sin2 = jnp.concatenate([jnp.zeros((CTX, HD), jnp.float32), sin2], axis=0)
    return cos2, sin2


MOE_TM = 512
MOE_SUB = 256
MOE_TF = 512
MOE_TN = 1024


def _moe_up_kernel(be_ref, first_ref, rows_ref, x_ref, wg_ref, wl_ref, bg_ref, bl_ref, o_ref,
                   wg_bf, wl_bf):
    b = pl.program_id(1)

    @pl.when(first_ref[b] == 1)
    def _():
        wg_bf[...] = wg_ref[...].astype(jnp.bfloat16)
        wl_bf[...] = wl_ref[...].astype(jnp.bfloat16)

    for s in range(MOE_TM // MOE_SUB):
        sl = slice(s * MOE_SUB, (s + 1) * MOE_SUB)

        @pl.when(rows_ref[b] > s * MOE_SUB)
        def _():
            x = x_ref[sl, :]
            glu = jnp.dot(x, wg_bf[...], preferred_element_type=jnp.float32) + bg_ref[...]
            lin = jnp.dot(x, wl_bf[...], preferred_element_type=jnp.float32) + bl_ref[...]
            glu = jnp.minimum(glu, SWIGLU_LIMIT)
            lin = jnp.clip(lin, -SWIGLU_LIMIT, SWIGLU_LIMIT)
            o_ref[sl, :] = (glu * _sigmoid(SWIGLU_ALPHA * glu) * (lin + 1.0)).astype(o_ref.dtype)

        @pl.when(rows_ref[b] <= s * MOE_SUB)
        def _():
            o_ref[sl, :] = jnp.zeros((MOE_SUB, MOE_TF), o_ref.dtype)


def _moe_down_kernel(be_ref, first_ref, rows_ref, a_ref, w_ref, bias_ref, o_ref, w_bf):
    b = pl.program_id(1)

    @pl.when(first_ref[b] == 1)
    def _():
        w_bf[...] = w_ref[...].astype(jnp.bfloat16)

    for s in range(MOE_TM // MOE_SUB):
        sl = slice(s * MOE_SUB, (s + 1) * MOE_SUB)

        @pl.when(rows_ref[b] > s * MOE_SUB)
        def _():
            o_ref[sl, :] = jnp.dot(a_ref[sl, :], w_bf[...],
                                   preferred_element_type=jnp.float32) + bias_ref[...]

        @pl.when(rows_ref[b] <= s * MOE_SUB)
        def _():
            o_ref[sl, :] = jnp.zeros((MOE_SUB, MOE_TN), o_ref.dtype)


def _moe_experts(x_sorted, block_e, block_first, block_rows, w_gate_up, b_gate_up, w_down, b_down):
    r = x_sorted.shape[0]
    nb = r // MOE_TM
    nf = D_FF // MOE_TF
    bgu = b_gate_up.reshape(N_EXPERTS, 1, 2 * D_FF)
    act = pl.pallas_call(
        _moe_up_kernel,
        grid_spec=pltpu.PrefetchScalarGridSpec(
            num_scalar_prefetch=3, grid=(nf, nb),
            in_specs=[pl.BlockSpec((MOE_TM, D), lambda f, b, be, fi, ro: (b, 0)),
                      pl.BlockSpec((None, D, MOE_TF), lambda f, b, be, fi, ro: (be[b], 0, f)),
                      pl.BlockSpec((None, D, MOE_TF), lambda f, b, be, fi, ro: (be[b], 0, nf + f)),
                      pl.BlockSpec((None, 1, MOE_TF), lambda f, b, be, fi, ro: (be[b], 0, f)),
                      pl.BlockSpec((None, 1, MOE_TF), lambda f, b, be, fi, ro: (be[b], 0, nf + f))],
            out_specs=pl.BlockSpec((MOE_TM, MOE_TF), lambda f, b, be, fi, ro: (b, f)),
            scratch_shapes=[pltpu.VMEM((D, MOE_TF), jnp.bfloat16), pltpu.VMEM((D, MOE_TF), jnp.bfloat16)]),
        out_shape=jax.ShapeDtypeStruct((r, D_FF), jnp.bfloat16),
        compiler_params=_cparams("arbitrary", "arbitrary"), name="moe_gate_up",
    )(block_e, block_first, block_rows, x_sorted, w_gate_up, w_gate_up, bgu, bgu)
    nn = D // MOE_TN
    return pl.pallas_call(
        _moe_down_kernel,
        grid_spec=pltpu.PrefetchScalarGridSpec(
            num_scalar_prefetch=3, grid=(nn, nb),
            in_specs=[pl.BlockSpec((MOE_TM, D_FF), lambda n, b, be, fi, ro: (b, 0)),
                      pl.BlockSpec((None, D_FF, MOE_TN), lambda n, b, be, fi, ro: (be[b], 0, n)),
                      pl.BlockSpec((None, 1, MOE_TN), lambda n, b, be, fi, ro: (be[b], 0, n))],
            out_specs=pl.BlockSpec((MOE_TM, MOE_TN), lambda n, b, be, fi, ro: (b, n)),
            scratch_shapes=[pltpu.VMEM((D_FF, MOE_TN), jnp.bfloat16)]),
        out_shape=jax.ShapeDtypeStruct((r, D), jnp.float32),
        compiler_params=_cparams("arbitrary", "arbitrary"), name="moe_down",
    )(block_e, block_first, block_rows, act, w_down, b_down.reshape(N_EXPERTS, 1, D))


def _moe_ffn(h, logits, w_gate_up, b_gate_up, w_down, b_down):
    n_tok = h.shape[0]
    n_assign = n_tok * TOP_K
    top_logit, top_e = lax.top_k(logits, TOP_K)
    gate = jax.nn.softmax(top_logit, axis=-1)
    flat_e = top_e.reshape(-1)
    onehot = (flat_e[:, None] == jnp.arange(N_EXPERTS, dtype=flat_e.dtype)[None, :]).astype(jnp.int32)
    csum = jnp.cumsum(onehot, axis=0)
    rank = jnp.sum((csum - onehot) * onehot, axis=1)
    counts = csum[-1]
    padded = (counts + MOE_TM - 1) // MOE_TM * MOE_TM
    pad_end = jnp.cumsum(padded)
    pad_start = pad_end - padded
    dest = pad_start[flat_e] + rank
    nb = -(-(n_assign + N_EXPERTS * (MOE_TM - 1)) // MOE_TM)
    n_rows = nb * MOE_TM
    tok = (jnp.arange(n_assign, dtype=jnp.int32) // TOP_K)
    row_tok = jnp.zeros((n_rows,), jnp.int32).at[dest].set(tok)
    block_start = jnp.arange(nb, dtype=jnp.int32) * MOE_TM
    block_e_raw = jnp.minimum(jnp.searchsorted(pad_end, block_start, side="right"), N_EXPERTS - 1)
    block_e_raw = block_e_raw.astype(jnp.int32)
    valid_end = pad_start + counts
    block_rows = jnp.clip(valid_end[block_e_raw] - block_start, 0, MOE_TM).astype(jnp.int32)
    block_rows = jnp.where(block_start < pad_end[-1], block_rows, 0)
    last_used = jnp.maximum(pad_end[-1] // MOE_TM - 1, 0)
    block_e = block_e_raw[jnp.minimum(jnp.arange(nb), last_used)]
    block_first = jnp.concatenate([jnp.ones((1,), jnp.int32),
                                   (block_e[1:] != block_e[:-1]).astype(jnp.int32)])
    x_sorted = jnp.take(h, row_tok, axis=0)
    out_sorted = _moe_experts(x_sorted, block_e, block_first, block_rows,
                              w_gate_up, b_gate_up, w_down, b_down)
    picked = jnp.take(out_sorted, dest, axis=0).reshape(n_tok, TOP_K, D)
    return jnp.sum(picked * gate[:, :, None], axis=1)


S5_GB = 8
S5_LANES = SSM_CHUNK * SSM_GROUP
S5_NK_CTX = CTX // SSM_CHUNK
S5_NK_LAT = SEQ // SSM_CHUNK
S5_PAIR = 2 * B


def _s5_direction(reverse, uc_ref, ul_ref, wb_ref, m_ref, wc_ref, coef_ref, y_ref,
                  s1c, s2c, s1l, s2l, xin, accumulate):
    half = 2 * SSM_STATE
    for g in range(S5_GB):
        sc = jnp.dot(uc_ref[g], wb_ref[g], preferred_element_type=jnp.float32)
        s1c[g] = sc[:, :half]
        s2c[g] = sc[:, half:]
        sl = jnp.dot(ul_ref[g], wb_ref[g], preferred_element_type=jnp.float32)
        s1l[g] = sl[:, :half]
        s2l[g] = sl[:, half:]

    lower = lax.broadcasted_iota(jnp.int32, (S5_PAIR, half), 0) < B
    p1 = [jnp.broadcast_to(coef_ref[g, 0:1, :], (S5_PAIR, half)) for g in range(S5_GB)]
    p2 = [jnp.broadcast_to(coef_ref[g, 1:2, :], (S5_PAIR, half)) for g in range(S5_GB)]

    def tile_step(g, t1, t2, v1, v2):
        y1a = p1[g] * v1 + p2[g] * v2 + t1
        y1b = p1[g] * v2 - p2[g] * v1 + t2
        r1a = pltpu.roll(y1a, B, 0)
        r1b = pltpu.roll(y1b, B, 0)
        y2a = p1[g] * r1a + p2[g] * r1b + t1
        y2b = p1[g] * r1b - p2[g] * r1a + t2
        r2a = pltpu.roll(y2a, B, 0)
        r2b = pltpu.roll(y2b, B, 0)
        if not reverse:
            x_in = jnp.where(lower, v1, r1a)
            return x_in, jnp.where(lower, r2a, y2a), jnp.where(lower, r2b, y2b)
        x_in = jnp.where(lower, r1a, v1)
        return x_in, jnp.where(lower, y2a, r2a), jnp.where(lower, y2b, r2b)

    def scan(s1, s2, n_tiles, state, record):
        def body(j, carry):
            jj = (n_tiles - 1 - j) if reverse else j
            r0 = pl.multiple_of(jj * S5_PAIR, S5_PAIR)
            new = []
            for g in range(S5_GB):
                v1, v2 = carry[2 * g], carry[2 * g + 1]
                x_in, v1, v2 = tile_step(g, s1[g, pl.ds(r0, S5_PAIR), :], s2[g, pl.ds(r0, S5_PAIR), :], v1, v2)
                if record:
                    xin[g, pl.ds(r0, S5_PAIR), :] = x_in
                new += [v1, v2]
            return tuple(new)
        return lax.fori_loop(0, n_tiles, body, state)

    zero = jnp.zeros((S5_PAIR, half), jnp.float32)
    state = tuple(zero for _ in range(2 * S5_GB))
    state = scan(s1c, s2c, S5_NK_CTX * B // S5_PAIR, state, False)
    scan(s1l, s2l, S5_NK_LAT * B // S5_PAIR, state, True)

    for g in range(S5_GB):
        y = (jnp.dot(ul_ref[g], m_ref[g], preferred_element_type=jnp.float32)
             + jnp.dot(xin[g].astype(jnp.bfloat16), wc_ref[g], preferred_element_type=jnp.float32))
        if accumulate:
            y_ref[g] = y_ref[g] + y
        else:
            y_ref[g] = y


def _s5_kernel(uc_ref, ul_ref, wb_ref, m_ref, wc_ref, coef_ref, y_ref, s1c, s2c, s1l, s2l, xin):
    args = (uc_ref, ul_ref, wb_ref, m_ref, wc_ref, coef_ref, y_ref, s1c, s2c, s1l, s2l, xin)

    @pl.when(pl.program_id(1) == 0)
    def _():
        _s5_direction(False, *args, accumulate=False)

    @pl.when(pl.program_id(1) == 1)
    def _():
        _s5_direction(True, *args, accumulate=True)


def _s5_scan(u_ctx, u_lat, wb, m, wc, coef):
    rc, rl = u_ctx.shape[1], u_lat.shape[1]
    half = 2 * SSM_STATE
    wspec = lambda k, n: pl.BlockSpec((None, S5_GB, k, n), lambda gi, d: (d, gi, 0, 0))
    return pl.pallas_call(
        _s5_kernel, grid=(SSM_GROUPS // S5_GB, 2),
        in_specs=[pl.BlockSpec((S5_GB, rc, S5_LANES), lambda gi, d: (gi, 0, 0)),
                  pl.BlockSpec((S5_GB, rl, S5_LANES), lambda gi, d: (gi, 0, 0)),
                  wspec(S5_LANES, 2 * half), wspec(S5_LANES, S5_LANES), wspec(half, S5_LANES),
                  wspec(2, half)],
        out_specs=pl.BlockSpec((S5_GB, rl, S5_LANES), lambda gi, d: (gi, 0, 0)),
        out_shape=jax.ShapeDtypeStruct((SSM_GROUPS, rl, S5_LANES), jnp.float32),
        scratch_shapes=[pltpu.VMEM((S5_GB, rc, half), jnp.float32), pltpu.VMEM((S5_GB, rc, half), jnp.float32),
                        pltpu.VMEM((S5_GB, rl, half), jnp.float32), pltpu.VMEM((S5_GB, rl, half), jnp.float32),
                        pltpu.VMEM((S5_GB, rl, half), jnp.float32)],
        compiler_params=_cparams("parallel", "arbitrary"), name="s5_scan",
    )(u_ctx, u_lat, wb, m, wc, coef)


def _s5_matrices(a_re, a_im, b_re, b_im, c_re, c_im, log_dt, reverse):
    hp = lax.Precision.HIGHEST
    n = SSM_CHUNK
    dt = jnp.exp(log_dt)[:, None]
    mag = jnp.exp(a_re * dt)
    ab_re, ab_im = mag * jnp.cos(a_im * dt), mag * jnp.sin(a_im * dt)
    den = a_re * a_re + a_im * a_im
    f_re = ((ab_re - 1.0) * a_re + ab_im * a_im) / den
    f_im = (ab_im * a_re - (ab_re - 1.0) * a_im) / den
    bb_re = f_re[..., None] * b_re - f_im[..., None] * b_im
    bb_im = f_re[..., None] * b_im + f_im[..., None] * b_re
    tau = jnp.arange(n + 1, dtype=jnp.float32)[:, None, None]
    pmag = jnp.exp(tau * (a_re * dt))
    pw_re, pw_im = pmag * jnp.cos(tau * (a_im * dt)), pmag * jnp.sin(tau * (a_im * dt))
    ca_re = c_re[None] * pw_re[:n, :, None, :] - c_im[None] * pw_im[:n, :, None, :]
    ca_im = c_re[None] * pw_im[:n, :, None, :] + c_im[None] * pw_re[:n, :, None, :]
    kern = (jnp.einsum("tgcp,gpd->tgcd", ca_re, bb_re, precision=hp)
            - jnp.einsum("tgcp,gpd->tgcd", ca_im, bb_im, precision=hp))
    s_idx = jnp.arange(n)[:, None]
    t_idx = jnp.arange(n)[None, :]
    lag = (s_idx - t_idx) if reverse else (t_idx - s_idx)
    mk = kern[jnp.clip(lag, 0, n - 1)] * (lag >= 0)[:, :, None, None, None]
    m = mk.transpose(2, 0, 4, 1, 3).reshape(SSM_GROUPS, S5_LANES, S5_LANES)
    e_idx = jnp.arange(n) if reverse else (n - 1 - jnp.arange(n))
    ae_re, ae_im = pw_re[e_idx], pw_im[e_idx]
    wb_re = ae_re[:, :, :, None] * bb_re[None] - ae_im[:, :, :, None] * bb_im[None]
    wb_im = ae_re[:, :, :, None] * bb_im[None] + ae_im[:, :, :, None] * bb_re[None]
    wb_re = wb_re.transpose(1, 0, 3, 2).reshape(SSM_GROUPS, S5_LANES, SSM_STATE)
    wb_im = wb_im.transpose(1, 0, 3, 2).reshape(SSM_GROUPS, S5_LANES, SSM_STATE)
    wb = jnp.concatenate([wb_re, wb_im, wb_im, wb_re], axis=-1)
    f_idx = (n - jnp.arange(n)) if reverse else (jnp.arange(n) + 1)
    af_re, af_im = pw_re[f_idx], pw_im[f_idx]
    cf_re = c_re[None] * af_re[:, :, None, :] - c_im[None] * af_im[:, :, None, :]
    cf_im = c_re[None] * af_im[:, :, None, :] + c_im[None] * af_re[:, :, None, :]
    wc = jnp.concatenate([cf_re.transpose(1, 3, 0, 2), -cf_im.transpose(1, 3, 0, 2)], axis=1)
    wc = wc.reshape(SSM_GROUPS, 2 * SSM_STATE, S5_LANES)
    an_re, an_im = pw_re[n], pw_im[n]
    coef = jnp.stack([jnp.concatenate([an_re, an_re], axis=-1),
                      jnp.concatenate([-an_im, an_im], axis=-1)], axis=1)
    return wb.astype(jnp.bfloat16), m.astype(jnp.bfloat16), wc.astype(jnp.bfloat16), coef


def _s5_chunks(h):
    t = h.shape[1]
    nk = t // SSM_CHUNK
    u = h.reshape(B, nk, SSM_CHUNK, SSM_GROUPS, SSM_GROUP).transpose(3, 1, 0, 2, 4)
    return u.reshape(SSM_GROUPS, nk * B, S5_LANES)


def _s5_unchunk(y):
    nk = y.shape[1] // B
    y = y.reshape(SSM_GROUPS, nk, B, SSM_CHUNK, SSM_GROUP).transpose(2, 1, 3, 0, 4)
    return y.reshape(B, nk * SSM_CHUNK, D)


def _s5_post_kernel(x_ref, y_ref, g_ref, sh_ref, sc_ref, d_ref, o_ref):
    h = _rms(x_ref[...], g_ref[...]) * (1.0 + sc_ref[...]) + sh_ref[...]
    y = d_ref[...] * h + y_ref[...]
    z = 0.5 * y * (1.0 + jnp.tanh(math.sqrt(2.0 / math.pi) * (y + 0.044715 * (y * y * y))))
    o_ref[...] = z.astype(o_ref.dtype)


def _s5_post(x_lat, y_ssm, g, mod, d_skip):
    row = pl.BlockSpec((None, ROW_TILE, D), lambda b, i: (b, i, 0))
    vec = pl.BlockSpec((1, D), lambda b, i: (0, 0))
    return pl.pallas_call(
        _s5_post_kernel, grid=(B, SEQ // ROW_TILE),
        in_specs=[row, row, vec, _mod_spec(0, 0), _mod_spec(1, 0), vec],
        out_specs=row,
        out_shape=jax.ShapeDtypeStruct((B, SEQ, D), jnp.bfloat16),
        compiler_params=_cparams("parallel", "parallel"), name="s5_skip_gelu",
    )(x_lat, y_ssm, g.reshape(1, D), mod, mod, d_skip.reshape(1, D))


def _layer_modulation(c, c_ctx, w_mod, b_mod):
    cond = jnp.concatenate([c, c_ctx[None, :], jnp.zeros((MOD_ROWS - B - 1, D), jnp.float32)], axis=0)
    return _modulation(cond, w_mod, b_mod).reshape(MOD_ROWS * 6, 1, D)


def kernel(x, c, ctx, c_ctx, l0_w_mod, l0_b_mod, l0_g_pre_mix, l0_g_post_mix, l0_g_pre_ffn, l0_g_post_ffn, l0_w_in, l0_w_out, l0_lambda_q1, l0_lambda_k1, l0_lambda_q2, l0_lambda_k2, l0_g_subln, l0_g_qnorm, l0_g_knorm, l0_w_router, l0_b_router, l0_w_gate_up, l0_b_gate_up, l0_w_down, l0_b_down, l1_w_mod, l1_b_mod, l1_g_pre_mix, l1_g_post_mix, l1_g_pre_ffn, l1_g_post_ffn, l1_ssm_a_re, l1_ssm_a_im, l1_ssm_b_re, l1_ssm_b_im, l1_ssm_c_re, l1_ssm_c_im, l1_ssm_log_dt, l1_ssm_d, l1_w_glu, l1_w_router, l1_b_router, l1_w_gate_up, l1_b_gate_up, l1_w_down, l1_b_down):
    xs = jnp.concatenate([ctx, x], axis=1)

    mod = _layer_modulation(c, c_ctx, l0_w_mod, l0_b_mod)
    h = _norm_mod(xs, l0_g_pre_mix, mod, 0, 1, 1)
    cos2, sin2 = _rope_tables()
    p = _inproj(h.reshape(B * TOK, D), l0_w_in, cos2, sin2, l0_g_qnorm, l0_g_knorm)
    lam_params = jnp.stack([l0_lambda_q1, l0_lambda_k1, l0_lambda_q2, l0_lambda_k2])
    lambda_init = 0.8 - 0.6 * math.exp(-0.3 * 0)
    att = _attention(p.reshape(B, TOK, ATTN_IN), lam_params, l0_g_subln, lambda_init)
    y = _matmul(att.reshape(B * TOK, D), l0_w_out, 1024, 512, name="attn_outproj")
    xs = _post_norm_residual(xs, y.reshape(B, TOK, D), l0_g_post_mix, mod, 2, 1)
    hf, logits = _norm_mod(xs, l0_g_pre_ffn, mod, 3, 4, 1, router=(l0_w_router, l0_b_router))
    f = _moe_ffn(hf.reshape(B * TOK, D), logits.reshape(B * TOK, N_EXPERTS),
                 l0_w_gate_up, l0_b_gate_up, l0_w_down, l0_b_down)
    xs = _post_norm_residual(xs, f.reshape(B, TOK, D), l0_g_post_ffn, mod, 5, 1)

    mod = _layer_modulation(c, c_ctx, l1_w_mod, l1_b_mod)
    h = _norm_mod(xs, l1_g_pre_mix, mod, 0, 1, 1)
    u_ctx = _s5_chunks(h[:, :CTX])
    u_lat = _s5_chunks(h[:, CTX:])
    mats = [_s5_matrices(l1_ssm_a_re[d], l1_ssm_a_im[d], l1_ssm_b_re[d], l1_ssm_b_im[d],
                         l1_ssm_c_re[d], l1_ssm_c_im[d], l1_ssm_log_dt[d], reverse=bool(d))
            for d in range(2)]
    wb, m, wc, coef = (jnp.stack([mats[0][i], mats[1][i]]) for i in range(4))
    y_ssm = _s5_unchunk(_s5_scan(u_ctx, u_lat, wb, m, wc, coef))
    x_lat = xs[:, CTX:]
    z = _s5_post(x_lat, y_ssm, l1_g_pre_mix, mod, l1_ssm_d)
    y = _glu_matmul(z.reshape(B * SEQ, D), l1_w_glu, 1024, 512)
    x_lat = _post_norm_residual(x_lat, y.reshape(B, SEQ, D), l1_g_post_mix, mod, 2, 0)
    hf, logits = _norm_mod(x_lat, l1_g_pre_ffn, mod, 3, 4, 0, router=(l1_w_router, l1_b_router))
    f = _moe_ffn(hf.reshape(B * SEQ, D), logits.reshape(B * SEQ, N_EXPERTS),
                 l1_w_gate_up, l1_b_gate_up, l1_w_down, l1_b_down)
    return _post_norm_residual(x_lat, f.reshape(B, SEQ, D), l1_g_post_ffn, mod, 5, 0)
```

```python
import functools
import math

import jax
import jax.numpy as jnp
from jax import lax
from jax.experimental import pallas as pl
from jax.experimental.pallas import tpu as pltpu

D = 2048
B = 4
SEQ = 2048
CTX = 256
TOK = CTX + SEQ
GRID_W = 64
HD = 128
DIFF_HEADS = 4
GQA_Q_HEADS = 8
GQA_KV_HEADS = 2
GQA_GROUP = GQA_Q_HEADS // GQA_KV_HEADS
ROPE_THETA = 10000.0
ROPE_FREQS = HD // 4
ATTN_IN = 4608
N_EXPERTS = 32
TOP_K = 4
D_FF = D
SWIGLU_LIMIT = 7.0
SWIGLU_ALPHA = 1.702
RMS_EPS = 1e-6
SSM_GROUP = 16
SSM_STATE = 64
SSM_GROUPS = D // SSM_GROUP
SSM_CHUNK = 16

ROW_TILE = 256
MOD_ROWS = 8
MOD_CTX_ROW = B

V7X_VMEM_BYTES = 64 * 1024 * 1024
VMEM_LIMIT = 56 * 1024 * 1024


def _cparams(*sem):
    return pltpu.CompilerParams(dimension_semantics=sem, vmem_limit_bytes=VMEM_LIMIT)


def _rms(x, g):
    return x * lax.rsqrt(jnp.mean(x * x, axis=-1, keepdims=True) + RMS_EPS) * g


def _sigmoid(x):
    return 1.0 / (1.0 + jnp.exp(-x))


def _mod_kernel(c_ref, w_ref, b_ref, o_ref):
    c = c_ref[...]
    a = (c * _sigmoid(c)).astype(jnp.bfloat16)
    o_ref[...] = jnp.dot(a, w_ref[...].astype(jnp.bfloat16),
                         preferred_element_type=jnp.float32) + b_ref[...]


def _modulation(cond, w_mod, b_mod):
    tn = 1024
    n = w_mod.shape[1]
    return pl.pallas_call(
        _mod_kernel,
        grid=(n // tn,),
        in_specs=[pl.BlockSpec((MOD_ROWS, D), lambda j: (0, 0)),
                  pl.BlockSpec((D, tn), lambda j: (0, j)),
                  pl.BlockSpec((1, tn), lambda j: (0, j))],
        out_specs=pl.BlockSpec((MOD_ROWS, tn), lambda j: (0, j)),
        out_shape=jax.ShapeDtypeStruct((MOD_ROWS, n), jnp.float32),
        compiler_params=_cparams("arbitrary"),
        name="adaln_modulation",
    )(cond, w_mod, b_mod.reshape(1, n))


def _mod_spec(which, n_ctx_blocks):
    def idx(b, i):
        r = jnp.where(i < n_ctx_blocks, MOD_CTX_ROW, b)
        return (r * 6 + which, 0, 0)
    return pl.BlockSpec((None, 1, D), idx)


def _norm_mod_kernel(x_ref, g_ref, sh_ref, sc_ref, o_ref):
    h = _rms(x_ref[...], g_ref[...]) * (1.0 + sc_ref[...]) + sh_ref[...]
    o_ref[...] = h.astype(o_ref.dtype)


META_LANES = 128
META_E = 0
META_RANK = TOP_K
META_GATE = 2 * TOP_K
HALF_D = D // 2
HI_MASK = 0xFFFF0000


def _pack_bf16_pair(lo, hi):
    ulo = pltpu.bitcast(lo.astype(jnp.bfloat16).astype(jnp.float32), jnp.uint32)
    uhi = pltpu.bitcast(hi.astype(jnp.bfloat16).astype(jnp.float32), jnp.uint32)
    return lax.shift_right_logical(ulo, jnp.uint32(16)) | (uhi & jnp.uint32(HI_MASK))


def _unpack_bf16_pair(w):
    lo = pltpu.bitcast(lax.shift_left(w, jnp.uint32(16)), jnp.float32).astype(jnp.bfloat16)
    hi = pltpu.bitcast(w & jnp.uint32(HI_MASK), jnp.float32).astype(jnp.bfloat16)
    return lo, hi


def _norm_mod_router_kernel(x_ref, g_ref, sh_ref, sc_ref, wr_ref, br_ref, o_ref, meta_ref, cnt_ref, run_ref):
    first = jnp.logical_and(pl.program_id(0) == 0, pl.program_id(1) == 0)

    @pl.when(first)
    def _():
        run_ref[...] = jnp.zeros_like(run_ref)

    h = _rms(x_ref[...], g_ref[...]) * (1.0 + sc_ref[...]) + sh_ref[...]
    o_ref[...] = _pack_bf16_pair(h[:, :HALF_D], h[:, HALF_D:])
    logits = jnp.dot(h, wr_ref[...], preferred_element_type=jnp.float32,
                     precision=lax.Precision.HIGHEST) + br_ref[...]
    lane = lax.broadcasted_iota(jnp.int32, (ROW_TILE, N_EXPERTS), 1)
    vals, hots = [], []
    l = logits
    for _ in range(TOP_K):
        m = jnp.max(l, axis=-1, keepdims=True)
        idx = jnp.min(jnp.where(l == m, lane, N_EXPERTS), axis=-1, keepdims=True)
        hot = lane == idx
        vals.append(m)
        hots.append(hot)
        l = jnp.where(hot, -jnp.inf, l)
    es = [jnp.exp(v - vals[0]) for v in vals]
    den = es[0] + es[1] + es[2] + es[3]
    onehot = sum(hot.astype(jnp.float32) for hot in hots)
    r_i = lax.broadcasted_iota(jnp.int32, (ROW_TILE, ROW_TILE), 0)
    c_i = lax.broadcasted_iota(jnp.int32, (ROW_TILE, ROW_TILE), 1)
    lower = jnp.where(r_i > c_i, 1.0, 0.0).astype(jnp.bfloat16)
    before = jnp.dot(lower, onehot.astype(jnp.bfloat16), preferred_element_type=jnp.float32) + run_ref[0:1, 0:N_EXPERTS]
    mlane = lax.broadcasted_iota(jnp.int32, (ROW_TILE, META_LANES), 1)
    lane_f = lane.astype(jnp.float32)
    meta = jnp.zeros((ROW_TILE, META_LANES), jnp.float32)
    for k in range(TOP_K):
        hot_f = hots[k].astype(jnp.float32)
        e_k = jnp.sum(hot_f * lane_f, axis=-1, keepdims=True)
        rank_k = jnp.sum(hot_f * before, axis=-1, keepdims=True)
        meta = jnp.where(mlane == META_E + k, e_k, meta)
        meta = jnp.where(mlane == META_RANK + k, rank_k, meta)
        meta = jnp.where(mlane == META_GATE + k, es[k] / den, meta)
    meta_ref[...] = meta
    total = run_ref[0:1, 0:N_EXPERTS] + jnp.sum(onehot, axis=0, keepdims=True)
    run_ref[0:1, 0:N_EXPERTS] = total
    cnt_ref[...] = jnp.broadcast_to(run_ref[0:1, :], cnt_ref.shape)


def _norm_mod(x, g, mod, shift_idx, scale_idx, n_ctx_blocks, router=None):
    t = x.shape[1]
    grid = (B, t // ROW_TILE)
    row = pl.BlockSpec((None, ROW_TILE, D), lambda b, i: (b, i, 0))
    in_specs = [row, pl.BlockSpec((1, D), lambda b, i: (0, 0)),
                _mod_spec(shift_idx, n_ctx_blocks), _mod_spec(scale_idx, n_ctx_blocks)]
    args = [x, g.reshape(1, D), mod, mod]
    if router is None:
        return pl.pallas_call(
            _norm_mod_kernel, grid=grid, in_specs=in_specs, out_specs=row,
            out_shape=jax.ShapeDtypeStruct((B, t, D), jnp.bfloat16),
            compiler_params=_cparams("parallel", "parallel"), name="norm_mod",
        )(*args)
    w_router, b_router = router
    in_specs += [pl.BlockSpec((D, N_EXPERTS), lambda b, i: (0, 0)),
                 pl.BlockSpec((1, N_EXPERTS), lambda b, i: (0, 0))]
    args += [w_router, b_router.reshape(1, N_EXPERTS)]
    return pl.pallas_call(
        _norm_mod_router_kernel, grid=grid, in_specs=in_specs,
        out_specs=[pl.BlockSpec((None, ROW_TILE, HALF_D), lambda b, i: (b, i, 0)),
                   pl.BlockSpec((None, ROW_TILE, META_LANES), lambda b, i: (b, i, 0)),
                   pl.BlockSpec((8, META_LANES), lambda b, i: (0, 0))],
        out_shape=[jax.ShapeDtypeStruct((B, t, HALF_D), jnp.uint32),
                   jax.ShapeDtypeStruct((B, t, META_LANES), jnp.float32),
                   jax.ShapeDtypeStruct((8, META_LANES), jnp.float32)],
        scratch_shapes=[pltpu.VMEM((8, META_LANES), jnp.float32)],
        compiler_params=_cparams("arbitrary", "arbitrary"), name="norm_mod_router",
    )(*args)


def _post_norm_kernel(x_ref, y_ref, g_ref, gt_ref, o_ref):
    o_ref[...] = x_ref[...] + gt_ref[...] * _rms(y_ref[...], g_ref[...])


def _post_norm_residual(x, y, g, mod, gate_idx, n_ctx_blocks):
    t = x.shape[1]
    row = pl.BlockSpec((None, ROW_TILE, D), lambda b, i: (b, i, 0))
    return pl.pallas_call(
        _post_norm_kernel, grid=(B, t // ROW_TILE),
        in_specs=[row, row, pl.BlockSpec((1, D), lambda b, i: (0, 0)),
                  _mod_spec(gate_idx, n_ctx_blocks)],
        out_specs=row,
        out_shape=jax.ShapeDtypeStruct((B, t, D), jnp.float32),
        input_output_aliases={0: 0},
        compiler_params=_cparams("parallel", "parallel"), name="post_norm_residual",
    )(x, y, g.reshape(1, D), mod)


def _matmul_kernel(a_ref, w_ref, o_ref):
    o_ref[...] = jnp.dot(a_ref[...], w_ref[...].astype(jnp.bfloat16),
                         preferred_element_type=jnp.float32).astype(o_ref.dtype)


def _matmul(a, w, tm, tn, out_dtype=jnp.float32, name="matmul"):
    m, k = a.shape
    n = w.shape[1]
    return pl.pallas_call(
        _matmul_kernel, grid=(m // tm, n // tn),
        in_specs=[pl.BlockSpec((tm, k), lambda i, j: (i, 0)),
                  pl.BlockSpec((k, tn), lambda i, j: (0, j))],
        out_specs=pl.BlockSpec((tm, tn), lambda i, j: (i, j)),
        out_shape=jax.ShapeDtypeStruct((m, n), out_dtype),
        compiler_params=_cparams("parallel", "arbitrary"), name=name,
    )(a, w)


def _glu_matmul_kernel(a_ref, wv_ref, wg_ref, o_ref):
    a = a_ref[...]
    val = jnp.dot(a, wv_ref[...].astype(jnp.bfloat16), preferred_element_type=jnp.float32)
    gate = jnp.dot(a, wg_ref[...].astype(jnp.bfloat16), preferred_element_type=jnp.float32)
    o_ref[...] = val * _sigmoid(gate)


def _glu_matmul(a, w_glu, tm, tn):
    m, k = a.shape
    n = w_glu.shape[1] // 2
    nj = n // tn
    return pl.pallas_call(
        _glu_matmul_kernel, grid=(m // tm, nj),
        in_specs=[pl.BlockSpec((tm, k), lambda i, j: (i, 0)),
                  pl.BlockSpec((k, tn), lambda i, j: (0, j)),
                  pl.BlockSpec((k, tn), lambda i, j: (0, nj + j))],
        out_specs=pl.BlockSpec((tm, tn), lambda i, j: (i, j)),
        out_shape=jax.ShapeDtypeStruct((m, n), jnp.float32),
        compiler_params=_cparams("parallel", "arbitrary"), name="glu_matmul",
    )(a, w_glu, w_glu)


IN_TN = 256
IN_ROPE_END = 8
IN_DV_END = 12
IN_GQ_END = 16
IN_GK_TILE = 16


def _inproj_kernel(a_ref, w_ref, cos_ref, sin_ref, gq_ref, gk_ref, o_ref):
    j = pl.program_id(1)
    acc = jnp.dot(a_ref[...], w_ref[...].astype(jnp.bfloat16), preferred_element_type=jnp.float32)

    def rope(x):
        return x * cos_ref[...] + pltpu.roll(x, HD // 2, 1) * sin_ref[...]

    def store(fn):
        for c in range(IN_TN // HD):
            o_ref[:, c * HD:(c + 1) * HD] = fn(acc[:, c * HD:(c + 1) * HD]).astype(o_ref.dtype)

    @pl.when(j < IN_ROPE_END)
    def _():
        store(rope)

    @pl.when(jnp.logical_or(jnp.logical_and(j >= IN_ROPE_END, j < IN_DV_END), j > IN_GK_TILE))
    def _():
        store(lambda x: x)

    @pl.when(jnp.logical_and(j >= IN_DV_END, j < IN_GQ_END))
    def _():
        store(lambda x: rope(_rms(x, gq_ref[...])))

    @pl.when(j == IN_GK_TILE)
    def _():
        store(lambda x: rope(_rms(x, gk_ref[...])))


def _inproj(h, w_in, cos2, sin2, g_q, g_k):
    m = h.shape[0]
    const = lambda i, j: (0, 0)
    return pl.pallas_call(
        _inproj_kernel, grid=(m // TOK, ATTN_IN // IN_TN),
        in_specs=[pl.BlockSpec((TOK, D), lambda i, j: (i, 0)),
                  pl.BlockSpec((D, IN_TN), lambda i, j: (0, j)),
                  pl.BlockSpec((TOK, HD), const), pl.BlockSpec((TOK, HD), const),
                  pl.BlockSpec((1, HD), const), pl.BlockSpec((1, HD), const)],
        out_specs=pl.BlockSpec((TOK, IN_TN), lambda i, j: (i, j)),
        out_shape=jax.ShapeDtypeStruct((m, ATTN_IN), jnp.bfloat16),
        compiler_params=_cparams("parallel", "arbitrary"), name="attn_inproj",
    )(h, w_in, cos2, sin2, g_q.reshape(1, HD), g_k.reshape(1, HD))


ATT_TQ = 256
ATT_SCALE = HD ** -0.5


def _softmax_pv(q, k, v):
    s = lax.dot_general(q, k, (((1,), (1,)), ((), ())), preferred_element_type=jnp.float32)
    m = jnp.max(s, axis=-1, keepdims=True)
    e = jnp.exp((s - m) * ATT_SCALE)
    l = jnp.sum(e, axis=-1, keepdims=True)
    return jnp.dot(e.astype(jnp.bfloat16), v, preferred_element_type=jnp.float32), l


def _diff_attn_kernel(lam_ref, q_ref, k_ref, v_ref, g_ref, o_ref, *, lambda_init):
    lp = lam_ref[...]
    lam = (jnp.exp(jnp.sum(lp[0:1] * lp[1:2], axis=-1, keepdims=True))
           - jnp.exp(jnp.sum(lp[2:3] * lp[3:4], axis=-1, keepdims=True)) + lambda_init)

    def run(nk):
        q = q_ref[...]
        k = k_ref[0:nk, :]
        v = v_ref[0:nk, :]
        pv1, l1 = _softmax_pv(q[:, :HD], k[:, :HD], v)
        pv2, l2 = _softmax_pv(q[:, HD:], k[:, HD:], v)
        o = pv1 / l1 - lam * (pv2 / l2)
        o_ref[...] = (_rms(o, g_ref[...]) * (1.0 - lambda_init)).astype(o_ref.dtype)

    @pl.when(pl.program_id(2) == 0)
    def _():
        run(CTX)

    @pl.when(pl.program_id(2) > 0)
    def _():
        run(TOK)


def _gqa_attn_kernel(q_ref, k_ref, v_ref, o_ref):
    def run(nk):
        k = k_ref[0:nk, :]
        v = v_ref[0:nk, :]
        for g in range(GQA_GROUP):
            pv, l = _softmax_pv(q_ref[:, g * HD:(g + 1) * HD], k, v)
            o_ref[:, g * HD:(g + 1) * HD] = (pv / l).astype(o_ref.dtype)

    @pl.when(pl.program_id(2) == 0)
    def _():
        run(CTX)

    @pl.when(pl.program_id(2) > 0)
    def _():
        run(TOK)


def _attention(p, lam_params, g_subln, lambda_init):
    nq = TOK // ATT_TQ
    dv = 2 * HD
    od = pl.pallas_call(
        functools.partial(_diff_attn_kernel, lambda_init=lambda_init),
        grid=(B, DIFF_HEADS, nq),
        in_specs=[pl.BlockSpec((4, HD), lambda b, h, i: (0, 0)),
                  pl.BlockSpec((None, ATT_TQ, dv), lambda b, h, i: (b, i, h)),
                  pl.BlockSpec((None, TOK, dv), lambda b, h, i: (b, 0, DIFF_HEADS + h)),
                  pl.BlockSpec((None, TOK, dv), lambda b, h, i: (b, 0, 2 * DIFF_HEADS + h)),
                  pl.BlockSpec((1, dv), lambda b, h, i: (0, 0))],
        out_specs=pl.BlockSpec((None, ATT_TQ, dv), lambda b, h, i: (b, i, h)),
        out_shape=jax.ShapeDtypeStruct((B, TOK, DIFF_HEADS * dv), jnp.bfloat16),
        compiler_params=_cparams("parallel", "parallel", "arbitrary"), name="diff_attention",
    )(lam_params, p, p, p, g_subln.reshape(1, dv))
    gq_w = GQA_GROUP * HD
    gq0 = 3072 // gq_w
    gk0 = 4096 // HD
    gv0 = 4352 // HD
    og = pl.pallas_call(
        _gqa_attn_kernel,
        grid=(B, GQA_KV_HEADS, nq),
        in_specs=[pl.BlockSpec((None, ATT_TQ, gq_w), lambda b, n, i: (b, i, gq0 + n)),
                  pl.BlockSpec((None, TOK, HD), lambda b, n, i: (b, 0, gk0 + n)),
                  pl.BlockSpec((None, TOK, HD), lambda b, n, i: (b, 0, gv0 + n))],
        out_specs=pl.BlockSpec((None, ATT_TQ, gq_w), lambda b, n, i: (b, i, n)),
        out_shape=jax.ShapeDtypeStruct((B, TOK, GQA_Q_HEADS * HD), jnp.bfloat16),
        compiler_params=_cparams("parallel", "parallel", "arbitrary"), name="gqa_attention",
    )(p, p, p)
    return jnp.concatenate([od, og], axis=-1)


def _rope_tables():
    rows = SEQ // GRID_W
    row_id, col_id = jnp.meshgrid(jnp.arange(rows), jnp.arange(GRID_W), indexing="ij")
    inv_freq = ROPE_THETA ** (-jnp.arange(ROPE_FREQS, dtype=jnp.float32) / ROPE_FREQS)
    ang = jnp.concatenate([row_id.reshape(-1, 1) * inv_freq, col_id.reshape(-1, 1) * inv_freq], axis=-1)
    cos, sin = jnp.cos(ang), jnp.sin(ang)
    cos2 = jnp.concatenate([cos, cos], axis=-1)
    sin2 = jnp.concatenate([-sin, sin], axis=-1)
    cos2 = jnp.concatenate([jnp.ones((CTX, HD), jnp.float32), cos2], axis=0)
    sin2 = jnp.concatenate([jnp.zeros((CTX, HD), jnp.float32), sin2], axis=0)
    return cos2, sin2


MOE_TM = 512
MOE_SUB = 256
MOE_TF = 1024
DISPATCH_TT = 256
COMBINE_TT = 64


def _moe_up_kernel(be_ref, first_ref, rows_ref, nxt_ref, x_ref, w_hbm, bg_ref, bl_ref, o_ref,
                   wst, wbf, sem):
    f = pl.program_id(0)
    b = pl.program_id(1)
    nf = pl.num_programs(0)

    def copies(e, ff):
        col = pl.multiple_of(ff * MOE_TF, MOE_TF)
        return [pltpu.make_async_copy(w_hbm.at[e, :, pl.ds(part * D_FF + col, MOE_TF)],
                                      wst.at[part], sem.at[part]) for part in range(2)]

    @pl.when(first_ref[b] == 1)
    def _():
        @pl.when(jnp.logical_and(f == 0, b == 0))
        def _():
            for c in copies(be_ref[0], 0):
                c.start()

        for c in copies(be_ref[b], f):
            c.wait()
        wbf[...] = wst[...].astype(jnp.bfloat16)
        e_next = nxt_ref[b]

        @pl.when(e_next >= 0)
        def _():
            for c in copies(e_next, f):
                c.start()

        @pl.when(jnp.logical_and(e_next < 0, f + 1 < nf))
        def _():
            for c in copies(be_ref[0], f + 1):
                c.start()

    for s in range(MOE_TM // MOE_SUB):
        sl = slice(s * MOE_SUB, (s + 1) * MOE_SUB)

        @pl.when(rows_ref[b] > s * MOE_SUB)
        def _():
            lo, hi = _unpack_bf16_pair(x_ref[sl, :])
            glu = (jnp.dot(lo, wbf[0, :HALF_D, :], preferred_element_type=jnp.float32)
                   + jnp.dot(hi, wbf[0, HALF_D:, :], preferred_element_type=jnp.float32) + bg_ref[...])
            lin = (jnp.dot(lo, wbf[1, :HALF_D, :], preferred_element_type=jnp.float32)
                   + jnp.dot(hi, wbf[1, HALF_D:, :], preferred_element_type=jnp.float32) + bl_ref[...])
            glu = jnp.minimum(glu, SWIGLU_LIMIT)
            lin = jnp.clip(lin, -SWIGLU_LIMIT, SWIGLU_LIMIT)
            o_ref[sl, :] = (glu * _sigmoid(SWIGLU_ALPHA * glu) * (lin + 1.0)).astype(o_ref.dtype)

        @pl.when(rows_ref[b] <= s * MOE_SUB)
        def _():
            o_ref[sl, :] = jnp.zeros((MOE_SUB, MOE_TF), o_ref.dtype)


def _moe_down_kernel(be_ref, first_ref, rows_ref, nxt_ref, a_ref, w_hbm, bias_ref, o_ref, wst, wbf, sem):
    b = pl.program_id(0)

    def copy(e):
        return pltpu.make_async_copy(w_hbm.at[e], wst, sem)

    @pl.when(first_ref[b] == 1)
    def _():
        @pl.when(b == 0)
        def _():
            copy(be_ref[0]).start()

        copy(be_ref[b]).wait()
        wbf[...] = wst[...].astype(jnp.bfloat16)
        e_next = nxt_ref[b]

        @pl.when(e_next >= 0)
        def _():
            copy(e_next).start()

    for s in range(MOE_TM // MOE_SUB):
        sl = slice(s * MOE_SUB, (s + 1) * MOE_SUB)

        @pl.when(rows_ref[b] > s * MOE_SUB)
        def _():
            o_ref[sl, :] = jnp.dot(a_ref[sl, :], wbf[...],
                                   preferred_element_type=jnp.float32) + bias_ref[...]

        @pl.when(rows_ref[b] <= s * MOE_SUB)
        def _():
            o_ref[sl, :] = jnp.zeros((MOE_SUB, D), o_ref.dtype)


def _moe_experts(x_sorted, tables, w_gate_up, b_gate_up, w_down, b_down):
    r = x_sorted.shape[0]
    nb = r // MOE_TM
    nf = D_FF // MOE_TF
    bgu = b_gate_up.reshape(N_EXPERTS, 1, 2 * D_FF)
    act = pl.pallas_call(
        _moe_up_kernel,
        grid_spec=pltpu.PrefetchScalarGridSpec(
            num_scalar_prefetch=4, grid=(nf, nb),
            in_specs=[pl.BlockSpec((MOE_TM, HALF_D), lambda f, b, be, fi, ro, nx: (b, 0)),
                      pl.BlockSpec(memory_space=pl.ANY),
                      pl.BlockSpec((None, 1, MOE_TF), lambda f, b, be, fi, ro, nx: (be[b], 0, f)),
                      pl.BlockSpec((None, 1, MOE_TF), lambda f, b, be, fi, ro, nx: (be[b], 0, nf + f))],
            out_specs=pl.BlockSpec((MOE_TM, MOE_TF), lambda f, b, be, fi, ro, nx: (b, f)),
            scratch_shapes=[pltpu.VMEM((2, D, MOE_TF), jnp.float32), pltpu.VMEM((2, D, MOE_TF), jnp.bfloat16),
                            pltpu.SemaphoreType.DMA((2,))]),
        out_shape=jax.ShapeDtypeStruct((r, D_FF), jnp.bfloat16),
        compiler_params=_cparams("arbitrary", "arbitrary"), name="moe_gate_up",
    )(*tables, x_sorted, w_gate_up, bgu, bgu)
    return pl.pallas_call(
        _moe_down_kernel,
        grid_spec=pltpu.PrefetchScalarGridSpec(
            num_scalar_prefetch=4, grid=(nb,),
            in_specs=[pl.BlockSpec((MOE_TM, D_FF), lambda b, be, fi, ro, nx: (b, 0)),
                      pl.BlockSpec(memory_space=pl.ANY),
                      pl.BlockSpec((None, 1, D), lambda b, be, fi, ro, nx: (be[b], 0, 0))],
            out_specs=pl.BlockSpec((MOE_TM, D), lambda b, be, fi, ro, nx: (b, 0)),
            scratch_shapes=[pltpu.VMEM((D_FF, D), jnp.float32), pltpu.VMEM((D_FF, D), jnp.bfloat16),
                            pltpu.SemaphoreType.DMA]),
        out_shape=jax.ShapeDtypeStruct((r, D), jnp.float32),
        compiler_params=_cparams("arbitrary"), name="moe_down",
    )(*tables, act, w_down, b_down.reshape(N_EXPERTS, 1, D))


def _dispatch_kernel(dest_ref, hp_hbm, xs_in_hbm, xs_hbm, idx_smem, sem_idx, sem_rows):
    del xs_in_hbm
    i = pl.program_id(0)
    n = pl.num_programs(0)
    idx_copy = pltpu.make_async_copy(dest_ref, idx_smem, sem_idx)
    idx_copy.start()
    idx_copy.wait()
    base = i * DISPATCH_TT

    def issue(t, carry):
        for k in range(TOP_K):
            d = idx_smem[0, 0, t * TOP_K + k]
            pltpu.make_async_copy(hp_hbm.at[base + t], xs_hbm.at[d], sem_rows).start()
        return carry

    lax.fori_loop(0, DISPATCH_TT, issue, 0)

    def drain():
        def wait_one(j, carry):
            pltpu.make_async_copy(hp_hbm.at[0], xs_hbm.at[0], sem_rows).wait()
            return carry
        lax.fori_loop(0, DISPATCH_TT * TOP_K, wait_one, 0)

    @pl.when(i > 0)
    def _():
        drain()

    @pl.when(i == n - 1)
    def _():
        drain()


def _dispatch(dest, hp, n_rows):
    n_tok = hp.shape[0]
    n = n_tok // DISPATCH_TT
    width = DISPATCH_TT * TOP_K
    zeros = jnp.zeros((n_rows, HALF_D), jnp.uint32)
    return pl.pallas_call(
        _dispatch_kernel, grid=(n,),
        in_specs=[pl.BlockSpec((1, 1, width), lambda i: (i, 0, 0)),
                  pl.BlockSpec(memory_space=pl.ANY), pl.BlockSpec(memory_space=pl.ANY)],
        out_specs=pl.BlockSpec(memory_space=pl.ANY),
        out_shape=jax.ShapeDtypeStruct((n_rows, HALF_D), jnp.uint32),
        scratch_shapes=[pltpu.SMEM((1, 1, width), jnp.int32), pltpu.SemaphoreType.DMA, pltpu.SemaphoreType.DMA],
        input_output_aliases={2: 0},
        compiler_params=_cparams("arbitrary"), name="moe_dispatch",
    )(dest.reshape(n, 1, width), hp, zeros)


def _combine_kernel(dcur_ref, dnext_ref, out_hbm, meta_ref, x_ref, g_ref, gt_ref, o_ref,
                    idx_smem, buf, sem_idx, sem_rows):
    i = pl.program_id(0)
    n = pl.num_programs(0)

    def row_copy(d, slot, k, t):
        return pltpu.make_async_copy(out_hbm.at[d], buf.at[slot, k, t], sem_rows.at[slot])

    def gather(d_ref, slot):
        idx_copy = pltpu.make_async_copy(d_ref, idx_smem, sem_idx)
        idx_copy.start()
        idx_copy.wait()

        def issue(t, carry):
            for k in range(TOP_K):
                row_copy(idx_smem[0, 0, t * TOP_K + k], slot, k, t).start()
            return carry
        lax.fori_loop(0, COMBINE_TT, issue, 0)

    @pl.when(i == 0)
    def _():
        gather(dcur_ref, 0)

    @pl.when(i + 1 < n)
    def _():
        gather(dnext_ref, (i + 1) % 2)

    slot = i % 2

    def wait_one(j, carry):
        row_copy(0, slot, 0, 0).wait()
        return carry
    lax.fori_loop(0, COMBINE_TT * TOP_K, wait_one, 0)

    meta = meta_ref[...]
    f = meta[:, META_GATE:META_GATE + 1] * buf[slot, 0]
    for k in range(1, TOP_K):
        f = f + meta[:, META_GATE + k:META_GATE + k + 1] * buf[slot, k]
    o_ref[...] = x_ref[...] + gt_ref[...] * _rms(f, g_ref[...])


def _combine(dest, out_sorted, meta, x, g, mod, gate_idx, rows_per_batch, ctx_rows):
    n_tok = x.shape[0]
    n = n_tok // COMBINE_TT
    width = COMBINE_TT * TOP_K
    per_batch = rows_per_batch // COMBINE_TT
    ctx_blocks = ctx_rows // COMBINE_TT

    def gate_row(i):
        r = jnp.where(i % per_batch < ctx_blocks, MOD_CTX_ROW, i // per_batch)
        return (r * 6 + gate_idx, 0, 0)

    row = pl.BlockSpec((COMBINE_TT, D), lambda i: (i, 0))
    return pl.pallas_call(
        _combine_kernel, grid=(n,),
        in_specs=[pl.BlockSpec((1, 1, width), lambda i: (i, 0, 0)),
                  pl.BlockSpec((1, 1, width), lambda i: (jnp.minimum(i + 1, n - 1), 0, 0)),
                  pl.BlockSpec(memory_space=pl.ANY),
                  pl.BlockSpec((COMBINE_TT, META_LANES), lambda i: (i, 0)),
                  row, pl.BlockSpec((1, D), lambda i: (0, 0)),
                  pl.BlockSpec((None, 1, D), gate_row)],
        out_specs=row,
        out_shape=jax.ShapeDtypeStruct((n_tok, D), jnp.float32),
        scratch_shapes=[pltpu.SMEM((1, 1, width), jnp.int32),
                        pltpu.VMEM((2, TOP_K, COMBINE_TT, D), jnp.float32),
                        pltpu.SemaphoreType.DMA, pltpu.SemaphoreType.DMA((2,))],
        input_output_aliases={4: 0},
        compiler_params=_cparams("arbitrary"), name="moe_combine",
    )(dest.reshape(n, 1, width), dest.reshape(n, 1, width), out_sorted, meta, x, g.reshape(1, D), mod)


def _moe_block_tables(counts, n_assign):
    padded = (counts + MOE_TM - 1) // MOE_TM * MOE_TM
    pad_end = jnp.cumsum(padded)
    pad_start = pad_end - padded
    nb = -(-(n_assign + N_EXPERTS * (MOE_TM - 1)) // MOE_TM)
    block_start = jnp.arange(nb, dtype=jnp.int32) * MOE_TM
    block_e_raw = jnp.minimum(jnp.sum(block_start[:, None] >= pad_end[None, :], axis=1), N_EXPERTS - 1)
    block_e_raw = block_e_raw.astype(jnp.int32)
    onehot_e = block_e_raw[:, None] == jnp.arange(N_EXPERTS)[None, :]
    valid_end = jnp.sum(jnp.where(onehot_e, (pad_start + counts)[None, :], 0), axis=1)
    block_rows = jnp.clip(valid_end - block_start, 0, MOE_TM).astype(jnp.int32)
    block_rows = jnp.where(block_start < pad_end[-1], block_rows, 0)
    last_e = jnp.max(jnp.where(block_rows > 0, block_e_raw, 0))
    block_e = jnp.where(block_rows > 0, block_e_raw, last_e)
    block_first = jnp.concatenate([jnp.ones((1,), jnp.int32),
                                   (block_e[1:] != block_e[:-1]).astype(jnp.int32)])
    idx = jnp.arange(nb, dtype=jnp.int32)
    first_pos = jnp.where(block_first == 1, idx, nb)
    later = jnp.where(idx[None, :] > idx[:, None], first_pos[None, :], nb)
    next_pos = jnp.min(later, axis=1)
    next_e = jnp.sum(jnp.where(idx[None, :] == next_pos[:, None], block_e[None, :], 0), axis=1)
    block_next = jnp.where(next_pos < nb, next_e, -1).astype(jnp.int32)
    return pad_start, nb * MOE_TM, (block_e, block_first, block_rows, block_next)


def _moe_ffn(x_res, hp, meta, cnt, g_post, mod, gate_idx, rows_per_batch, ctx_rows,
             w_gate_up, b_gate_up, w_down, b_down):
    n_tok = hp.shape[0]
    counts = cnt[0, :N_EXPERTS].astype(jnp.int32)
    pad_start, n_rows, tables = _moe_block_tables(counts, n_tok * TOP_K)
    top_e = meta[:, META_E:META_E + TOP_K].astype(jnp.int32)
    rank = meta[:, META_RANK:META_RANK + TOP_K].astype(jnp.int32)
    hot = top_e[:, :, None] == jnp.arange(N_EXPERTS)[None, None, :]
    dest = jnp.sum(jnp.where(hot, pad_start[None, None, :], 0), axis=-1) + rank
    x_sorted = _dispatch(dest, hp, n_rows)
    out_sorted = _moe_experts(x_sorted, tables, w_gate_up, b_gate_up, w_down, b_down)
    return _combine(dest, out_sorted, meta, x_res, g_post, mod, gate_idx, rows_per_batch, ctx_rows)


S5_GB = 8
S5_LANES = SSM_CHUNK * SSM_GROUP
S5_NK_CTX = CTX // SSM_CHUNK
S5_NK_LAT = SEQ // SSM_CHUNK
S5_PAIR = 2 * B


def _s5_direction(reverse, uc_ref, ul_ref, wb_ref, m_ref, wc_ref, coef_ref, y_ref,
                  s1c, s2c, s1l, s2l, xin, accumulate):
    half = 2 * SSM_STATE
    for g in range(S5_GB):
        sc = jnp.dot(uc_ref[g], wb_ref[g], preferred_element_type=jnp.float32)
        s1c[g] = sc[:, :half]
        s2c[g] = sc[:, half:]
        sl = jnp.dot(ul_ref[g], wb_ref[g], preferred_element_type=jnp.float32)
        s1l[g] = sl[:, :half]
        s2l[g] = sl[:, half:]

    lower = lax.broadcasted_iota(jnp.int32, (S5_PAIR, half), 0) < B
    p1 = [jnp.broadcast_to(coef_ref[g, 0:1, :], (S5_PAIR, half)) for g in range(S5_GB)]
    p2 = [jnp.broadcast_to(coef_ref[g, 1:2, :], (S5_PAIR, half)) for g in range(S5_GB)]

    def tile_step(g, t1, t2, v1, v2):
        y1a = p1[g] * v1 + p2[g] * v2 + t1
        y1b = p1[g] * v2 - p2[g] * v1 + t2
        r1a = pltpu.roll(y1a, B, 0)
        r1b = pltpu.roll(y1b, B, 0)
        y2a = p1[g] * r1a + p2[g] * r1b + t1
        y2b = p1[g] * r1b - p2[g] * r1a + t2
        r2a = pltpu.roll(y2a, B, 0)
        r2b = pltpu.roll(y2b, B, 0)
        if not reverse:
            x_in = jnp.where(lower, v1, r1a)
            return x_in, jnp.where(lower, r2a, y2a), jnp.where(lower, r2b, y2b)
        x_in = jnp.where(lower, r1a, v1)
        return x_in, jnp.where(lower, y2a, r2a), jnp.where(lower, y2b, r2b)

    def scan(s1, s2, n_tiles, state, record):
        def body(j, carry):
            jj = (n_tiles - 1 - j) if reverse else j
            r0 = pl.multiple_of(jj * S5_PAIR, S5_PAIR)
            new = []
            for g in range(S5_GB):
                v1, v2 = carry[2 * g], carry[2 * g + 1]
                x_in, v1, v2 = tile_step(g, s1[g, pl.ds(r0, S5_PAIR), :], s2[g, pl.ds(r0, S5_PAIR), :], v1, v2)
                if record:
                    xin[g, pl.ds(r0, S5_PAIR), :] = x_in
                new += [v1, v2]
            return tuple(new)
        return lax.fori_loop(0, n_tiles, body, state)

    zero = jnp.zeros((S5_PAIR, half), jnp.float32)
    state = tuple(zero for _ in range(2 * S5_GB))
    state = scan(s1c, s2c, S5_NK_CTX * B // S5_PAIR, state, False)
    scan(s1l, s2l, S5_NK_LAT * B // S5_PAIR, state, True)

    for g in range(S5_GB):
        y = (jnp.dot(ul_ref[g], m_ref[g], preferred_element_type=jnp.float32)
             + jnp.dot(xin[g].astype(jnp.bfloat16), wc_ref[g], preferred_element_type=jnp.float32))
        if accumulate:
            y_ref[g] = y_ref[g] + y
        else:
            y_ref[g] = y


def _s5_kernel(uc_ref, ul_ref, wb_ref, m_ref, wc_ref, coef_ref, y_ref, s1c, s2c, s1l, s2l, xin):
    args = (uc_ref, ul_ref, wb_ref, m_ref, wc_ref, coef_ref, y_ref, s1c, s2c, s1l, s2l, xin)

    @pl.when(pl.program_id(1) == 0)
    def _():
        _s5_direction(False, *args, accumulate=False)

    @pl.when(pl.program_id(1) == 1)
    def _():
        _s5_direction(True, *args, accumulate=True)


def _s5_scan(u_ctx, u_lat, wb, m, wc, coef):
    rc, rl = u_ctx.shape[1], u_lat.shape[1]
    half = 2 * SSM_STATE
    wspec = lambda k, n: pl.BlockSpec((None, S5_GB, k, n), lambda gi, d: (d, gi, 0, 0))
    return pl.pallas_call(
        _s5_kernel, grid=(SSM_GROUPS // S5_GB, 2),
        in_specs=[pl.BlockSpec((S5_GB, rc, S5_LANES), lambda gi, d: (gi, 0, 0)),
                  pl.BlockSpec((S5_GB, rl, S5_LANES), lambda gi, d: (gi, 0, 0)),
                  wspec(S5_LANES, 2 * half), wspec(S5_LANES, S5_LANES), wspec(half, S5_LANES),
                  wspec(2, half)],
        out_specs=pl.BlockSpec((S5_GB, rl, S5_LANES), lambda gi, d: (gi, 0, 0)),
        out_shape=jax.ShapeDtypeStruct((SSM_GROUPS, rl, S5_LANES), jnp.float32),
        scratch_shapes=[pltpu.VMEM((S5_GB, rc, half), jnp.float32), pltpu.VMEM((S5_GB, rc, half), jnp.float32),
                        pltpu.VMEM((S5_GB, rl, half), jnp.float32), pltpu.VMEM((S5_GB, rl, half), jnp.float32),
                        pltpu.VMEM((S5_GB, rl, half), jnp.float32)],
        compiler_params=_cparams("parallel", "arbitrary"), name="s5_scan",
    )(u_ctx, u_lat, wb, m, wc, coef)


def _s5_matrices(a_re, a_im, b_re, b_im, c_re, c_im, log_dt, reverse):
    hp = lax.Precision.HIGHEST
    n = SSM_CHUNK
    dt = jnp.exp(log_dt)[:, None]
    mag = jnp.exp(a_re * dt)
    ab_re, ab_im = mag * jnp.cos(a_im * dt), mag * jnp.sin(a_im * dt)
    den = a_re * a_re + a_im * a_im
    f_re = ((ab_re - 1.0) * a_re + ab_im * a_im) / den
    f_im = (ab_im * a_re - (ab_re - 1.0) * a_im) / den
    bb_re = f_re[..., None] * b_re - f_im[..., None] * b_im
    bb_im = f_re[..., None] * b_im + f_im[..., None] * b_re
    tau = jnp.arange(n + 1, dtype=jnp.float32)[:, None, None]
    pmag = jnp.exp(tau * (a_re * dt))
    pw_re, pw_im = pmag * jnp.cos(tau * (a_im * dt)), pmag * jnp.sin(tau * (a_im * dt))
    ca_re = c_re[None] * pw_re[:n, :, None, :] - c_im[None] * pw_im[:n, :, None, :]
    ca_im = c_re[None] * pw_im[:n, :, None, :] + c_im[None] * pw_re[:n, :, None, :]
    kern = (jnp.einsum("tgcp,gpd->tgcd", ca_re, bb_re, precision=hp)
            - jnp.einsum("tgcp,gpd->tgcd", ca_im, bb_im, precision=hp))
    s_idx = jnp.arange(n)[:, None]
    t_idx = jnp.arange(n)[None, :]
    lag = (s_idx - t_idx) if reverse else (t_idx - s_idx)
    mk = kern[jnp.clip(lag, 0, n - 1)] * (lag >= 0)[:, :, None, None, None]
    m = mk.transpose(2, 0, 4, 1, 3).reshape(SSM_GROUPS, S5_LANES, S5_LANES)
    e_idx = jnp.arange(n) if reverse else (n - 1 - jnp.arange(n))
    ae_re, ae_im = pw_re[e_idx], pw_im[e_idx]
    wb_re = ae_re[:, :, :, None] * bb_re[None] - ae_im[:, :, :, None] * bb_im[None]
    wb_im = ae_re[:, :, :, None] * bb_im[None] + ae_im[:, :, :, None] * bb_re[None]
    wb_re = wb_re.transpose(1, 0, 3, 2).reshape(SSM_GROUPS, S5_LANES, SSM_STATE)
    wb_im = wb_im.transpose(1, 0, 3, 2).reshape(SSM_GROUPS, S5_LANES, SSM_STATE)
    wb = jnp.concatenate([wb_re, wb_im, wb_im, wb_re], axis=-1)
    f_idx = (n - jnp.arange(n)) if reverse else (jnp.arange(n) + 1)
    af_re, af_im = pw_re[f_idx], pw_im[f_idx]
    cf_re = c_re[None] * af_re[:, :, None, :] - c_im[None] * af_im[:, :, None, :]
    cf_im = c_re[None] * af_im[:, :, None, :] + c_im[None] * af_re[:, :, None, :]
    wc = jnp.concatenate([cf_re.transpose(1, 3, 0, 2), -cf_im.transpose(1, 3, 0, 2)], axis=1)
    wc = wc.reshape(SSM_GROUPS, 2 * SSM_STATE, S5_LANES)
    an_re, an_im = pw_re[n], pw_im[n]
    coef = jnp.stack([jnp.concatenate([an_re, an_re], axis=-1),
                      jnp.concatenate([-an_im, an_im], axis=-1)], axis=1)
    return wb.astype(jnp.bfloat16), m.astype(jnp.bfloat16), wc.astype(jnp.bfloat16), coef


def _s5_chunks(h):
    t = h.shape[1]
    nk = t // SSM_CHUNK
    u = h.reshape(B, nk, SSM_CHUNK, SSM_GROUPS, SSM_GROUP).transpose(3, 1, 0, 2, 4)
    return u.reshape(SSM_GROUPS, nk * B, S5_LANES)


def _s5_unchunk(y):
    nk = y.shape[1] // B
    y = y.reshape(SSM_GROUPS, nk, B, SSM_CHUNK, SSM_GROUP).transpose(2, 1, 3, 0, 4)
    return y.reshape(B, nk * SSM_CHUNK, D)


def _s5_post_kernel(x_ref, y_ref, g_ref, sh_ref, sc_ref, d_ref, o_ref):
    h = _rms(x_ref[...], g_ref[...]) * (1.0 + sc_ref[...]) + sh_ref[...]
    y = d_ref[...] * h + y_ref[...]
    z = 0.5 * y * (1.0 + jnp.tanh(math.sqrt(2.0 / math.pi) * (y + 0.044715 * (y * y * y))))
    o_ref[...] = z.astype(o_ref.dtype)


def _s5_post(x_lat, y_ssm, g, mod, d_skip):
    row = pl.BlockSpec((None, ROW_TILE, D), lambda b, i: (b, i, 0))
    vec = pl.BlockSpec((1, D), lambda b, i: (0, 0))
    return pl.pallas_call(
        _s5_post_kernel, grid=(B, SEQ // ROW_TILE),
        in_specs=[row, row, vec, _mod_spec(0, 0), _mod_spec(1, 0), vec],
        out_specs=row,
        out_shape=jax.ShapeDtypeStruct((B, SEQ, D), jnp.bfloat16),
        compiler_params=_cparams("parallel", "parallel"), name="s5_skip_gelu",
    )(x_lat, y_ssm, g.reshape(1, D), mod, mod, d_skip.reshape(1, D))


def _layer_modulation(c, c_ctx, w_mod, b_mod):
    cond = jnp.concatenate([c, c_ctx[None, :], jnp.zeros((MOD_ROWS - B - 1, D), jnp.float32)], axis=0)
    return _modulation(cond, w_mod, b_mod).reshape(MOD_ROWS * 6, 1, D)


def kernel(x, c, ctx, c_ctx, l0_w_mod, l0_b_mod, l0_g_pre_mix, l0_g_post_mix, l0_g_pre_ffn, l0_g_post_ffn, l0_w_in, l0_w_out, l0_lambda_q1, l0_lambda_k1, l0_lambda_q2, l0_lambda_k2, l0_g_subln, l0_g_qnorm, l0_g_knorm, l0_w_router, l0_b_router, l0_w_gate_up, l0_b_gate_up, l0_w_down, l0_b_down, l1_w_mod, l1_b_mod, l1_g_pre_mix, l1_g_post_mix, l1_g_pre_ffn, l1_g_post_ffn, l1_ssm_a_re, l1_ssm_a_im, l1_ssm_b_re, l1_ssm_b_im, l1_ssm_c_re, l1_ssm_c_im, l1_ssm_log_dt, l1_ssm_d, l1_w_glu, l1_w_router, l1_b_router, l1_w_gate_up, l1_b_gate_up, l1_w_down, l1_b_down):
    xs = jnp.concatenate([ctx, x], axis=1)

    mod = _layer_modulation(c, c_ctx, l0_w_mod, l0_b_mod)
    h = _norm_mod(xs, l0_g_pre_mix, mod, 0, 1, 1)
    cos2, sin2 = _rope_tables()
    p = _inproj(h.reshape(B * TOK, D), l0_w_in, cos2, sin2, l0_g_qnorm, l0_g_knorm)
    lam_params = jnp.stack([l0_lambda_q1, l0_lambda_k1, l0_lambda_q2, l0_lambda_k2])
    lambda_init = 0.8 - 0.6 * math.exp(-0.3 * 0)
    att = _attention(p.reshape(B, TOK, ATTN_IN), lam_params, l0_g_subln, lambda_init)
    y = _matmul(att.reshape(B * TOK, D), l0_w_out, 1024, 512, name="attn_outproj")
    xs = _post_norm_residual(xs, y.reshape(B, TOK, D), l0_g_post_mix, mod, 2, 1)
    hp, meta, cnt = _norm_mod(xs, l0_g_pre_ffn, mod, 3, 4, 1, router=(l0_w_router, l0_b_router))
    xs = _moe_ffn(xs.reshape(B * TOK, D), hp.reshape(B * TOK, HALF_D), meta.reshape(B * TOK, META_LANES), cnt,
                  l0_g_post_ffn, mod, 5, TOK, CTX,
                  l0_w_gate_up, l0_b_gate_up, l0_w_down, l0_b_down).reshape(B, TOK, D)

    mod = _layer_modulation(c, c_ctx, l1_w_mod, l1_b_mod)
    h = _norm_mod(xs, l1_g_pre_mix, mod, 0, 1, 1)
    u_ctx = _s5_chunks(h[:, :CTX])
    u_lat = _s5_chunks(h[:, CTX:])
    mats = [_s5_matrices(l1_ssm_a_re[d], l1_ssm_a_im[d], l1_ssm_b_re[d], l1_ssm_b_im[d],
                         l1_ssm_c_re[d], l1_ssm_c_im[d], l1_ssm_log_dt[d], reverse=bool(d))
            for d in range(2)]
    wb, m, wc, coef = (jnp.stack([mats[0][i], mats[1][i]]) for i in range(4))
    y_ssm = _s5_unchunk(_s5_scan(u_ctx, u_lat, wb, m, wc, coef))
    x_lat = xs[:, CTX:]
    z = _s5_post(x_lat, y_ssm, l1_g_pre_mix, mod, l1_ssm_d)
    y = _glu_matmul(z.reshape(B * SEQ, D), l1_w_glu, 1024, 512)
    x_lat = _post_norm_residual(x_lat, y.reshape(B, SEQ, D), l1_g_post_mix, mod, 2, 0)
    hp, meta, cnt = _norm_mod(x_lat, l1_g_pre_ffn, mod, 3, 4, 0, router=(l1_w_router, l1_b_router))
    return _moe_ffn(x_lat.reshape(B * SEQ, D), hp.reshape(B * SEQ, HALF_D), meta.reshape(B * SEQ, META_LANES), cnt,
                    l1_g_post_ffn, mod, 5, SEQ, 0,
                    l1_w_gate_up, l1_b_gate_up, l1_w_down, l1_b_down).reshape(B, SEQ, D)
```

```python
import functools
import math

import jax
import jax.numpy as jnp
import numpy as np
from jax import lax
from jax.experimental import pallas as pl
from jax.experimental.pallas import tpu as pltpu

D = 2048
B = 4
SEQ = 2048
CTX = 256
TOK = CTX + SEQ
GRID_W = 64
HD = 128
DIFF_HEADS = 4
GQA_Q_HEADS = 8
GQA_KV_HEADS = 2
GQA_GROUP = GQA_Q_HEADS // GQA_KV_HEADS
ROPE_THETA = 10000.0
ROPE_FREQS = HD // 4
ATTN_IN = 4608
N_EXPERTS = 32
TOP_K = 4
D_FF = D
SWIGLU_LIMIT = 7.0
SWIGLU_ALPHA = 1.702
RMS_EPS = 1e-6
SSM_GROUP = 16
SSM_STATE = 64
SSM_GROUPS = D // SSM_GROUP
SSM_CHUNK = 16

ROW_TILE = 256
MOD_ROWS = 8
MOD_CTX_ROW = B

V7X_VMEM_BYTES = 64 * 1024 * 1024
VMEM_LIMIT = 56 * 1024 * 1024


def _cparams(*sem):
    return pltpu.CompilerParams(dimension_semantics=sem, vmem_limit_bytes=VMEM_LIMIT)


def _rms(x, g):
    return x * lax.rsqrt(jnp.mean(x * x, axis=-1, keepdims=True) + RMS_EPS) * g


def _sigmoid(x):
    return 1.0 / (1.0 + jnp.exp(-x))


def _mod_kernel(c_ref, w_ref, b_ref, o_ref):
    c = c_ref[...]
    a = (c * _sigmoid(c)).astype(jnp.bfloat16)
    o_ref[...] = jnp.dot(a, w_ref[...].astype(jnp.bfloat16),
                         preferred_element_type=jnp.float32) + b_ref[...]


def _modulation(cond, w_mod, b_mod):
    tn = 1024
    n = w_mod.shape[1]
    return pl.pallas_call(
        _mod_kernel,
        grid=(n // tn,),
        in_specs=[pl.BlockSpec((MOD_ROWS, D), lambda j: (0, 0)),
                  pl.BlockSpec((D, tn), lambda j: (0, j)),
                  pl.BlockSpec((1, tn), lambda j: (0, j))],
        out_specs=pl.BlockSpec((MOD_ROWS, tn), lambda j: (0, j)),
        out_shape=jax.ShapeDtypeStruct((MOD_ROWS, n), jnp.float32),
        compiler_params=_cparams("arbitrary"),
        name="adaln_modulation",
    )(cond, w_mod, b_mod.reshape(1, n))


def _mod_spec(which, n_ctx_blocks):
    def idx(b, i):
        r = jnp.where(i < n_ctx_blocks, MOD_CTX_ROW, b)
        return (r * 6 + which, 0, 0)
    return pl.BlockSpec((None, 1, D), idx)


def _norm_mod_kernel(x_ref, g_ref, sh_ref, sc_ref, o_ref):
    h = _rms(x_ref[...], g_ref[...]) * (1.0 + sc_ref[...]) + sh_ref[...]
    o_ref[...] = h.astype(o_ref.dtype)


META_LANES = 128
META_E = 0
META_RANK = TOP_K
META_GATE = 2 * TOP_K
HALF_D = D // 2
HI_MASK = 0xFFFF0000


def _pack_bf16_pair(lo, hi):
    ulo = pltpu.bitcast(lo.astype(jnp.bfloat16).astype(jnp.float32), jnp.uint32)
    uhi = pltpu.bitcast(hi.astype(jnp.bfloat16).astype(jnp.float32), jnp.uint32)
    return lax.shift_right_logical(ulo, jnp.uint32(16)) | (uhi & jnp.uint32(HI_MASK))


def _unpack_bf16_pair(w):
    lo = pltpu.bitcast(lax.shift_left(w, jnp.uint32(16)), jnp.float32).astype(jnp.bfloat16)
    hi = pltpu.bitcast(w & jnp.uint32(HI_MASK), jnp.float32).astype(jnp.bfloat16)
    return lo, hi


def _norm_mod_router_kernel(x_ref, g_ref, sh_ref, sc_ref, wr_ref, br_ref, o_ref, meta_ref, cnt_ref, run_ref):
    first = jnp.logical_and(pl.program_id(0) == 0, pl.program_id(1) == 0)

    @pl.when(first)
    def _():
        run_ref[...] = jnp.zeros_like(run_ref)

    h = _rms(x_ref[...], g_ref[...]) * (1.0 + sc_ref[...]) + sh_ref[...]
    o_ref[...] = _pack_bf16_pair(h[:, :HALF_D], h[:, HALF_D:])
    logits = jnp.dot(h, wr_ref[...], preferred_element_type=jnp.float32,
                     precision=lax.Precision.HIGHEST) + br_ref[...]
    lane = lax.broadcasted_iota(jnp.int32, (ROW_TILE, N_EXPERTS), 1)
    vals, hots = [], []
    l = logits
    for _ in range(TOP_K):
        m = jnp.max(l, axis=-1, keepdims=True)
        idx = jnp.min(jnp.where(l == m, lane, N_EXPERTS), axis=-1, keepdims=True)
        hot = lane == idx
        vals.append(m)
        hots.append(hot)
        l = jnp.where(hot, -jnp.inf, l)
    es = [jnp.exp(v - vals[0]) for v in vals]
    den = es[0] + es[1] + es[2] + es[3]
    onehot = sum(hot.astype(jnp.float32) for hot in hots)
    r_i = lax.broadcasted_iota(jnp.int32, (ROW_TILE, ROW_TILE), 0)
    c_i = lax.broadcasted_iota(jnp.int32, (ROW_TILE, ROW_TILE), 1)
    lower = jnp.where(r_i > c_i, 1.0, 0.0).astype(jnp.bfloat16)
    before = jnp.dot(lower, onehot.astype(jnp.bfloat16), preferred_element_type=jnp.float32) + run_ref[0:1, 0:N_EXPERTS]
    mlane = lax.broadcasted_iota(jnp.int32, (ROW_TILE, META_LANES), 1)
    lane_f = lane.astype(jnp.float32)
    meta = jnp.zeros((ROW_TILE, META_LANES), jnp.float32)
    for k in range(TOP_K):
        hot_f = hots[k].astype(jnp.float32)
        e_k = jnp.sum(hot_f * lane_f, axis=-1, keepdims=True)
        rank_k = jnp.sum(hot_f * before, axis=-1, keepdims=True)
        meta = jnp.where(mlane == META_E + k, e_k, meta)
        meta = jnp.where(mlane == META_RANK + k, rank_k, meta)
        meta = jnp.where(mlane == META_GATE + k, es[k] / den, meta)
    meta_ref[...] = meta
    total = run_ref[0:1, 0:N_EXPERTS] + jnp.sum(onehot, axis=0, keepdims=True)
    run_ref[0:1, 0:N_EXPERTS] = total
    cnt_ref[...] = jnp.broadcast_to(run_ref[0:1, :], cnt_ref.shape)


def _norm_mod(x, g, mod, shift_idx, scale_idx, n_ctx_blocks, router=None):
    t = x.shape[1]
    grid = (B, t // ROW_TILE)
    row = pl.BlockSpec((None, ROW_TILE, D), lambda b, i: (b, i, 0))
    in_specs = [row, pl.BlockSpec((1, D), lambda b, i: (0, 0)),
                _mod_spec(shift_idx, n_ctx_blocks), _mod_spec(scale_idx, n_ctx_blocks)]
    args = [x, g.reshape(1, D), mod, mod]
    if router is None:
        return pl.pallas_call(
            _norm_mod_kernel, grid=grid, in_specs=in_specs, out_specs=row,
            out_shape=jax.ShapeDtypeStruct((B, t, D), jnp.bfloat16),
            compiler_params=_cparams("parallel", "parallel"), name="norm_mod",
        )(*args)
    w_router, b_router = router
    in_specs += [pl.BlockSpec((D, N_EXPERTS), lambda b, i: (0, 0)),
                 pl.BlockSpec((1, N_EXPERTS), lambda b, i: (0, 0))]
    args += [w_router, b_router.reshape(1, N_EXPERTS)]
    return pl.pallas_call(
        _norm_mod_router_kernel, grid=grid, in_specs=in_specs,
        out_specs=[pl.BlockSpec((None, ROW_TILE, HALF_D), lambda b, i: (b, i, 0)),
                   pl.BlockSpec((None, ROW_TILE, META_LANES), lambda b, i: (b, i, 0)),
                   pl.BlockSpec((8, META_LANES), lambda b, i: (0, 0))],
        out_shape=[jax.ShapeDtypeStruct((B, t, HALF_D), jnp.uint32),
                   jax.ShapeDtypeStruct((B, t, META_LANES), jnp.float32),
                   jax.ShapeDtypeStruct((8, META_LANES), jnp.float32)],
        scratch_shapes=[pltpu.VMEM((8, META_LANES), jnp.float32)],
        compiler_params=_cparams("arbitrary", "arbitrary"), name="norm_mod_router",
    )(*args)


def _post_norm_kernel(x_ref, y_ref, g_ref, gt_ref, o_ref):
    o_ref[...] = x_ref[...] + gt_ref[...] * _rms(y_ref[...], g_ref[...])


def _post_norm_residual(x, y, g, mod, gate_idx, n_ctx_blocks):
    t = x.shape[1]
    row = pl.BlockSpec((None, ROW_TILE, D), lambda b, i: (b, i, 0))
    return pl.pallas_call(
        _post_norm_kernel, grid=(B, t // ROW_TILE),
        in_specs=[row, row, pl.BlockSpec((1, D), lambda b, i: (0, 0)),
                  _mod_spec(gate_idx, n_ctx_blocks)],
        out_specs=row,
        out_shape=jax.ShapeDtypeStruct((B, t, D), jnp.float32),
        input_output_aliases={0: 0},
        compiler_params=_cparams("parallel", "parallel"), name="post_norm_residual",
    )(x, y, g.reshape(1, D), mod)


def _matmul_kernel(a_ref, w_ref, o_ref):
    o_ref[...] = jnp.dot(a_ref[...], w_ref[...].astype(jnp.bfloat16),
                         preferred_element_type=jnp.float32).astype(o_ref.dtype)


def _matmul(a, w, tm, tn, out_dtype=jnp.float32, name="matmul"):
    m, k = a.shape
    n = w.shape[1]
    return pl.pallas_call(
        _matmul_kernel, grid=(m // tm, n // tn),
        in_specs=[pl.BlockSpec((tm, k), lambda i, j: (i, 0)),
                  pl.BlockSpec((k, tn), lambda i, j: (0, j))],
        out_specs=pl.BlockSpec((tm, tn), lambda i, j: (i, j)),
        out_shape=jax.ShapeDtypeStruct((m, n), out_dtype),
        compiler_params=_cparams("parallel", "arbitrary"), name=name,
    )(a, w)


def _glu_matmul_kernel(a_ref, wv_ref, wg_ref, o_ref):
    a = a_ref[...]
    val = jnp.dot(a, wv_ref[...].astype(jnp.bfloat16), preferred_element_type=jnp.float32)
    gate = jnp.dot(a, wg_ref[...].astype(jnp.bfloat16), preferred_element_type=jnp.float32)
    o_ref[...] = val * _sigmoid(gate)


def _glu_matmul(a, w_glu, tm, tn):
    m, k = a.shape
    n = w_glu.shape[1] // 2
    nj = n // tn
    return pl.pallas_call(
        _glu_matmul_kernel, grid=(m // tm, nj),
        in_specs=[pl.BlockSpec((tm, k), lambda i, j: (i, 0)),
                  pl.BlockSpec((k, tn), lambda i, j: (0, j)),
                  pl.BlockSpec((k, tn), lambda i, j: (0, nj + j))],
        out_specs=pl.BlockSpec((tm, tn), lambda i, j: (i, j)),
        out_shape=jax.ShapeDtypeStruct((m, n), jnp.float32),
        compiler_params=_cparams("parallel", "arbitrary"), name="glu_matmul",
    )(a, w_glu, w_glu)


IN_TN = 256
IN_ROPE_END = 8
IN_DV_END = 12
IN_GQ_END = 16
IN_GK_TILE = 16


def _inproj_kernel(a_ref, w_ref, cos_ref, sin_ref, gq_ref, gk_ref, o_ref):
    j = pl.program_id(1)
    acc = jnp.dot(a_ref[...], w_ref[...].astype(jnp.bfloat16), preferred_element_type=jnp.float32)

    def rope(x):
        return x * cos_ref[...] + pltpu.roll(x, HD // 2, 1) * sin_ref[...]

    def store(fn):
        for c in range(IN_TN // HD):
            o_ref[:, c * HD:(c + 1) * HD] = fn(acc[:, c * HD:(c + 1) * HD]).astype(o_ref.dtype)

    @pl.when(j < IN_ROPE_END)
    def _():
        store(rope)

    @pl.when(jnp.logical_or(jnp.logical_and(j >= IN_ROPE_END, j < IN_DV_END), j > IN_GK_TILE))
    def _():
        store(lambda x: x)

    @pl.when(jnp.logical_and(j >= IN_DV_END, j < IN_GQ_END))
    def _():
        store(lambda x: rope(_rms(x, gq_ref[...])))

    @pl.when(j == IN_GK_TILE)
    def _():
        store(lambda x: rope(_rms(x, gk_ref[...])))


def _inproj(h, w_in, cos2, sin2, g_q, g_k):
    m = h.shape[0]
    const = lambda i, j: (0, 0)
    return pl.pallas_call(
        _inproj_kernel, grid=(m // TOK, ATTN_IN // IN_TN),
        in_specs=[pl.BlockSpec((TOK, D), lambda i, j: (i, 0)),
                  pl.BlockSpec((D, IN_TN), lambda i, j: (0, j)),
                  pl.BlockSpec((TOK, HD), const), pl.BlockSpec((TOK, HD), const),
                  pl.BlockSpec((1, HD), const), pl.BlockSpec((1, HD), const)],
        out_specs=pl.BlockSpec((TOK, IN_TN), lambda i, j: (i, j)),
        out_shape=jax.ShapeDtypeStruct((m, ATTN_IN), jnp.bfloat16),
        compiler_params=_cparams("parallel", "arbitrary"), name="attn_inproj",
    )(h, w_in, cos2, sin2, g_q.reshape(1, HD), g_k.reshape(1, HD))


ATT_TQ = 256
ATT_SCALE = HD ** -0.5


def _softmax_pv(q, k, v):
    s = lax.dot_general(q, k, (((1,), (1,)), ((), ())), preferred_element_type=jnp.float32)
    m = jnp.max(s, axis=-1, keepdims=True)
    e = jnp.exp((s - m) * ATT_SCALE)
    l = jnp.sum(e, axis=-1, keepdims=True)
    return jnp.dot(e.astype(jnp.bfloat16), v, preferred_element_type=jnp.float32), l


def _diff_attn_kernel(lam_ref, q_ref, k_ref, v_ref, g_ref, o_ref, *, lambda_init):
    lp = lam_ref[...]
    lam = (jnp.exp(jnp.sum(lp[0:1] * lp[1:2], axis=-1, keepdims=True))
           - jnp.exp(jnp.sum(lp[2:3] * lp[3:4], axis=-1, keepdims=True)) + lambda_init)

    def run(nk):
        q = q_ref[...]
        k = k_ref[0:nk, :]
        v = v_ref[0:nk, :]
        pv1, l1 = _softmax_pv(q[:, :HD], k[:, :HD], v)
        pv2, l2 = _softmax_pv(q[:, HD:], k[:, HD:], v)
        o = pv1 / l1 - lam * (pv2 / l2)
        o_ref[...] = (_rms(o, g_ref[...]) * (1.0 - lambda_init)).astype(o_ref.dtype)

    @pl.when(pl.program_id(2) == 0)
    def _():
        run(CTX)

    @pl.when(pl.program_id(2) > 0)
    def _():
        run(TOK)


def _gqa_attn_kernel(q_ref, k_ref, v_ref, o_ref):
    def run(nk):
        k = k_ref[0:nk, :]
        v = v_ref[0:nk, :]
        for g in range(GQA_GROUP):
            pv, l = _softmax_pv(q_ref[:, g * HD:(g + 1) * HD], k, v)
            o_ref[:, g * HD:(g + 1) * HD] = (pv / l).astype(o_ref.dtype)

    @pl.when(pl.program_id(2) == 0)
    def _():
        run(CTX)

    @pl.when(pl.program_id(2) > 0)
    def _():
        run(TOK)


def _attention(p, lam_params, g_subln, lambda_init):
    nq = TOK // ATT_TQ
    dv = 2 * HD
    od = pl.pallas_call(
        functools.partial(_diff_attn_kernel, lambda_init=lambda_init),
        grid=(B, DIFF_HEADS, nq),
        in_specs=[pl.BlockSpec((4, HD), lambda b, h, i: (0, 0)),
                  pl.BlockSpec((None, ATT_TQ, dv), lambda b, h, i: (b, i, h)),
                  pl.BlockSpec((None, TOK, dv), lambda b, h, i: (b, 0, DIFF_HEADS + h)),
                  pl.BlockSpec((None, TOK, dv), lambda b, h, i: (b, 0, 2 * DIFF_HEADS + h)),
                  pl.BlockSpec((1, dv), lambda b, h, i: (0, 0))],
        out_specs=pl.BlockSpec((None, ATT_TQ, dv), lambda b, h, i: (b, i, h)),
        out_shape=jax.ShapeDtypeStruct((B, TOK, DIFF_HEADS * dv), jnp.bfloat16),
        compiler_params=_cparams("parallel", "parallel", "arbitrary"), name="diff_attention",
    )(lam_params, p, p, p, g_subln.reshape(1, dv))
    gq_w = GQA_GROUP * HD
    gq0 = 3072 // gq_w
    gk0 = 4096 // HD
    gv0 = 4352 // HD
    og = pl.pallas_call(
        _gqa_attn_kernel,
        grid=(B, GQA_KV_HEADS, nq),
        in_specs=[pl.BlockSpec((None, ATT_TQ, gq_w), lambda b, n, i: (b, i, gq0 + n)),
                  pl.BlockSpec((None, TOK, HD), lambda b, n, i: (b, 0, gk0 + n)),
                  pl.BlockSpec((None, TOK, HD), lambda b, n, i: (b, 0, gv0 + n))],
        out_specs=pl.BlockSpec((None, ATT_TQ, gq_w), lambda b, n, i: (b, i, n)),
        out_shape=jax.ShapeDtypeStruct((B, TOK, GQA_Q_HEADS * HD), jnp.bfloat16),
        compiler_params=_cparams("parallel", "parallel", "arbitrary"), name="gqa_attention",
    )(p, p, p)
    return jnp.concatenate([od, og], axis=-1)


def _rope_tables():
    rows = SEQ // GRID_W
    row_id, col_id = jnp.meshgrid(jnp.arange(rows), jnp.arange(GRID_W), indexing="ij")
    inv_freq = ROPE_THETA ** (-jnp.arange(ROPE_FREQS, dtype=jnp.float32) / ROPE_FREQS)
    ang = jnp.concatenate([row_id.reshape(-1, 1) * inv_freq, col_id.reshape(-1, 1) * inv_freq], axis=-1)
    cos, sin = jnp.cos(ang), jnp.sin(ang)
    cos2 = jnp.concatenate([cos, cos], axis=-1)
    sin2 = jnp.concatenate([-sin, sin], axis=-1)
    cos2 = jnp.concatenate([jnp.ones((CTX, HD), jnp.float32), cos2], axis=0)
    sin2 = jnp.concatenate([jnp.zeros((CTX, HD), jnp.float32), sin2], axis=0)
    return cos2, sin2


MOE_TM = 512
MOE_SUB = 256
MOE_TF = 1024
DISPATCH_TT = 256
COMBINE_TT = 64


def _moe_up_kernel(be_ref, first_ref, rows_ref, nxt_ref, x_ref, w_hbm, bg_ref, bl_ref, o_ref,
                   wst, wbf, sem):
    f = pl.program_id(0)
    b = pl.program_id(1)
    nf = pl.num_programs(0)

    def copies(e, ff):
        col = pl.multiple_of(ff * MOE_TF, MOE_TF)
        return [pltpu.make_async_copy(w_hbm.at[e, :, pl.ds(part * D_FF + col, MOE_TF)],
                                      wst.at[part], sem.at[part]) for part in range(2)]

    @pl.when(first_ref[b] == 1)
    def _():
        @pl.when(jnp.logical_and(f == 0, b == 0))
        def _():
            for c in copies(be_ref[0], 0):
                c.start()

        for c in copies(be_ref[b], f):
            c.wait()
        wbf[...] = wst[...].astype(jnp.bfloat16)
        e_next = nxt_ref[b]

        @pl.when(e_next >= 0)
        def _():
            for c in copies(e_next, f):
                c.start()

        @pl.when(jnp.logical_and(e_next < 0, f + 1 < nf))
        def _():
            for c in copies(be_ref[0], f + 1):
                c.start()

    for s in range(MOE_TM // MOE_SUB):
        sl = slice(s * MOE_SUB, (s + 1) * MOE_SUB)

        @pl.when(rows_ref[b] > s * MOE_SUB)
        def _():
            lo, hi = _unpack_bf16_pair(x_ref[sl, :])
            glu = (jnp.dot(lo, wbf[0, :HALF_D, :], preferred_element_type=jnp.float32)
                   + jnp.dot(hi, wbf[0, HALF_D:, :], preferred_element_type=jnp.float32) + bg_ref[...])
            lin = (jnp.dot(lo, wbf[1, :HALF_D, :], preferred_element_type=jnp.float32)
                   + jnp.dot(hi, wbf[1, HALF_D:, :], preferred_element_type=jnp.float32) + bl_ref[...])
            glu = jnp.minimum(glu, SWIGLU_LIMIT)
            lin = jnp.clip(lin, -SWIGLU_LIMIT, SWIGLU_LIMIT)
            o_ref[sl, :] = (glu * _sigmoid(SWIGLU_ALPHA * glu) * (lin + 1.0)).astype(o_ref.dtype)

        @pl.when(rows_ref[b] <= s * MOE_SUB)
        def _():
            o_ref[sl, :] = jnp.zeros((MOE_SUB, MOE_TF), o_ref.dtype)


def _moe_down_kernel(be_ref, first_ref, rows_ref, nxt_ref, a_ref, w_hbm, bias_ref, o_ref, wst, wbf, sem):
    b = pl.program_id(0)

    def copy(e):
        return pltpu.make_async_copy(w_hbm.at[e], wst, sem)

    @pl.when(first_ref[b] == 1)
    def _():
        @pl.when(b == 0)
        def _():
            copy(be_ref[0]).start()

        copy(be_ref[b]).wait()
        wbf[...] = wst[...].astype(jnp.bfloat16)
        e_next = nxt_ref[b]

        @pl.when(e_next >= 0)
        def _():
            copy(e_next).start()

    for s in range(MOE_TM // MOE_SUB):
        sl = slice(s * MOE_SUB, (s + 1) * MOE_SUB)

        @pl.when(rows_ref[b] > s * MOE_SUB)
        def _():
            o_ref[sl, :] = jnp.dot(a_ref[sl, :], wbf[...],
                                   preferred_element_type=jnp.float32) + bias_ref[...]

        @pl.when(rows_ref[b] <= s * MOE_SUB)
        def _():
            o_ref[sl, :] = jnp.zeros((MOE_SUB, D), o_ref.dtype)


def _moe_experts(x_sorted, tables, w_gate_up, b_gate_up, w_down, b_down):
    r = x_sorted.shape[0]
    nb = r // MOE_TM
    nf = D_FF // MOE_TF
    bgu = b_gate_up.reshape(N_EXPERTS, 1, 2 * D_FF)
    act = pl.pallas_call(
        _moe_up_kernel,
        grid_spec=pltpu.PrefetchScalarGridSpec(
            num_scalar_prefetch=4, grid=(nf, nb),
            in_specs=[pl.BlockSpec((MOE_TM, HALF_D), lambda f, b, be, fi, ro, nx: (b, 0)),
                      pl.BlockSpec(memory_space=pl.ANY),
                      pl.BlockSpec((None, 1, MOE_TF), lambda f, b, be, fi, ro, nx: (be[b], 0, f)),
                      pl.BlockSpec((None, 1, MOE_TF), lambda f, b, be, fi, ro, nx: (be[b], 0, nf + f))],
            out_specs=pl.BlockSpec((MOE_TM, MOE_TF), lambda f, b, be, fi, ro, nx: (b, f)),
            scratch_shapes=[pltpu.VMEM((2, D, MOE_TF), jnp.float32), pltpu.VMEM((2, D, MOE_TF), jnp.bfloat16),
                            pltpu.SemaphoreType.DMA((2,))]),
        out_shape=jax.ShapeDtypeStruct((r, D_FF), jnp.bfloat16),
        compiler_params=_cparams("arbitrary", "arbitrary"), name="moe_gate_up",
    )(*tables, x_sorted, w_gate_up, bgu, bgu)
    return pl.pallas_call(
        _moe_down_kernel,
        grid_spec=pltpu.PrefetchScalarGridSpec(
            num_scalar_prefetch=4, grid=(nb,),
            in_specs=[pl.BlockSpec((MOE_TM, D_FF), lambda b, be, fi, ro, nx: (b, 0)),
                      pl.BlockSpec(memory_space=pl.ANY),
                      pl.BlockSpec((None, 1, D), lambda b, be, fi, ro, nx: (be[b], 0, 0))],
            out_specs=pl.BlockSpec((MOE_TM, D), lambda b, be, fi, ro, nx: (b, 0)),
            scratch_shapes=[pltpu.VMEM((D_FF, D), jnp.float32), pltpu.VMEM((D_FF, D), jnp.bfloat16),
                            pltpu.SemaphoreType.DMA]),
        out_shape=jax.ShapeDtypeStruct((r, D), jnp.float32),
        compiler_params=_cparams("arbitrary"), name="moe_down",
    )(*tables, act, w_down, b_down.reshape(N_EXPERTS, 1, D))


def _dispatch_kernel(dest_ref, hp_ref, xs_in_hbm, xs_hbm, idx_smem, sem_idx, sem_rows):
    del xs_in_hbm
    idx_copy = pltpu.make_async_copy(dest_ref, idx_smem, sem_idx)
    idx_copy.start()
    idx_copy.wait()

    def issue(t, carry):
        for k in range(TOP_K):
            d = idx_smem[0, 0, t * TOP_K + k]
            pltpu.make_async_copy(hp_ref.at[t], xs_hbm.at[d], sem_rows).start(priority=k % 2)
        return carry

    lax.fori_loop(0, DISPATCH_TT, issue, 0, unroll=4)
    for k in range(TOP_K):
        pltpu.make_async_copy(hp_ref, xs_hbm.at[pl.ds(0, DISPATCH_TT)], sem_rows).wait()


def _dispatch(dest, hp, n_rows):
    n_tok = hp.shape[0]
    n = n_tok // DISPATCH_TT
    width = DISPATCH_TT * TOP_K
    zeros = jnp.zeros((n_rows, HALF_D), jnp.uint32)
    return pl.pallas_call(
        _dispatch_kernel, grid=(n,),
        in_specs=[pl.BlockSpec((1, 1, width), lambda i: (i, 0, 0)),
                  pl.BlockSpec((DISPATCH_TT, HALF_D), lambda i: (i, 0)), pl.BlockSpec(memory_space=pl.ANY)],
        out_specs=pl.BlockSpec(memory_space=pl.ANY),
        out_shape=jax.ShapeDtypeStruct((n_rows, HALF_D), jnp.uint32),
        scratch_shapes=[pltpu.SMEM((1, 1, width), jnp.int32), pltpu.SemaphoreType.DMA, pltpu.SemaphoreType.DMA],
        input_output_aliases={2: 0},
        compiler_params=_cparams("arbitrary"), name="moe_dispatch",
    )(dest.reshape(n, 1, width), hp, zeros)


def _combine_kernel(dcur_ref, dnext_ref, out_hbm, meta_ref, x_ref, g_ref, gt_ref, o_ref,
                    idx_smem, buf, sem_idx, sem_rows):
    i = pl.program_id(0)
    n = pl.num_programs(0)

    def row_copy(d, slot, k, t):
        return pltpu.make_async_copy(out_hbm.at[d], buf.at[slot, k, t], sem_rows.at[slot])

    def gather(d_ref, slot):
        idx_copy = pltpu.make_async_copy(d_ref, idx_smem, sem_idx)
        idx_copy.start()
        idx_copy.wait()

        def issue(t, carry):
            for k in range(TOP_K):
                row_copy(idx_smem[0, 0, t * TOP_K + k], slot, k, t).start(priority=k % 2)
            return carry
        lax.fori_loop(0, COMBINE_TT, issue, 0, unroll=4)

    @pl.when(i == 0)
    def _():
        gather(dcur_ref, 0)

    @pl.when(i + 1 < n)
    def _():
        gather(dnext_ref, (i + 1) % 2)

    slot = i % 2
    pltpu.make_async_copy(buf.at[slot], buf.at[slot], sem_rows.at[slot]).wait()

    meta = meta_ref[...]
    f = meta[:, META_GATE:META_GATE + 1] * buf[slot, 0]
    for k in range(1, TOP_K):
        f = f + meta[:, META_GATE + k:META_GATE + k + 1] * buf[slot, k]
    o_ref[...] = x_ref[...] + gt_ref[...] * _rms(f, g_ref[...])


def _combine(dest, out_sorted, meta, x, g, mod, gate_idx, rows_per_batch, ctx_rows):
    n_tok = x.shape[0]
    n = n_tok // COMBINE_TT
    width = COMBINE_TT * TOP_K
    per_batch = rows_per_batch // COMBINE_TT
    ctx_blocks = ctx_rows // COMBINE_TT

    def gate_row(i):
        r = jnp.where(i % per_batch < ctx_blocks, MOD_CTX_ROW, i // per_batch)
        return (r * 6 + gate_idx, 0, 0)

    row = pl.BlockSpec((COMBINE_TT, D), lambda i: (i, 0))
    return pl.pallas_call(
        _combine_kernel, grid=(n,),
        in_specs=[pl.BlockSpec((1, 1, width), lambda i: (i, 0, 0)),
                  pl.BlockSpec((1, 1, width), lambda i: (jnp.minimum(i + 1, n - 1), 0, 0)),
                  pl.BlockSpec(memory_space=pl.ANY),
                  pl.BlockSpec((COMBINE_TT, META_LANES), lambda i: (i, 0)),
                  row, pl.BlockSpec((1, D), lambda i: (0, 0)),
                  pl.BlockSpec((None, 1, D), gate_row)],
        out_specs=row,
        out_shape=jax.ShapeDtypeStruct((n_tok, D), jnp.float32),
        scratch_shapes=[pltpu.SMEM((1, 1, width), jnp.int32),
                        pltpu.VMEM((2, TOP_K, COMBINE_TT, D), jnp.float32),
                        pltpu.SemaphoreType.DMA, pltpu.SemaphoreType.DMA((2,))],
        input_output_aliases={4: 0},
        compiler_params=_cparams("arbitrary"), name="moe_combine",
    )(dest.reshape(n, 1, width), dest.reshape(n, 1, width), out_sorted, meta, x, g.reshape(1, D), mod)


def _moe_block_tables(counts, n_assign):
    padded = (counts + MOE_TM - 1) // MOE_TM * MOE_TM
    pad_end = jnp.cumsum(padded)
    pad_start = pad_end - padded
    nb = -(-(n_assign + N_EXPERTS * (MOE_TM - 1)) // MOE_TM)
    block_start = jnp.arange(nb, dtype=jnp.int32) * MOE_TM
    block_e_raw = jnp.minimum(jnp.sum(block_start[:, None] >= pad_end[None, :], axis=1), N_EXPERTS - 1)
    block_e_raw = block_e_raw.astype(jnp.int32)
    onehot_e = block_e_raw[:, None] == jnp.arange(N_EXPERTS)[None, :]
    valid_end = jnp.sum(jnp.where(onehot_e, (pad_start + counts)[None, :], 0), axis=1)
    block_rows = jnp.clip(valid_end - block_start, 0, MOE_TM).astype(jnp.int32)
    block_rows = jnp.where(block_start < pad_end[-1], block_rows, 0)
    last_e = jnp.max(jnp.where(block_rows > 0, block_e_raw, 0))
    block_e = jnp.where(block_rows > 0, block_e_raw, last_e)
    block_first = jnp.concatenate([jnp.ones((1,), jnp.int32),
                                   (block_e[1:] != block_e[:-1]).astype(jnp.int32)])
    idx = jnp.arange(nb, dtype=jnp.int32)
    first_pos = jnp.where(block_first == 1, idx, nb)
    later = jnp.where(idx[None, :] > idx[:, None], first_pos[None, :], nb)
    next_pos = jnp.min(later, axis=1)
    next_e = jnp.sum(jnp.where(idx[None, :] == next_pos[:, None], block_e[None, :], 0), axis=1)
    block_next = jnp.where(next_pos < nb, next_e, -1).astype(jnp.int32)
    return pad_start, nb * MOE_TM, (block_e, block_first, block_rows, block_next)


def _moe_ffn(x_res, hp, meta, cnt, g_post, mod, gate_idx, rows_per_batch, ctx_rows,
             w_gate_up, b_gate_up, w_down, b_down):
    n_tok = hp.shape[0]
    counts = cnt[0, :N_EXPERTS].astype(jnp.int32)
    pad_start, n_rows, tables = _moe_block_tables(counts, n_tok * TOP_K)
    top_e = meta[:, META_E:META_E + TOP_K].astype(jnp.int32)
    rank = meta[:, META_RANK:META_RANK + TOP_K].astype(jnp.int32)
    hot = top_e[:, :, None] == jnp.arange(N_EXPERTS)[None, None, :]
    dest = jnp.sum(jnp.where(hot, pad_start[None, None, :], 0), axis=-1) + rank
    x_sorted = _dispatch(dest, hp, n_rows)
    out_sorted = _moe_experts(x_sorted, tables, w_gate_up, b_gate_up, w_down, b_down)
    return _combine(dest, out_sorted, meta, x_res, g_post, mod, gate_idx, rows_per_batch, ctx_rows)


S5_GB = 8
S5_LANES = SSM_CHUNK * SSM_GROUP
S5_NK_CTX = CTX // SSM_CHUNK
S5_NK_LAT = SEQ // SSM_CHUNK
S5_PAIR = 2 * B


def _s5_direction(reverse, uc_ref, ul_ref, wb_ref, m_ref, wc_ref, coef_ref, y_ref,
                  s1c, s2c, s1l, s2l, xin, accumulate):
    half = 2 * SSM_STATE
    for g in range(S5_GB):
        sc = jnp.dot(uc_ref[g], wb_ref[g], preferred_element_type=jnp.float32)
        s1c[g] = sc[:, :half]
        s2c[g] = sc[:, half:]
        sl = jnp.dot(ul_ref[g], wb_ref[g], preferred_element_type=jnp.float32)
        s1l[g] = sl[:, :half]
        s2l[g] = sl[:, half:]

    lower = lax.broadcasted_iota(jnp.int32, (S5_PAIR, half), 0) < B
    p1 = [jnp.broadcast_to(coef_ref[g, 0:1, :], (S5_PAIR, half)) for g in range(S5_GB)]
    p2 = [jnp.broadcast_to(coef_ref[g, 1:2, :], (S5_PAIR, half)) for g in range(S5_GB)]

    def tile_step(g, t1, t2, v1, v2):
        y1a = p1[g] * v1 + p2[g] * v2 + t1
        y1b = p1[g] * v2 - p2[g] * v1 + t2
        r1a = pltpu.roll(y1a, B, 0)
        r1b = pltpu.roll(y1b, B, 0)
        y2a = p1[g] * r1a + p2[g] * r1b + t1
        y2b = p1[g] * r1b - p2[g] * r1a + t2
        r2a = pltpu.roll(y2a, B, 0)
        r2b = pltpu.roll(y2b, B, 0)
        if not reverse:
            x_in = jnp.where(lower, v1, r1a)
            return x_in, jnp.where(lower, r2a, y2a), jnp.where(lower, r2b, y2b)
        x_in = jnp.where(lower, r1a, v1)
        return x_in, jnp.where(lower, y2a, r2a), jnp.where(lower, y2b, r2b)

    def scan(s1, s2, n_tiles, state, record):
        def body(j, carry):
            jj = (n_tiles - 1 - j) if reverse else j
            r0 = pl.multiple_of(jj * S5_PAIR, S5_PAIR)
            new = []
            for g in range(S5_GB):
                v1, v2 = carry[2 * g], carry[2 * g + 1]
                x_in, v1, v2 = tile_step(g, s1[g, pl.ds(r0, S5_PAIR), :], s2[g, pl.ds(r0, S5_PAIR), :], v1, v2)
                if record:
                    xin[g, pl.ds(r0, S5_PAIR), :] = x_in
                new += [v1, v2]
            return tuple(new)
        return lax.fori_loop(0, n_tiles, body, state)

    zero = jnp.zeros((S5_PAIR, half), jnp.float32)
    state = tuple(zero for _ in range(2 * S5_GB))
    state = scan(s1c, s2c, S5_NK_CTX * B // S5_PAIR, state, False)
    scan(s1l, s2l, S5_NK_LAT * B // S5_PAIR, state, True)

    for g in range(S5_GB):
        y = (jnp.dot(ul_ref[g], m_ref[g], preferred_element_type=jnp.float32)
             + jnp.dot(xin[g].astype(jnp.bfloat16), wc_ref[g], preferred_element_type=jnp.float32))
        if accumulate:
            y_ref[g] = y_ref[g] + y
        else:
            y_ref[g] = y


def _s5_kernel(uc_ref, ul_ref, wb_ref, m_ref, wc_ref, coef_ref, y_ref, s1c, s2c, s1l, s2l, xin):
    args = (uc_ref, ul_ref, wb_ref, m_ref, wc_ref, coef_ref, y_ref, s1c, s2c, s1l, s2l, xin)

    @pl.when(pl.program_id(1) == 0)
    def _():
        _s5_direction(False, *args, accumulate=False)

    @pl.when(pl.program_id(1) == 1)
    def _():
        _s5_direction(True, *args, accumulate=True)


def _s5_scan(u_ctx, u_lat, wb, m, wc, coef):
    rc, rl = u_ctx.shape[1], u_lat.shape[1]
    half = 2 * SSM_STATE
    wspec = lambda k, n: pl.BlockSpec((None, S5_GB, k, n), lambda gi, d: (d, gi, 0, 0))
    return pl.pallas_call(
        _s5_kernel, grid=(SSM_GROUPS // S5_GB, 2),
        in_specs=[pl.BlockSpec((S5_GB, rc, S5_LANES), lambda gi, d: (gi, 0, 0)),
                  pl.BlockSpec((S5_GB, rl, S5_LANES), lambda gi, d: (gi, 0, 0)),
                  wspec(S5_LANES, 2 * half), wspec(S5_LANES, S5_LANES), wspec(half, S5_LANES),
                  wspec(2, half)],
        out_specs=pl.BlockSpec((S5_GB, rl, S5_LANES), lambda gi, d: (gi, 0, 0)),
        out_shape=jax.ShapeDtypeStruct((SSM_GROUPS, rl, S5_LANES), jnp.float32),
        scratch_shapes=[pltpu.VMEM((S5_GB, rc, half), jnp.float32), pltpu.VMEM((S5_GB, rc, half), jnp.float32),
                        pltpu.VMEM((S5_GB, rl, half), jnp.float32), pltpu.VMEM((S5_GB, rl, half), jnp.float32),
                        pltpu.VMEM((S5_GB, rl, half), jnp.float32)],
        compiler_params=_cparams("parallel", "arbitrary"), name="s5_scan",
    )(u_ctx, u_lat, wb, m, wc, coef)


def _s5_matrices(a_re, a_im, b_re, b_im, c_re, c_im, log_dt, reverse):
    hp = lax.Precision.HIGHEST
    n = SSM_CHUNK
    dt = jnp.exp(log_dt)[:, None]
    mag = jnp.exp(a_re * dt)
    ab_re, ab_im = mag * jnp.cos(a_im * dt), mag * jnp.sin(a_im * dt)
    den = a_re * a_re + a_im * a_im
    f_re = ((ab_re - 1.0) * a_re + ab_im * a_im) / den
    f_im = (ab_im * a_re - (ab_re - 1.0) * a_im) / den
    bb_re = f_re[..., None] * b_re - f_im[..., None] * b_im
    bb_im = f_re[..., None] * b_im + f_im[..., None] * b_re
    tau = jnp.arange(n + 1, dtype=jnp.float32)[:, None, None]
    pmag = jnp.exp(tau * (a_re * dt))
    pw_re, pw_im = pmag * jnp.cos(tau * (a_im * dt)), pmag * jnp.sin(tau * (a_im * dt))
    ca_re = c_re[None] * pw_re[:n, :, None, :] - c_im[None] * pw_im[:n, :, None, :]
    ca_im = c_re[None] * pw_im[:n, :, None, :] + c_im[None] * pw_re[:n, :, None, :]
    kern = (jnp.einsum("tgcp,gpd->tgcd", ca_re, bb_re, precision=hp)
            - jnp.einsum("tgcp,gpd->tgcd", ca_im, bb_im, precision=hp))
    s_idx = np.arange(n)[:, None]
    t_idx = np.arange(n)[None, :]
    lag = (s_idx - t_idx) if reverse else (t_idx - s_idx)
    pick = (lag[None] == np.arange(n)[:, None, None]).astype(np.float32)
    mk = jnp.einsum("lst,lgcd->stgcd", pick, kern, precision=hp)
    m = mk.transpose(2, 0, 4, 1, 3).reshape(SSM_GROUPS, S5_LANES, S5_LANES)
    e_idx = jnp.arange(n) if reverse else (n - 1 - jnp.arange(n))
    ae_re, ae_im = pw_re[e_idx], pw_im[e_idx]
    wb_re = ae_re[:, :, :, None] * bb_re[None] - ae_im[:, :, :, None] * bb_im[None]
    wb_im = ae_re[:, :, :, None] * bb_im[None] + ae_im[:, :, :, None] * bb_re[None]
    wb_re = wb_re.transpose(1, 0, 3, 2).reshape(SSM_GROUPS, S5_LANES, SSM_STATE)
    wb_im = wb_im.transpose(1, 0, 3, 2).reshape(SSM_GROUPS, S5_LANES, SSM_STATE)
    wb = jnp.concatenate([wb_re, wb_im, wb_im, wb_re], axis=-1)
    f_idx = (n - jnp.arange(n)) if reverse else (jnp.arange(n) + 1)
    af_re, af_im = pw_re[f_idx], pw_im[f_idx]
    cf_re = c_re[None] * af_re[:, :, None, :] - c_im[None] * af_im[:, :, None, :]
    cf_im = c_re[None] * af_im[:, :, None, :] + c_im[None] * af_re[:, :, None, :]
    wc = jnp.concatenate([cf_re.transpose(1, 3, 0, 2), -cf_im.transpose(1, 3, 0, 2)], axis=1)
    wc = wc.reshape(SSM_GROUPS, 2 * SSM_STATE, S5_LANES)
    an_re, an_im = pw_re[n], pw_im[n]
    coef = jnp.stack([jnp.concatenate([an_re, an_re], axis=-1),
                      jnp.concatenate([-an_im, an_im], axis=-1)], axis=1)
    return wb.astype(jnp.bfloat16), m.astype(jnp.bfloat16), wc.astype(jnp.bfloat16), coef


def _s5_chunks(h):
    t = h.shape[1]
    nk = t // SSM_CHUNK
    u = h.reshape(B, nk, SSM_CHUNK, SSM_GROUPS, SSM_GROUP).transpose(3, 1, 0, 2, 4)
    return u.reshape(SSM_GROUPS, nk * B, S5_LANES)


def _s5_unchunk(y):
    nk = y.shape[1] // B
    y = y.reshape(SSM_GROUPS, nk, B, SSM_CHUNK, SSM_GROUP).transpose(2, 1, 3, 0, 4)
    return y.reshape(B, nk * SSM_CHUNK, D)


def _s5_post_kernel(x_ref, y_ref, g_ref, sh_ref, sc_ref, d_ref, o_ref):
    h = _rms(x_ref[...], g_ref[...]) * (1.0 + sc_ref[...]) + sh_ref[...]
    y = d_ref[...] * h + y_ref[...]
    z = 0.5 * y * (1.0 + jnp.tanh(math.sqrt(2.0 / math.pi) * (y + 0.044715 * (y * y * y))))
    o_ref[...] = z.astype(o_ref.dtype)


def _s5_post(x_lat, y_ssm, g, mod, d_skip):
    row = pl.BlockSpec((None, ROW_TILE, D), lambda b, i: (b, i, 0))
    vec = pl.BlockSpec((1, D), lambda b, i: (0, 0))
    return pl.pallas_call(
        _s5_post_kernel, grid=(B, SEQ // ROW_TILE),
        in_specs=[row, row, vec, _mod_spec(0, 0), _mod_spec(1, 0), vec],
        out_specs=row,
        out_shape=jax.ShapeDtypeStruct((B, SEQ, D), jnp.bfloat16),
        compiler_params=_cparams("parallel", "parallel"), name="s5_skip_gelu",
    )(x_lat, y_ssm, g.reshape(1, D), mod, mod, d_skip.reshape(1, D))


def _layer_modulation(c, c_ctx, w_mod, b_mod):
    cond = jnp.concatenate([c, c_ctx[None, :], jnp.zeros((MOD_ROWS - B - 1, D), jnp.float32)], axis=0)
    return _modulation(cond, w_mod, b_mod).reshape(MOD_ROWS * 6, 1, D)


def kernel(x, c, ctx, c_ctx, l0_w_mod, l0_b_mod, l0_g_pre_mix, l0_g_post_mix, l0_g_pre_ffn, l0_g_post_ffn, l0_w_in, l0_w_out, l0_lambda_q1, l0_lambda_k1, l0_lambda_q2, l0_lambda_k2, l0_g_subln, l0_g_qnorm, l0_g_knorm, l0_w_router, l0_b_router, l0_w_gate_up, l0_b_gate_up, l0_w_down, l0_b_down, l1_w_mod, l1_b_mod, l1_g_pre_mix, l1_g_post_mix, l1_g_pre_ffn, l1_g_post_ffn, l1_ssm_a_re, l1_ssm_a_im, l1_ssm_b_re, l1_ssm_b_im, l1_ssm_c_re, l1_ssm_c_im, l1_ssm_log_dt, l1_ssm_d, l1_w_glu, l1_w_router, l1_b_router, l1_w_gate_up, l1_b_gate_up, l1_w_down, l1_b_down):
    xs = jnp.concatenate([ctx, x], axis=1)

    mod = _layer_modulation(c, c_ctx, l0_w_mod, l0_b_mod)
    h = _norm_mod(xs, l0_g_pre_mix, mod, 0, 1, 1)
    cos2, sin2 = _rope_tables()
    p = _inproj(h.reshape(B * TOK, D), l0_w_in, cos2, sin2, l0_g_qnorm, l0_g_knorm)
    lam_params = jnp.stack([l0_lambda_q1, l0_lambda_k1, l0_lambda_q2, l0_lambda_k2])
    lambda_init = 0.8 - 0.6 * math.exp(-0.3 * 0)
    att = _attention(p.reshape(B, TOK, ATTN_IN), lam_params, l0_g_subln, lambda_init)
    y = _matmul(att.reshape(B * TOK, D), l0_w_out, 1024, 512, name="attn_outproj")
    xs = _post_norm_residual(xs, y.reshape(B, TOK, D), l0_g_post_mix, mod, 2, 1)
    hp, meta, cnt = _norm_mod(xs, l0_g_pre_ffn, mod, 3, 4, 1, router=(l0_w_router, l0_b_router))
    xs = _moe_ffn(xs.reshape(B * TOK, D), hp.reshape(B * TOK, HALF_D), meta.reshape(B * TOK, META_LANES), cnt,
                  l0_g_post_ffn, mod, 5, TOK, CTX,
                  l0_w_gate_up, l0_b_gate_up, l0_w_down, l0_b_down).reshape(B, TOK, D)

    mod = _layer_modulation(c, c_ctx, l1_w_mod, l1_b_mod)
    h = _norm_mod(xs, l1_g_pre_mix, mod, 0, 1, 1)
    u_ctx = _s5_chunks(h[:, :CTX])
    u_lat = _s5_chunks(h[:, CTX:])
    mats = [_s5_matrices(l1_ssm_a_re[d], l1_ssm_a_im[d], l1_ssm_b_re[d], l1_ssm_b_im[d],
                         l1_ssm_c_re[d], l1_ssm_c_im[d], l1_ssm_log_dt[d], reverse=bool(d))
            for d in range(2)]
    wb, m, wc, coef = (jnp.stack([mats[0][i], mats[1][i]]) for i in range(4))
    y_ssm = _s5_unchunk(_s5_scan(u_ctx, u_lat, wb, m, wc, coef))
    x_lat = xs[:, CTX:]
    z = _s5_post(x_lat, y_ssm, l1_g_pre_mix, mod, l1_ssm_d)
    y = _glu_matmul(z.reshape(B * SEQ, D), l1_w_glu, 1024, 512)
    x_lat = _post_norm_residual(x_lat, y.reshape(B, SEQ, D), l1_g_post_mix, mod, 2, 0)
    hp, meta, cnt = _norm_mod(x_lat, l1_g_pre_ffn, mod, 3, 4, 0, router=(l1_w_router, l1_b_router))
    return _moe_ffn(x_lat.reshape(B * SEQ, D), hp.reshape(B * SEQ, HALF_D), meta.reshape(B * SEQ, META_LANES), cnt,
                    l1_g_post_ffn, mod, 5, SEQ, 0,
                    l1_w_gate_up, l1_b_gate_up, l1_w_down, l1_b_down).reshape(B, SEQ, D)
```

```python
import functools
import math

import jax
import jax.numpy as jnp
import numpy as np
from jax import lax
from jax.experimental import pallas as pl
from jax.experimental.pallas import tpu as pltpu

D = 2048
B = 4
SEQ = 2048
CTX = 256
TOK = CTX + SEQ
GRID_W = 64
HD = 128
DIFF_HEADS = 4
GQA_Q_HEADS = 8
GQA_KV_HEADS = 2
GQA_GROUP = GQA_Q_HEADS // GQA_KV_HEADS
ROPE_THETA = 10000.0
ROPE_FREQS = HD // 4
ATTN_IN = 4608
N_EXPERTS = 32
TOP_K = 4
D_FF = D
SWIGLU_LIMIT = 7.0
SWIGLU_ALPHA = 1.702
RMS_EPS = 1e-6
SSM_GROUP = 16
SSM_STATE = 64
SSM_GROUPS = D // SSM_GROUP
SSM_CHUNK = 16

ROW_TILE = 256
MOD_ROWS = 8
MOD_CTX_ROW = B

V7X_VMEM_BYTES = 64 * 1024 * 1024
VMEM_LIMIT = 56 * 1024 * 1024


def _cparams(*sem):
    return pltpu.CompilerParams(dimension_semantics=sem, vmem_limit_bytes=VMEM_LIMIT)


def _rms(x, g):
    return x * lax.rsqrt(jnp.mean(x * x, axis=-1, keepdims=True) + RMS_EPS) * g


def _sigmoid(x):
    return 1.0 / (1.0 + jnp.exp(-x))


def _mod_kernel(c_ref, w_ref, b_ref, o_ref):
    c = c_ref[...]
    a = (c * _sigmoid(c)).astype(jnp.bfloat16)
    o_ref[...] = jnp.dot(a, w_ref[...].astype(jnp.bfloat16),
                         preferred_element_type=jnp.float32) + b_ref[...]


def _modulation(cond, w_mod, b_mod):
    tn = 1024
    n = w_mod.shape[1]
    return pl.pallas_call(
        _mod_kernel,
        grid=(n // tn,),
        in_specs=[pl.BlockSpec((MOD_ROWS, D), lambda j: (0, 0)),
                  pl.BlockSpec((D, tn), lambda j: (0, j)),
                  pl.BlockSpec((1, tn), lambda j: (0, j))],
        out_specs=pl.BlockSpec((MOD_ROWS, tn), lambda j: (0, j)),
        out_shape=jax.ShapeDtypeStruct((MOD_ROWS, n), jnp.float32),
        compiler_params=_cparams("arbitrary"),
        name="adaln_modulation",
    )(cond, w_mod, b_mod.reshape(1, n))


def _mod_spec(which, n_ctx_blocks):
    def idx(b, i):
        r = jnp.where(i < n_ctx_blocks, MOD_CTX_ROW, b)
        return (r * 6 + which, 0, 0)
    return pl.BlockSpec((None, 1, D), idx)


def _norm_mod_kernel(x_ref, g_ref, sh_ref, sc_ref, o_ref):
    h = _rms(x_ref[...], g_ref[...]) * (1.0 + sc_ref[...]) + sh_ref[...]
    o_ref[...] = h.astype(o_ref.dtype)


META_LANES = 128
META_E = 0
META_RANK = TOP_K
META_GATE = 2 * TOP_K
HALF_D = D // 2
HI_MASK = 0xFFFF0000


def _pack_bf16_pair(lo, hi):
    ulo = pltpu.bitcast(lo.astype(jnp.bfloat16).astype(jnp.float32), jnp.uint32)
    uhi = pltpu.bitcast(hi.astype(jnp.bfloat16).astype(jnp.float32), jnp.uint32)
    return lax.shift_right_logical(ulo, jnp.uint32(16)) | (uhi & jnp.uint32(HI_MASK))


def _unpack_bf16_pair(w):
    lo = pltpu.bitcast(lax.shift_left(w, jnp.uint32(16)), jnp.float32).astype(jnp.bfloat16)
    hi = pltpu.bitcast(w & jnp.uint32(HI_MASK), jnp.float32).astype(jnp.bfloat16)
    return lo, hi


def _norm_mod_router_kernel(x_ref, g_ref, sh_ref, sc_ref, wr_ref, br_ref, o_ref, meta_ref, cnt_ref, run_ref):
    first = jnp.logical_and(pl.program_id(0) == 0, pl.program_id(1) == 0)

    @pl.when(first)
    def _():
        run_ref[...] = jnp.zeros_like(run_ref)

    h = _rms(x_ref[...], g_ref[...]) * (1.0 + sc_ref[...]) + sh_ref[...]
    o_ref[...] = _pack_bf16_pair(h[:, :HALF_D], h[:, HALF_D:])
    logits = jnp.dot(h, wr_ref[...], preferred_element_type=jnp.float32,
                     precision=lax.Precision.HIGHEST) + br_ref[...]
    lane = lax.broadcasted_iota(jnp.int32, (ROW_TILE, N_EXPERTS), 1)
    vals, hots = [], []
    l = logits
    for _ in range(TOP_K):
        m = jnp.max(l, axis=-1, keepdims=True)
        idx = jnp.min(jnp.where(l == m, lane, N_EXPERTS), axis=-1, keepdims=True)
        hot = lane == idx
        vals.append(m)
        hots.append(hot)
        l = jnp.where(hot, -jnp.inf, l)
    es = [jnp.exp(v - vals[0]) for v in vals]
    den = es[0] + es[1] + es[2] + es[3]
    onehot = sum(hot.astype(jnp.float32) for hot in hots)
    r_i = lax.broadcasted_iota(jnp.int32, (ROW_TILE, ROW_TILE), 0)
    c_i = lax.broadcasted_iota(jnp.int32, (ROW_TILE, ROW_TILE), 1)
    lower = jnp.where(r_i > c_i, 1.0, 0.0).astype(jnp.bfloat16)
    before = jnp.dot(lower, onehot.astype(jnp.bfloat16), preferred_element_type=jnp.float32) + run_ref[0:1, 0:N_EXPERTS]
    mlane = lax.broadcasted_iota(jnp.int32, (ROW_TILE, META_LANES), 1)
    lane_f = lane.astype(jnp.float32)
    meta = jnp.zeros((ROW_TILE, META_LANES), jnp.float32)
    for k in range(TOP_K):
        hot_f = hots[k].astype(jnp.float32)
        e_k = jnp.sum(hot_f * lane_f, axis=-1, keepdims=True)
        rank_k = jnp.sum(hot_f * before, axis=-1, keepdims=True)
        meta = jnp.where(mlane == META_E + k, e_k, meta)
        meta = jnp.where(mlane == META_RANK + k, rank_k, meta)
        meta = jnp.where(mlane == META_GATE + k, es[k] / den, meta)
    meta_ref[...] = meta
    total = run_ref[0:1, 0:N_EXPERTS] + jnp.sum(onehot, axis=0, keepdims=True)
    run_ref[0:1, 0:N_EXPERTS] = total
    cnt_ref[...] = jnp.broadcast_to(run_ref[0:1, :], cnt_ref.shape)


def _norm_mod(x, g, mod, shift_idx, scale_idx, n_ctx_blocks, router=None):
    t = x.shape[1]
    grid = (B, t // ROW_TILE)
    row = pl.BlockSpec((None, ROW_TILE, D), lambda b, i: (b, i, 0))
    in_specs = [row, pl.BlockSpec((1, D), lambda b, i: (0, 0)),
                _mod_spec(shift_idx, n_ctx_blocks), _mod_spec(scale_idx, n_ctx_blocks)]
    args = [x, g.reshape(1, D), mod, mod]
    if router is None:
        return pl.pallas_call(
            _norm_mod_kernel, grid=grid, in_specs=in_specs, out_specs=row,
            out_shape=jax.ShapeDtypeStruct((B, t, D), jnp.bfloat16),
            compiler_params=_cparams("parallel", "parallel"), name="norm_mod",
        )(*args)
    w_router, b_router = router
    in_specs += [pl.BlockSpec((D, N_EXPERTS), lambda b, i: (0, 0)),
                 pl.BlockSpec((1, N_EXPERTS), lambda b, i: (0, 0))]
    args += [w_router, b_router.reshape(1, N_EXPERTS)]
    return pl.pallas_call(
        _norm_mod_router_kernel, grid=grid, in_specs=in_specs,
        out_specs=[pl.BlockSpec((None, ROW_TILE, HALF_D), lambda b, i: (b, i, 0)),
                   pl.BlockSpec((None, ROW_TILE, META_LANES), lambda b, i: (b, i, 0)),
                   pl.BlockSpec((8, META_LANES), lambda b, i: (0, 0))],
        out_shape=[jax.ShapeDtypeStruct((B, t, HALF_D), jnp.uint32),
                   jax.ShapeDtypeStruct((B, t, META_LANES), jnp.float32),
                   jax.ShapeDtypeStruct((8, META_LANES), jnp.float32)],
        scratch_shapes=[pltpu.VMEM((8, META_LANES), jnp.float32)],
        compiler_params=_cparams("arbitrary", "arbitrary"), name="norm_mod_router",
    )(*args)


def _post_norm_kernel(x_ref, y_ref, g_ref, gt_ref, o_ref):
    o_ref[...] = x_ref[...] + gt_ref[...] * _rms(y_ref[...], g_ref[...])


def _post_norm_residual(x, y, g, mod, gate_idx, n_ctx_blocks):
    t = x.shape[1]
    row = pl.BlockSpec((None, ROW_TILE, D), lambda b, i: (b, i, 0))
    return pl.pallas_call(
        _post_norm_kernel, grid=(B, t // ROW_TILE),
        in_specs=[row, row, pl.BlockSpec((1, D), lambda b, i: (0, 0)),
                  _mod_spec(gate_idx, n_ctx_blocks)],
        out_specs=row,
        out_shape=jax.ShapeDtypeStruct((B, t, D), jnp.float32),
        input_output_aliases={0: 0},
        compiler_params=_cparams("parallel", "parallel"), name="post_norm_residual",
    )(x, y, g.reshape(1, D), mod)


def _matmul_kernel(a_ref, w_ref, o_ref):
    o_ref[...] = jnp.dot(a_ref[...], w_ref[...].astype(jnp.bfloat16),
                         preferred_element_type=jnp.float32).astype(o_ref.dtype)


def _matmul(a, w, tm, tn, out_dtype=jnp.float32, name="matmul"):
    m, k = a.shape
    n = w.shape[1]
    return pl.pallas_call(
        _matmul_kernel, grid=(m // tm, n // tn),
        in_specs=[pl.BlockSpec((tm, k), lambda i, j: (i, 0)),
                  pl.BlockSpec((k, tn), lambda i, j: (0, j))],
        out_specs=pl.BlockSpec((tm, tn), lambda i, j: (i, j)),
        out_shape=jax.ShapeDtypeStruct((m, n), out_dtype),
        compiler_params=_cparams("parallel", "arbitrary"), name=name,
    )(a, w)


def _glu_matmul_kernel(a_ref, wv_ref, wg_ref, o_ref):
    a = a_ref[...]
    val = jnp.dot(a, wv_ref[...].astype(jnp.bfloat16), preferred_element_type=jnp.float32)
    gate = jnp.dot(a, wg_ref[...].astype(jnp.bfloat16), preferred_element_type=jnp.float32)
    o_ref[...] = val * _sigmoid(gate)


def _glu_matmul(a, w_glu, tm, tn):
    m, k = a.shape
    n = w_glu.shape[1] // 2
    nj = n // tn
    return pl.pallas_call(
        _glu_matmul_kernel, grid=(m // tm, nj),
        in_specs=[pl.BlockSpec((tm, k), lambda i, j: (i, 0)),
                  pl.BlockSpec((k, tn), lambda i, j: (0, j)),
                  pl.BlockSpec((k, tn), lambda i, j: (0, nj + j))],
        out_specs=pl.BlockSpec((tm, tn), lambda i, j: (i, j)),
        out_shape=jax.ShapeDtypeStruct((m, n), jnp.float32),
        compiler_params=_cparams("parallel", "arbitrary"), name="glu_matmul",
    )(a, w_glu, w_glu)


IN_TN = 256
IN_ROPE_END = 8
IN_DV_END = 12
IN_GQ_END = 16
IN_GK_TILE = 16


def _inproj_kernel(a_ref, w_ref, cos_ref, sin_ref, gq_ref, gk_ref, o_ref):
    j = pl.program_id(1)
    acc = jnp.dot(a_ref[...], w_ref[...].astype(jnp.bfloat16), preferred_element_type=jnp.float32)

    def rope(x):
        return x * cos_ref[...] + pltpu.roll(x, HD // 2, 1) * sin_ref[...]

    def store(fn):
        for c in range(IN_TN // HD):
            o_ref[:, c * HD:(c + 1) * HD] = fn(acc[:, c * HD:(c + 1) * HD]).astype(o_ref.dtype)

    @pl.when(j < IN_ROPE_END)
    def _():
        store(rope)

    @pl.when(jnp.logical_or(jnp.logical_and(j >= IN_ROPE_END, j < IN_DV_END), j > IN_GK_TILE))
    def _():
        store(lambda x: x)

    @pl.when(jnp.logical_and(j >= IN_DV_END, j < IN_GQ_END))
    def _():
        store(lambda x: rope(_rms(x, gq_ref[...])))

    @pl.when(j == IN_GK_TILE)
    def _():
        store(lambda x: rope(_rms(x, gk_ref[...])))


def _inproj(h, w_in, cos2, sin2, g_q, g_k):
    m = h.shape[0]
    const = lambda i, j: (0, 0)
    return pl.pallas_call(
        _inproj_kernel, grid=(m // TOK, ATTN_IN // IN_TN),
        in_specs=[pl.BlockSpec((TOK, D), lambda i, j: (i, 0)),
                  pl.BlockSpec((D, IN_TN), lambda i, j: (0, j)),
                  pl.BlockSpec((TOK, HD), const), pl.BlockSpec((TOK, HD), const),
                  pl.BlockSpec((1, HD), const), pl.BlockSpec((1, HD), const)],
        out_specs=pl.BlockSpec((TOK, IN_TN), lambda i, j: (i, j)),
        out_shape=jax.ShapeDtypeStruct((m, ATTN_IN), jnp.bfloat16),
        compiler_params=_cparams("parallel", "arbitrary"), name="attn_inproj",
    )(h, w_in, cos2, sin2, g_q.reshape(1, HD), g_k.reshape(1, HD))


ATT_TQ = 256
ATT_SCALE = HD ** -0.5


def _softmax_pv(q, k, v):
    s = lax.dot_general(q, k, (((1,), (1,)), ((), ())), preferred_element_type=jnp.float32)
    m = jnp.max(s, axis=-1, keepdims=True)
    e = jnp.exp((s - m) * ATT_SCALE)
    l = jnp.sum(e, axis=-1, keepdims=True)
    return jnp.dot(e.astype(jnp.bfloat16), v, preferred_element_type=jnp.float32), l


def _diff_attn_kernel(lam_ref, q_ref, k_ref, v_ref, g_ref, o_ref, *, lambda_init):
    lp = lam_ref[...]
    lam = (jnp.exp(jnp.sum(lp[0:1] * lp[1:2], axis=-1, keepdims=True))
           - jnp.exp(jnp.sum(lp[2:3] * lp[3:4], axis=-1, keepdims=True)) + lambda_init)

    def run(nk):
        q = q_ref[...]
        k = k_ref[0:nk, :]
        v = v_ref[0:nk, :]
        pv1, l1 = _softmax_pv(q[:, :HD], k[:, :HD], v)
        pv2, l2 = _softmax_pv(q[:, HD:], k[:, HD:], v)
        o = pv1 / l1 - lam * (pv2 / l2)
        o_ref[...] = (_rms(o, g_ref[...]) * (1.0 - lambda_init)).astype(o_ref.dtype)

    @pl.when(pl.program_id(2) == 0)
    def _():
        run(CTX)

    @pl.when(pl.program_id(2) > 0)
    def _():
        run(TOK)


def _gqa_attn_kernel(q_ref, k_ref, v_ref, o_ref):
    def run(nk):
        k = k_ref[0:nk, :]
        v = v_ref[0:nk, :]
        for g in range(GQA_GROUP):
            pv, l = _softmax_pv(q_ref[:, g * HD:(g + 1) * HD], k, v)
            o_ref[:, g * HD:(g + 1) * HD] = (pv / l).astype(o_ref.dtype)

    @pl.when(pl.program_id(2) == 0)
    def _():
        run(CTX)

    @pl.when(pl.program_id(2) > 0)
    def _():
        run(TOK)


def _attention(p, lam_params, g_subln, lambda_init):
    nq = TOK // ATT_TQ
    dv = 2 * HD
    od = pl.pallas_call(
        functools.partial(_diff_attn_kernel, lambda_init=lambda_init),
        grid=(B, DIFF_HEADS, nq),
        in_specs=[pl.BlockSpec((4, HD), lambda b, h, i: (0, 0)),
                  pl.BlockSpec((None, ATT_TQ, dv), lambda b, h, i: (b, i, h)),
                  pl.BlockSpec((None, TOK, dv), lambda b, h, i: (b, 0, DIFF_HEADS + h)),
                  pl.BlockSpec((None, TOK, dv), lambda b, h, i: (b, 0, 2 * DIFF_HEADS + h)),
                  pl.BlockSpec((1, dv), lambda b, h, i: (0, 0))],
        out_specs=pl.BlockSpec((None, ATT_TQ, dv), lambda b, h, i: (b, i, h)),
        out_shape=jax.ShapeDtypeStruct((B, TOK, DIFF_HEADS * dv), jnp.bfloat16),
        compiler_params=_cparams("parallel", "parallel", "arbitrary"), name="diff_attention",
    )(lam_params, p, p, p, g_subln.reshape(1, dv))
    gq_w = GQA_GROUP * HD
    gq0 = 3072 // gq_w
    gk0 = 4096 // HD
    gv0 = 4352 // HD
    og = pl.pallas_call(
        _gqa_attn_kernel,
        grid=(B, GQA_KV_HEADS, nq),
        in_specs=[pl.BlockSpec((None, ATT_TQ, gq_w), lambda b, n, i: (b, i, gq0 + n)),
                  pl.BlockSpec((None, TOK, HD), lambda b, n, i: (b, 0, gk0 + n)),
                  pl.BlockSpec((None, TOK, HD), lambda b, n, i: (b, 0, gv0 + n))],
        out_specs=pl.BlockSpec((None, ATT_TQ, gq_w), lambda b, n, i: (b, i, n)),
        out_shape=jax.ShapeDtypeStruct((B, TOK, GQA_Q_HEADS * HD), jnp.bfloat16),
        compiler_params=_cparams("parallel", "parallel", "arbitrary"), name="gqa_attention",
    )(p, p, p)
    return jnp.concatenate([od, og], axis=-1)


def _rope_tables():
    rows = SEQ // GRID_W
    row_id, col_id = jnp.meshgrid(jnp.arange(rows), jnp.arange(GRID_W), indexing="ij")
    inv_freq = ROPE_THETA ** (-jnp.arange(ROPE_FREQS, dtype=jnp.float32) / ROPE_FREQS)
    ang = jnp.concatenate([row_id.reshape(-1, 1) * inv_freq, col_id.reshape(-1, 1) * inv_freq], axis=-1)
    cos, sin = jnp.cos(ang), jnp.sin(ang)
    cos2 = jnp.concatenate([cos, cos], axis=-1)
    sin2 = jnp.concatenate([-sin, sin], axis=-1)
    cos2 = jnp.concatenate([jnp.ones((CTX, HD), jnp.float32), cos2], axis=0)
    sin2 = jnp.concatenate([jnp.zeros((CTX, HD), jnp.float32), sin2], axis=0)
    return cos2, sin2


MOE_TM = 512
MOE_SUB = 256
MOE_TF = 1024
DISPATCH_TT = 256
COMBINE_TT = 64


def _for_valid_rows(rows, compute, o_ref):
    half = MOE_SUB // 2
    width = o_ref.shape[1]
    for s in range(MOE_TM // MOE_SUB):
        base = s * MOE_SUB

        @pl.when(rows > base + half)
        def _():
            compute(slice(base, base + MOE_SUB))

        @pl.when(jnp.logical_and(rows > base, rows <= base + half))
        def _():
            compute(slice(base, base + half))
            o_ref[base + half:base + MOE_SUB, :] = jnp.zeros((half, width), o_ref.dtype)

        @pl.when(rows <= base)
        def _():
            o_ref[base:base + MOE_SUB, :] = jnp.zeros((MOE_SUB, width), o_ref.dtype)


def _moe_up_kernel(be_ref, first_ref, rows_ref, nxt_ref, x_ref, w_hbm, bg_ref, bl_ref, o_ref,
                   wst, wbf, sem):
    f = pl.program_id(0)
    b = pl.program_id(1)
    nf = pl.num_programs(0)

    def copies(e, ff):
        col = pl.multiple_of(ff * MOE_TF, MOE_TF)
        return [pltpu.make_async_copy(w_hbm.at[e, :, pl.ds(part * D_FF + col, MOE_TF)],
                                      wst.at[part], sem.at[part]) for part in range(2)]

    @pl.when(first_ref[b] == 1)
    def _():
        @pl.when(jnp.logical_and(f == 0, b == 0))
        def _():
            for c in copies(be_ref[0], 0):
                c.start()

        for c in copies(be_ref[b], f):
            c.wait()
        wbf[...] = wst[...].astype(jnp.bfloat16)
        e_next = nxt_ref[b]

        @pl.when(e_next >= 0)
        def _():
            for c in copies(e_next, f):
                c.start()

        @pl.when(jnp.logical_and(e_next < 0, f + 1 < nf))
        def _():
            for c in copies(be_ref[0], f + 1):
                c.start()

    def compute(sl):
        lo, hi = _unpack_bf16_pair(x_ref[sl, :])
        glu = (jnp.dot(lo, wbf[0, :HALF_D, :], preferred_element_type=jnp.float32)
               + jnp.dot(hi, wbf[0, HALF_D:, :], preferred_element_type=jnp.float32) + bg_ref[...])
        lin = (jnp.dot(lo, wbf[1, :HALF_D, :], preferred_element_type=jnp.float32)
               + jnp.dot(hi, wbf[1, HALF_D:, :], preferred_element_type=jnp.float32) + bl_ref[...])
        glu = jnp.minimum(glu, SWIGLU_LIMIT)
        lin = jnp.clip(lin, -SWIGLU_LIMIT, SWIGLU_LIMIT)
        o_ref[sl, :] = (glu * _sigmoid(SWIGLU_ALPHA * glu) * (lin + 1.0)).astype(o_ref.dtype)

    _for_valid_rows(rows_ref[b], compute, o_ref)


def _moe_down_kernel(be_ref, first_ref, rows_ref, nxt_ref, a_ref, w_hbm, bias_ref, o_ref, wst, wbf, sem):
    b = pl.program_id(0)

    def copy(e):
        return pltpu.make_async_copy(w_hbm.at[e], wst, sem)

    @pl.when(first_ref[b] == 1)
    def _():
        @pl.when(b == 0)
        def _():
            copy(be_ref[0]).start()

        copy(be_ref[b]).wait()
        wbf[...] = wst[...].astype(jnp.bfloat16)
        e_next = nxt_ref[b]

        @pl.when(e_next >= 0)
        def _():
            copy(e_next).start()

    def compute(sl):
        o_ref[sl, :] = jnp.dot(a_ref[sl, :], wbf[...], preferred_element_type=jnp.float32) + bias_ref[...]

    _for_valid_rows(rows_ref[b], compute, o_ref)


def _moe_experts(x_sorted, tables, w_gate_up, b_gate_up, w_down, b_down):
    r = x_sorted.shape[0]
    nb = r // MOE_TM
    nf = D_FF // MOE_TF
    bgu = b_gate_up.reshape(N_EXPERTS, 1, 2 * D_FF)
    act = pl.pallas_call(
        _moe_up_kernel,
        grid_spec=pltpu.PrefetchScalarGridSpec(
            num_scalar_prefetch=4, grid=(nf, nb),
            in_specs=[pl.BlockSpec((MOE_TM, HALF_D), lambda f, b, be, fi, ro, nx: (b, 0)),
                      pl.BlockSpec(memory_space=pl.ANY),
                      pl.BlockSpec((None, 1, MOE_TF), lambda f, b, be, fi, ro, nx: (be[b], 0, f)),
                      pl.BlockSpec((None, 1, MOE_TF), lambda f, b, be, fi, ro, nx: (be[b], 0, nf + f))],
            out_specs=pl.BlockSpec((MOE_TM, MOE_TF), lambda f, b, be, fi, ro, nx: (b, f)),
            scratch_shapes=[pltpu.VMEM((2, D, MOE_TF), jnp.float32), pltpu.VMEM((2, D, MOE_TF), jnp.bfloat16),
                            pltpu.SemaphoreType.DMA((2,))]),
        out_shape=jax.ShapeDtypeStruct((r, D_FF), jnp.bfloat16),
        compiler_params=_cparams("arbitrary", "arbitrary"), name="moe_gate_up",
    )(*tables, x_sorted, w_gate_up, bgu, bgu)
    return pl.pallas_call(
        _moe_down_kernel,
        grid_spec=pltpu.PrefetchScalarGridSpec(
            num_scalar_prefetch=4, grid=(nb,),
            in_specs=[pl.BlockSpec((MOE_TM, D_FF), lambda b, be, fi, ro, nx: (b, 0)),
                      pl.BlockSpec(memory_space=pl.ANY),
                      pl.BlockSpec((None, 1, D), lambda b, be, fi, ro, nx: (be[b], 0, 0))],
            out_specs=pl.BlockSpec((MOE_TM, D), lambda b, be, fi, ro, nx: (b, 0)),
            scratch_shapes=[pltpu.VMEM((D_FF, D), jnp.float32), pltpu.VMEM((D_FF, D), jnp.bfloat16),
                            pltpu.SemaphoreType.DMA]),
        out_shape=jax.ShapeDtypeStruct((r, D), jnp.float32),
        compiler_params=_cparams("arbitrary"), name="moe_down",
    )(*tables, act, w_down, b_down.reshape(N_EXPERTS, 1, D))


def _dispatch_kernel(dest_ref, hp_ref, xs_in_hbm, xs_hbm, idx_smem, sem_idx, sem_rows):
    del xs_in_hbm
    idx_copy = pltpu.make_async_copy(dest_ref, idx_smem, sem_idx)
    idx_copy.start()
    idx_copy.wait()

    def issue(t, carry):
        for k in range(TOP_K):
            d = idx_smem[0, 0, t * TOP_K + k]
            pltpu.make_async_copy(hp_ref.at[t], xs_hbm.at[d], sem_rows).start(priority=k % 2)
        return carry

    lax.fori_loop(0, DISPATCH_TT, issue, 0, unroll=4)
    for k in range(TOP_K):
        pltpu.make_async_copy(hp_ref, xs_hbm.at[pl.ds(0, DISPATCH_TT)], sem_rows).wait()


def _dispatch(dest, hp, n_rows):
    n_tok = hp.shape[0]
    n = n_tok // DISPATCH_TT
    width = DISPATCH_TT * TOP_K
    zeros = jnp.zeros((n_rows, HALF_D), jnp.uint32)
    return pl.pallas_call(
        _dispatch_kernel, grid=(n,),
        in_specs=[pl.BlockSpec((1, 1, width), lambda i: (i, 0, 0)),
                  pl.BlockSpec((DISPATCH_TT, HALF_D), lambda i: (i, 0)), pl.BlockSpec(memory_space=pl.ANY)],
        out_specs=pl.BlockSpec(memory_space=pl.ANY),
        out_shape=jax.ShapeDtypeStruct((n_rows, HALF_D), jnp.uint32),
        scratch_shapes=[pltpu.SMEM((1, 1, width), jnp.int32), pltpu.SemaphoreType.DMA, pltpu.SemaphoreType.DMA],
        input_output_aliases={2: 0},
        compiler_params=_cparams("arbitrary"), name="moe_dispatch",
    )(dest.reshape(n, 1, width), hp, zeros)


def _combine_kernel(dcur_ref, dnext_ref, out_hbm, meta_ref, x_ref, g_ref, gt_ref, o_ref,
                    idx_smem, buf, sem_idx, sem_rows):
    i = pl.program_id(0)
    n = pl.num_programs(0)

    def row_copy(d, slot, k, t):
        return pltpu.make_async_copy(out_hbm.at[d], buf.at[slot, k, t], sem_rows.at[slot])

    def gather(d_ref, slot):
        idx_copy = pltpu.make_async_copy(d_ref, idx_smem, sem_idx)
        idx_copy.start()
        idx_copy.wait()

        def issue(t, carry):
            for k in range(TOP_K):
                row_copy(idx_smem[0, 0, t * TOP_K + k], slot, k, t).start(priority=k % 2)
            return carry
        lax.fori_loop(0, COMBINE_TT, issue, 0, unroll=4)

    @pl.when(i == 0)
    def _():
        gather(dcur_ref, 0)

    @pl.when(i + 1 < n)
    def _():
        gather(dnext_ref, (i + 1) % 2)

    slot = i % 2
    pltpu.make_async_copy(buf.at[slot], buf.at[slot], sem_rows.at[slot]).wait()

    meta = meta_ref[...]
    f = meta[:, META_GATE:META_GATE + 1] * buf[slot, 0]
    for k in range(1, TOP_K):
        f = f + meta[:, META_GATE + k:META_GATE + k + 1] * buf[slot, k]
    o_ref[...] = x_ref[...] + gt_ref[...] * _rms(f, g_ref[...])


def _combine(dest, out_sorted, meta, x, g, mod, gate_idx, rows_per_batch, ctx_rows):
    n_tok = x.shape[0]
    n = n_tok // COMBINE_TT
    width = COMBINE_TT * TOP_K
    per_batch = rows_per_batch // COMBINE_TT
    ctx_blocks = ctx_rows // COMBINE_TT

    def gate_row(i):
        r = jnp.where(i % per_batch < ctx_blocks, MOD_CTX_ROW, i // per_batch)
        return (r * 6 + gate_idx, 0, 0)

    row = pl.BlockSpec((COMBINE_TT, D), lambda i: (i, 0))
    return pl.pallas_call(
        _combine_kernel, grid=(n,),
        in_specs=[pl.BlockSpec((1, 1, width), lambda i: (i, 0, 0)),
                  pl.BlockSpec((1, 1, width), lambda i: (jnp.minimum(i + 1, n - 1), 0, 0)),
                  pl.BlockSpec(memory_space=pl.ANY),
                  pl.BlockSpec((COMBINE_TT, META_LANES), lambda i: (i, 0)),
                  row, pl.BlockSpec((1, D), lambda i: (0, 0)),
                  pl.BlockSpec((None, 1, D), gate_row)],
        out_specs=row,
        out_shape=jax.ShapeDtypeStruct((n_tok, D), jnp.float32),
        scratch_shapes=[pltpu.SMEM((1, 1, width), jnp.int32),
                        pltpu.VMEM((2, TOP_K, COMBINE_TT, D), jnp.float32),
                        pltpu.SemaphoreType.DMA, pltpu.SemaphoreType.DMA((2,))],
        input_output_aliases={4: 0},
        compiler_params=_cparams("arbitrary"), name="moe_combine",
    )(dest.reshape(n, 1, width), dest.reshape(n, 1, width), out_sorted, meta, x, g.reshape(1, D), mod)


def _moe_block_tables(counts, n_assign):
    padded = (counts + MOE_TM - 1) // MOE_TM * MOE_TM
    pad_end = jnp.cumsum(padded)
    pad_start = pad_end - padded
    nb = -(-(n_assign + N_EXPERTS * (MOE_TM - 1)) // MOE_TM)
    block_start = jnp.arange(nb, dtype=jnp.int32) * MOE_TM
    block_e_raw = jnp.minimum(jnp.sum(block_start[:, None] >= pad_end[None, :], axis=1), N_EXPERTS - 1)
    block_e_raw = block_e_raw.astype(jnp.int32)
    onehot_e = block_e_raw[:, None] == jnp.arange(N_EXPERTS)[None, :]
    valid_end = jnp.sum(jnp.where(onehot_e, (pad_start + counts)[None, :], 0), axis=1)
    block_rows = jnp.clip(valid_end - block_start, 0, MOE_TM).astype(jnp.int32)
    block_rows = jnp.where(block_start < pad_end[-1], block_rows, 0)
    last_e = jnp.max(jnp.where(block_rows > 0, block_e_raw, 0))
    block_e = jnp.where(block_rows > 0, block_e_raw, last_e)
    block_first = jnp.concatenate([jnp.ones((1,), jnp.int32),
                                   (block_e[1:] != block_e[:-1]).astype(jnp.int32)])
    idx = jnp.arange(nb, dtype=jnp.int32)
    first_pos = jnp.where(block_first == 1, idx, nb)
    later = jnp.where(idx[None, :] > idx[:, None], first_pos[None, :], nb)
    next_pos = jnp.min(later, axis=1)
    next_e = jnp.sum(jnp.where(idx[None, :] == next_pos[:, None], block_e[None, :], 0), axis=1)
    block_next = jnp.where(next_pos < nb, next_e, -1).astype(jnp.int32)
    return pad_start, nb * MOE_TM, (block_e, block_first, block_rows, block_next)


def _moe_ffn(x_res, hp, meta, cnt, g_post, mod, gate_idx, rows_per_batch, ctx_rows,
             w_gate_up, b_gate_up, w_down, b_down):
    n_tok = hp.shape[0]
    counts = cnt[0, :N_EXPERTS].astype(jnp.int32)
    pad_start, n_rows, tables = _moe_block_tables(counts, n_tok * TOP_K)
    top_e = meta[:, META_E:META_E + TOP_K].astype(jnp.int32)
    rank = meta[:, META_RANK:META_RANK + TOP_K].astype(jnp.int32)
    hot = top_e[:, :, None] == jnp.arange(N_EXPERTS)[None, None, :]
    dest = jnp.sum(jnp.where(hot, pad_start[None, None, :], 0), axis=-1) + rank
    x_sorted = _dispatch(dest, hp, n_rows)
    out_sorted = _moe_experts(x_sorted, tables, w_gate_up, b_gate_up, w_down, b_down)
    return _combine(dest, out_sorted, meta, x_res, g_post, mod, gate_idx, rows_per_batch, ctx_rows)


S5_GB = 8
S5_LANES = SSM_CHUNK * SSM_GROUP
S5_NK_CTX = CTX // SSM_CHUNK
S5_NK_LAT = SEQ // SSM_CHUNK
S5_PAIR = 2 * B


def _s5_direction(reverse, uc_ref, ul_ref, wb_ref, m_ref, wc_ref, coef_ref, y_ref,
                  s1c, s2c, s1l, s2l, xin, accumulate):
    half = 2 * SSM_STATE
    for g in range(S5_GB):
        sc = jnp.dot(uc_ref[g], wb_ref[g], preferred_element_type=jnp.float32)
        s1c[g] = sc[:, :half]
        s2c[g] = sc[:, half:]
        sl = jnp.dot(ul_ref[g], wb_ref[g], preferred_element_type=jnp.float32)
        s1l[g] = sl[:, :half]
        s2l[g] = sl[:, half:]

    lower = lax.broadcasted_iota(jnp.int32, (S5_PAIR, half), 0) < B
    p1 = [jnp.broadcast_to(coef_ref[g, 0:1, :], (S5_PAIR, half)) for g in range(S5_GB)]
    p2 = [jnp.broadcast_to(coef_ref[g, 1:2, :], (S5_PAIR, half)) for g in range(S5_GB)]

    def tile_step(g, t1, t2, v1, v2):
        y1a = p1[g] * v1 + p2[g] * v2 + t1
        y1b = p1[g] * v2 - p2[g] * v1 + t2
        r1a = pltpu.roll(y1a, B, 0)
        r1b = pltpu.roll(y1b, B, 0)
        y2a = p1[g] * r1a + p2[g] * r1b + t1
        y2b = p1[g] * r1b - p2[g] * r1a + t2
        r2a = pltpu.roll(y2a, B, 0)
        r2b = pltpu.roll(y2b, B, 0)
        if not reverse:
            x_in = jnp.where(lower, v1, r1a)
            return x_in, jnp.where(lower, r2a, y2a), jnp.where(lower, r2b, y2b)
        x_in = jnp.where(lower, r1a, v1)
        return x_in, jnp.where(lower, y2a, r2a), jnp.where(lower, y2b, r2b)

    def scan(s1, s2, n_tiles, state, record):
        def body(j, carry):
            jj = (n_tiles - 1 - j) if reverse else j
            r0 = pl.multiple_of(jj * S5_PAIR, S5_PAIR)
            new = []
            for g in range(S5_GB):
                v1, v2 = carry[2 * g], carry[2 * g + 1]
                x_in, v1, v2 = tile_step(g, s1[g, pl.ds(r0, S5_PAIR), :], s2[g, pl.ds(r0, S5_PAIR), :], v1, v2)
                if record:
                    xin[g, pl.ds(r0, S5_PAIR), :] = x_in
                new += [v1, v2]
            return tuple(new)
        return lax.fori_loop(0, n_tiles, body, state)

    zero = jnp.zeros((S5_PAIR, half), jnp.float32)
    state = tuple(zero for _ in range(2 * S5_GB))
    state = scan(s1c, s2c, S5_NK_CTX * B // S5_PAIR, state, False)
    scan(s1l, s2l, S5_NK_LAT * B // S5_PAIR, state, True)

    for g in range(S5_GB):
        y = (jnp.dot(ul_ref[g], m_ref[g], preferred_element_type=jnp.float32)
             + jnp.dot(xin[g].astype(jnp.bfloat16), wc_ref[g], preferred_element_type=jnp.float32))
        if accumulate:
            y_ref[g] = y_ref[g] + y
        else:
            y_ref[g] = y


def _s5_kernel(uc_ref, ul_ref, wb_ref, m_ref, wc_ref, coef_ref, y_ref, s1c, s2c, s1l, s2l, xin):
    args = (uc_ref, ul_ref, wb_ref, m_ref, wc_ref, coef_ref, y_ref, s1c, s2c, s1l, s2l, xin)

    @pl.when(pl.program_id(1) == 0)
    def _():
        _s5_direction(False, *args, accumulate=False)

    @pl.when(pl.program_id(1) == 1)
    def _():
        _s5_direction(True, *args, accumulate=True)


def _s5_scan(u_ctx, u_lat, wb, m, wc, coef):
    rc, rl = u_ctx.shape[1], u_lat.shape[1]
    half = 2 * SSM_STATE
    wspec = lambda k, n: pl.BlockSpec((None, S5_GB, k, n), lambda gi, d: (d, gi, 0, 0))
    return pl.pallas_call(
        _s5_kernel, grid=(SSM_GROUPS // S5_GB, 2),
        in_specs=[pl.BlockSpec((S5_GB, rc, S5_LANES), lambda gi, d: (gi, 0, 0)),
                  pl.BlockSpec((S5_GB, rl, S5_LANES), lambda gi, d: (gi, 0, 0)),
                  wspec(S5_LANES, 2 * half), wspec(S5_LANES, S5_LANES), wspec(half, S5_LANES),
                  wspec(2, half)],
        out_specs=pl.BlockSpec((S5_GB, rl, S5_LANES), lambda gi, d: (gi, 0, 0)),
        out_shape=jax.ShapeDtypeStruct((SSM_GROUPS, rl, S5_LANES), jnp.float32),
        scratch_shapes=[pltpu.VMEM((S5_GB, rc, half), jnp.float32), pltpu.VMEM((S5_GB, rc, half), jnp.float32),
                        pltpu.VMEM((S5_GB, rl, half), jnp.float32), pltpu.VMEM((S5_GB, rl, half), jnp.float32),
                        pltpu.VMEM((S5_GB, rl, half), jnp.float32)],
        compiler_params=_cparams("parallel", "arbitrary"), name="s5_scan",
    )(u_ctx, u_lat, wb, m, wc, coef)


def _s5_matrices(a_re, a_im, b_re, b_im, c_re, c_im, log_dt, reverse):
    hp = lax.Precision.HIGHEST
    n = SSM_CHUNK
    g, p = SSM_GROUPS, SSM_STATE
    dt = jnp.exp(log_dt)[:, None]
    mag = jnp.exp(a_re * dt)
    ab_re, ab_im = mag * jnp.cos(a_im * dt), mag * jnp.sin(a_im * dt)
    den = a_re * a_re + a_im * a_im
    f_re = ((ab_re - 1.0) * a_re + ab_im * a_im) / den
    f_im = (ab_im * a_re - (ab_re - 1.0) * a_im) / den
    b_re_t, b_im_t = b_re.transpose(0, 2, 1), b_im.transpose(0, 2, 1)
    bb_re = f_re[:, None, :] * b_re_t - f_im[:, None, :] * b_im_t
    bb_im = f_re[:, None, :] * b_im_t + f_im[:, None, :] * b_re_t
    tau = jnp.arange(n + 1, dtype=jnp.float32)[None, None, :]
    ang_re, ang_im = (a_re * dt)[:, :, None], (a_im * dt)[:, :, None]
    pmag = jnp.exp(tau * ang_re)
    pw_re, pw_im = pmag * jnp.cos(tau * ang_im), pmag * jnp.sin(tau * ang_im)
    c_re_t, c_im_t = c_re.transpose(0, 2, 1), c_im.transpose(0, 2, 1)

    def c_times_power(exps):
        q_re, q_im = pw_re[:, :, exps], pw_im[:, :, exps]
        re = c_re_t[:, :, None, :] * q_re[:, :, :, None] - c_im_t[:, :, None, :] * q_im[:, :, :, None]
        im = c_re_t[:, :, None, :] * q_im[:, :, :, None] + c_im_t[:, :, None, :] * q_re[:, :, :, None]
        return re.reshape(g, p, -1), im.reshape(g, p, -1)

    lags = np.arange(n)[::-1] if reverse else np.arange(n)
    ca_re, ca_im = c_times_power(lags)
    kst = jnp.matmul(jnp.concatenate([bb_re, bb_im], axis=-1), jnp.concatenate([ca_re, -ca_im], axis=1),
                     precision=hp)
    zeros = jnp.zeros_like(kst)
    if reverse:
        z = jnp.concatenate([kst, zeros], axis=-1)
        m = jnp.stack([z[:, :, (n - 1 - s) * SSM_GROUP:(n - 1 - s) * SSM_GROUP + S5_LANES] for s in range(n)], axis=1)
    else:
        z = jnp.concatenate([zeros, kst], axis=-1)
        m = jnp.stack([z[:, :, S5_LANES - s * SSM_GROUP:2 * S5_LANES - s * SSM_GROUP] for s in range(n)], axis=1)
    m = m.reshape(g, S5_LANES, S5_LANES)
    e_idx = np.arange(n) if reverse else (n - 1 - np.arange(n))
    ae_re = pw_re[:, :, e_idx].transpose(0, 2, 1)[:, :, None, :]
    ae_im = pw_im[:, :, e_idx].transpose(0, 2, 1)[:, :, None, :]
    wb_re = (ae_re * bb_re[:, None] - ae_im * bb_im[:, None]).reshape(g, S5_LANES, p)
    wb_im = (ae_re * bb_im[:, None] + ae_im * bb_re[:, None]).reshape(g, S5_LANES, p)
    wb = jnp.concatenate([wb_re, wb_im, wb_im, wb_re], axis=-1)
    f_idx = (n - np.arange(n)) if reverse else (np.arange(n) + 1)
    cf_re, cf_im = c_times_power(f_idx)
    wc = jnp.concatenate([cf_re, -cf_im], axis=1)
    an_re, an_im = pw_re[:, :, n], pw_im[:, :, n]
    coef = jnp.stack([jnp.concatenate([an_re, an_re], axis=-1),
                      jnp.concatenate([-an_im, an_im], axis=-1)], axis=1)
    return wb.astype(jnp.bfloat16), m.astype(jnp.bfloat16), wc.astype(jnp.bfloat16), coef


def _s5_chunks(h):
    t = h.shape[1]
    nk = t // SSM_CHUNK
    u = h.reshape(B, nk, SSM_CHUNK, SSM_GROUPS, SSM_GROUP).transpose(3, 1, 0, 2, 4)
    return u.reshape(SSM_GROUPS, nk * B, S5_LANES)


def _s5_unchunk(y):
    nk = y.shape[1] // B
    y = y.reshape(SSM_GROUPS, nk, B, SSM_CHUNK, SSM_GROUP).transpose(2, 1, 3, 0, 4)
    return y.reshape(B, nk * SSM_CHUNK, D)


def _s5_post_kernel(x_ref, y_ref, g_ref, sh_ref, sc_ref, d_ref, o_ref):
    h = _rms(x_ref[...], g_ref[...]) * (1.0 + sc_ref[...]) + sh_ref[...]
    y = d_ref[...] * h + y_ref[...]
    z = 0.5 * y * (1.0 + jnp.tanh(math.sqrt(2.0 / math.pi) * (y + 0.044715 * (y * y * y))))
    o_ref[...] = z.astype(o_ref.dtype)


def _s5_post(x_lat, y_ssm, g, mod, d_skip):
    row = pl.BlockSpec((None, ROW_TILE, D), lambda b, i: (b, i, 0))
    vec = pl.BlockSpec((1, D), lambda b, i: (0, 0))
    return pl.pallas_call(
        _s5_post_kernel, grid=(B, SEQ // ROW_TILE),
        in_specs=[row, row, vec, _mod_spec(0, 0), _mod_spec(1, 0), vec],
        out_specs=row,
        out_shape=jax.ShapeDtypeStruct((B, SEQ, D), jnp.bfloat16),
        compiler_params=_cparams("parallel", "parallel"), name="s5_skip_gelu",
    )(x_lat, y_ssm, g.reshape(1, D), mod, mod, d_skip.reshape(1, D))


def _layer_modulation(c, c_ctx, w_mod, b_mod):
    cond = jnp.concatenate([c, c_ctx[None, :], jnp.zeros((MOD_ROWS - B - 1, D), jnp.float32)], axis=0)
    return _modulation(cond, w_mod, b_mod).reshape(MOD_ROWS * 6, 1, D)


def kernel(x, c, ctx, c_ctx, l0_w_mod, l0_b_mod, l0_g_pre_mix, l0_g_post_mix, l0_g_pre_ffn, l0_g_post_ffn, l0_w_in, l0_w_out, l0_lambda_q1, l0_lambda_k1, l0_lambda_q2, l0_lambda_k2, l0_g_subln, l0_g_qnorm, l0_g_knorm, l0_w_router, l0_b_router, l0_w_gate_up, l0_b_gate_up, l0_w_down, l0_b_down, l1_w_mod, l1_b_mod, l1_g_pre_mix, l1_g_post_mix, l1_g_pre_ffn, l1_g_post_ffn, l1_ssm_a_re, l1_ssm_a_im, l1_ssm_b_re, l1_ssm_b_im, l1_ssm_c_re, l1_ssm_c_im, l1_ssm_log_dt, l1_ssm_d, l1_w_glu, l1_w_router, l1_b_router, l1_w_gate_up, l1_b_gate_up, l1_w_down, l1_b_down):
    xs = jnp.concatenate([ctx, x], axis=1)

    mod = _layer_modulation(c, c_ctx, l0_w_mod, l0_b_mod)
    h = _norm_mod(xs, l0_g_pre_mix, mod, 0, 1, 1)
    cos2, sin2 = _rope_tables()
    p = _inproj(h.reshape(B * TOK, D), l0_w_in, cos2, sin2, l0_g_qnorm, l0_g_knorm)
    lam_params = jnp.stack([l0_lambda_q1, l0_lambda_k1, l0_lambda_q2, l0_lambda_k2])
    lambda_init = 0.8 - 0.6 * math.exp(-0.3 * 0)
    att = _attention(p.reshape(B, TOK, ATTN_IN), lam_params, l0_g_subln, lambda_init)
    y = _matmul(att.reshape(B * TOK, D), l0_w_out, 1024, 512, name="attn_outproj")
    xs = _post_norm_residual(xs, y.reshape(B, TOK, D), l0_g_post_mix, mod, 2, 1)
    hp, meta, cnt = _norm_mod(xs, l0_g_pre_ffn, mod, 3, 4, 1, router=(l0_w_router, l0_b_router))
    xs = _moe_ffn(xs.reshape(B * TOK, D), hp.reshape(B * TOK, HALF_D), meta.reshape(B * TOK, META_LANES), cnt,
                  l0_g_post_ffn, mod, 5, TOK, CTX,
                  l0_w_gate_up, l0_b_gate_up, l0_w_down, l0_b_down).reshape(B, TOK, D)

    mod = _layer_modulation(c, c_ctx, l1_w_mod, l1_b_mod)
    h = _norm_mod(xs, l1_g_pre_mix, mod, 0, 1, 1)
    u_ctx = _s5_chunks(h[:, :CTX])
    u_lat = _s5_chunks(h[:, CTX:])
    mats = [_s5_matrices(l1_ssm_a_re[d], l1_ssm_a_im[d], l1_ssm_b_re[d], l1_ssm_b_im[d],
                         l1_ssm_c_re[d], l1_ssm_c_im[d], l1_ssm_log_dt[d], reverse=bool(d))
            for d in range(2)]
    wb, m, wc, coef = (jnp.stack([mats[0][i], mats[1][i]]) for i in range(4))
    y_ssm = _s5_unchunk(_s5_scan(u_ctx, u_lat, wb, m, wc, coef))
    x_lat = xs[:, CTX:]
    z = _s5_post(x_lat, y_ssm, l1_g_pre_mix, mod, l1_ssm_d)
    y = _glu_matmul(z.reshape(B * SEQ, D), l1_w_glu, 1024, 512)
    x_lat = _post_norm_residual(x_lat, y.reshape(B, SEQ, D), l1_g_post_mix, mod, 2, 0)
    hp, meta, cnt = _norm_mod(x_lat, l1_g_pre_ffn, mod, 3, 4, 0, router=(l1_w_router, l1_b_router))
    return _moe_ffn(x_lat.reshape(B * SEQ, D), hp.reshape(B * SEQ, HALF_D), meta.reshape(B * SEQ, META_LANES), cnt,
                    l1_g_post_ffn, mod, 5, SEQ, 0,
                    l1_w_gate_up, l1_b_gate_up, l1_w_down, l1_b_down).reshape(B, SEQ, D)
```

```python
import functools
import math

import jax
import jax.numpy as jnp
import numpy as np
from jax import lax
from jax.experimental import pallas as pl
from jax.experimental.pallas import tpu as pltpu

D = 2048
B = 4
SEQ = 2048
CTX = 256
TOK = CTX + SEQ
GRID_W = 64
HD = 128
DIFF_HEADS = 4
GQA_Q_HEADS = 8
GQA_KV_HEADS = 2
GQA_GROUP = GQA_Q_HEADS // GQA_KV_HEADS
ROPE_THETA = 10000.0
ROPE_FREQS = HD // 4
ATTN_IN = 4608
N_EXPERTS = 32
TOP_K = 4
D_FF = D
SWIGLU_LIMIT = 7.0
SWIGLU_ALPHA = 1.702
RMS_EPS = 1e-6
SSM_GROUP = 16
SSM_STATE = 64
SSM_GROUPS = D // SSM_GROUP
SSM_CHUNK = 16

ROW_TILE = 256
MOD_ROWS = 8
MOD_CTX_ROW = B

V7X_VMEM_BYTES = 64 * 1024 * 1024
VMEM_LIMIT = 56 * 1024 * 1024


def _cparams(*sem):
    return pltpu.CompilerParams(dimension_semantics=sem, vmem_limit_bytes=VMEM_LIMIT)


def _rms(x, g):
    return x * lax.rsqrt(jnp.mean(x * x, axis=-1, keepdims=True) + RMS_EPS) * g


def _sigmoid(x):
    return 1.0 / (1.0 + jnp.exp(-x))


def _mod_kernel(c_ref, w_ref, b_ref, o_ref):
    c = c_ref[...]
    a = (c * _sigmoid(c)).astype(jnp.bfloat16)
    o_ref[...] = jnp.dot(a, w_ref[...].astype(jnp.bfloat16),
                         preferred_element_type=jnp.float32) + b_ref[...]


def _modulation(cond, w_mod, b_mod):
    tn = 1024
    n = w_mod.shape[1]
    return pl.pallas_call(
        _mod_kernel,
        grid=(n // tn,),
        in_specs=[pl.BlockSpec((MOD_ROWS, D), lambda j: (0, 0)),
                  pl.BlockSpec((D, tn), lambda j: (0, j)),
                  pl.BlockSpec((1, tn), lambda j: (0, j))],
        out_specs=pl.BlockSpec((MOD_ROWS, tn), lambda j: (0, j)),
        out_shape=jax.ShapeDtypeStruct((MOD_ROWS, n), jnp.float32),
        compiler_params=_cparams("arbitrary"),
        name="adaln_modulation",
    )(cond, w_mod, b_mod.reshape(1, n))


def _mod_spec(which, n_ctx_blocks):
    def idx(b, i):
        r = jnp.where(i < n_ctx_blocks, MOD_CTX_ROW, b)
        return (r * 6 + which, 0, 0)
    return pl.BlockSpec((None, 1, D), idx)


def _norm_mod_kernel(x_ref, g_ref, sh_ref, sc_ref, o_ref):
    h = _rms(x_ref[...], g_ref[...]) * (1.0 + sc_ref[...]) + sh_ref[...]
    o_ref[...] = h.astype(o_ref.dtype)


META_LANES = 128
META_E = 0
META_RANK = TOP_K
META_GATE = 2 * TOP_K
HALF_D = D // 2
HI_MASK = 0xFFFF0000


def _pack_bf16_pair(lo, hi):
    ulo = pltpu.bitcast(lo.astype(jnp.bfloat16).astype(jnp.float32), jnp.uint32)
    uhi = pltpu.bitcast(hi.astype(jnp.bfloat16).astype(jnp.float32), jnp.uint32)
    return lax.shift_right_logical(ulo, jnp.uint32(16)) | (uhi & jnp.uint32(HI_MASK))


def _unpack_bf16_pair(w):
    lo = pltpu.bitcast(lax.shift_left(w, jnp.uint32(16)), jnp.float32).astype(jnp.bfloat16)
    hi = pltpu.bitcast(w & jnp.uint32(HI_MASK), jnp.float32).astype(jnp.bfloat16)
    return lo, hi


def _norm_mod_router_kernel(x_ref, g_ref, sh_ref, sc_ref, wr_ref, br_ref, o_ref, meta_ref, cnt_ref, run_ref):
    first = jnp.logical_and(pl.program_id(0) == 0, pl.program_id(1) == 0)

    @pl.when(first)
    def _():
        run_ref[...] = jnp.zeros_like(run_ref)

    h = _rms(x_ref[...], g_ref[...]) * (1.0 + sc_ref[...]) + sh_ref[...]
    o_ref[...] = _pack_bf16_pair(h[:, :HALF_D], h[:, HALF_D:])
    logits = jnp.dot(h, wr_ref[...], preferred_element_type=jnp.float32,
                     precision=lax.Precision.HIGHEST) + br_ref[...]
    lane = lax.broadcasted_iota(jnp.int32, (ROW_TILE, N_EXPERTS), 1)
    vals, hots = [], []
    l = logits
    for _ in range(TOP_K):
        m = jnp.max(l, axis=-1, keepdims=True)
        idx = jnp.min(jnp.where(l == m, lane, N_EXPERTS), axis=-1, keepdims=True)
        hot = lane == idx
        vals.append(m)
        hots.append(hot)
        l = jnp.where(hot, -jnp.inf, l)
    es = [jnp.exp(v - vals[0]) for v in vals]
    den = es[0] + es[1] + es[2] + es[3]
    onehot = sum(hot.astype(jnp.float32) for hot in hots)
    r_i = lax.broadcasted_iota(jnp.int32, (ROW_TILE, ROW_TILE), 0)
    c_i = lax.broadcasted_iota(jnp.int32, (ROW_TILE, ROW_TILE), 1)
    lower = jnp.where(r_i > c_i, 1.0, 0.0).astype(jnp.bfloat16)
    before = jnp.dot(lower, onehot.astype(jnp.bfloat16), preferred_element_type=jnp.float32) + run_ref[0:1, 0:N_EXPERTS]
    mlane = lax.broadcasted_iota(jnp.int32, (ROW_TILE, META_LANES), 1)
    lane_f = lane.astype(jnp.float32)
    meta = jnp.zeros((ROW_TILE, META_LANES), jnp.float32)
    for k in range(TOP_K):
        hot_f = hots[k].astype(jnp.float32)
        e_k = jnp.sum(hot_f * lane_f, axis=-1, keepdims=True)
        rank_k = jnp.sum(hot_f * before, axis=-1, keepdims=True)
        meta = jnp.where(mlane == META_E + k, e_k, meta)
        meta = jnp.where(mlane == META_RANK + k, rank_k, meta)
        meta = jnp.where(mlane == META_GATE + k, es[k] / den, meta)
    meta_ref[...] = meta
    total = run_ref[0:1, 0:N_EXPERTS] + jnp.sum(onehot, axis=0, keepdims=True)
    run_ref[0:1, 0:N_EXPERTS] = total
    cnt_ref[...] = jnp.broadcast_to(run_ref[0:1, :], cnt_ref.shape)


def _norm_mod(x, g, mod, shift_idx, scale_idx, n_ctx_blocks, router=None, skip_rows=0):
    t = x.shape[1] - skip_rows
    skip = skip_rows // ROW_TILE
    grid = (B, t // ROW_TILE)
    row = pl.BlockSpec((None, ROW_TILE, D), lambda b, i: (b, i, 0))
    in_specs = [pl.BlockSpec((None, ROW_TILE, D), lambda b, i: (b, i + skip, 0)),
                pl.BlockSpec((1, D), lambda b, i: (0, 0)),
                _mod_spec(shift_idx, n_ctx_blocks), _mod_spec(scale_idx, n_ctx_blocks)]
    args = [x, g.reshape(1, D), mod, mod]
    if router is None:
        return pl.pallas_call(
            _norm_mod_kernel, grid=grid, in_specs=in_specs, out_specs=row,
            out_shape=jax.ShapeDtypeStruct((B, t, D), jnp.bfloat16),
            compiler_params=_cparams("parallel", "parallel"), name="norm_mod",
        )(*args)
    w_router, b_router = router
    in_specs += [pl.BlockSpec((D, N_EXPERTS), lambda b, i: (0, 0)),
                 pl.BlockSpec((1, N_EXPERTS), lambda b, i: (0, 0))]
    args += [w_router, b_router.reshape(1, N_EXPERTS)]
    return pl.pallas_call(
        _norm_mod_router_kernel, grid=grid, in_specs=in_specs,
        out_specs=[pl.BlockSpec((None, ROW_TILE, HALF_D), lambda b, i: (b, i, 0)),
                   pl.BlockSpec((None, ROW_TILE, META_LANES), lambda b, i: (b, i, 0)),
                   pl.BlockSpec((8, META_LANES), lambda b, i: (0, 0))],
        out_shape=[jax.ShapeDtypeStruct((B, t, HALF_D), jnp.uint32),
                   jax.ShapeDtypeStruct((B, t, META_LANES), jnp.float32),
                   jax.ShapeDtypeStruct((8, META_LANES), jnp.float32)],
        scratch_shapes=[pltpu.VMEM((8, META_LANES), jnp.float32)],
        compiler_params=_cparams("arbitrary", "arbitrary"), name="norm_mod_router",
    )(*args)


def _post_norm_kernel(x_ref, y_ref, g_ref, gt_ref, o_ref):
    o_ref[...] = x_ref[...] + gt_ref[...] * _rms(y_ref[...], g_ref[...])


def _post_norm_residual(x, y, g, mod, gate_idx, n_ctx_blocks):
    t = y.shape[1]
    skip = (x.shape[1] - t) // ROW_TILE
    row = pl.BlockSpec((None, ROW_TILE, D), lambda b, i: (b, i, 0))
    xrow = pl.BlockSpec((None, ROW_TILE, D), lambda b, i: (b, i + skip, 0))
    return pl.pallas_call(
        _post_norm_kernel, grid=(B, t // ROW_TILE),
        in_specs=[xrow, row, pl.BlockSpec((1, D), lambda b, i: (0, 0)),
                  _mod_spec(gate_idx, n_ctx_blocks)],
        out_specs=xrow,
        out_shape=jax.ShapeDtypeStruct(x.shape, jnp.float32),
        input_output_aliases={0: 0},
        compiler_params=_cparams("parallel", "parallel"), name="post_norm_residual",
    )(x, y, g.reshape(1, D), mod)


def _matmul_kernel(a_ref, w_ref, o_ref):
    o_ref[...] = jnp.dot(a_ref[...], w_ref[...].astype(jnp.bfloat16),
                         preferred_element_type=jnp.float32).astype(o_ref.dtype)


def _matmul(a, w, tm, tn, out_dtype=jnp.float32, name="matmul"):
    m, k = a.shape
    n = w.shape[1]
    return pl.pallas_call(
        _matmul_kernel, grid=(m // tm, n // tn),
        in_specs=[pl.BlockSpec((tm, k), lambda i, j: (i, 0)),
                  pl.BlockSpec((k, tn), lambda i, j: (0, j))],
        out_specs=pl.BlockSpec((tm, tn), lambda i, j: (i, j)),
        out_shape=jax.ShapeDtypeStruct((m, n), out_dtype),
        compiler_params=_cparams("parallel", "arbitrary"), name=name,
    )(a, w)


def _glu_matmul_kernel(a_ref, wv_ref, wg_ref, o_ref):
    a = a_ref[...]
    val = jnp.dot(a, wv_ref[...].astype(jnp.bfloat16), preferred_element_type=jnp.float32)
    gate = jnp.dot(a, wg_ref[...].astype(jnp.bfloat16), preferred_element_type=jnp.float32)
    o_ref[...] = val * _sigmoid(gate)


def _glu_matmul(a, w_glu, tm, tn):
    m, k = a.shape
    n = w_glu.shape[1] // 2
    nj = n // tn
    return pl.pallas_call(
        _glu_matmul_kernel, grid=(m // tm, nj),
        in_specs=[pl.BlockSpec((tm, k), lambda i, j: (i, 0)),
                  pl.BlockSpec((k, tn), lambda i, j: (0, j)),
                  pl.BlockSpec((k, tn), lambda i, j: (0, nj + j))],
        out_specs=pl.BlockSpec((tm, tn), lambda i, j: (i, j)),
        out_shape=jax.ShapeDtypeStruct((m, n), jnp.float32),
        compiler_params=_cparams("parallel", "arbitrary"), name="glu_matmul",
    )(a, w_glu, w_glu)


IN_TN = 256
IN_ROPE_END = 8
IN_DV_END = 12
IN_GQ_END = 16
IN_GK_TILE = 16


def _inproj_kernel(a_ref, w_ref, cos_ref, sin_ref, gq_ref, gk_ref, o_ref):
    j = pl.program_id(1)
    acc = jnp.dot(a_ref[...], w_ref[...].astype(jnp.bfloat16), preferred_element_type=jnp.float32)

    def rope(x):
        return x * cos_ref[...] + pltpu.roll(x, HD // 2, 1) * sin_ref[...]

    def store(fn):
        for c in range(IN_TN // HD):
            o_ref[:, c * HD:(c + 1) * HD] = fn(acc[:, c * HD:(c + 1) * HD]).astype(o_ref.dtype)

    @pl.when(j < IN_ROPE_END)
    def _():
        store(rope)

    @pl.when(jnp.logical_or(jnp.logical_and(j >= IN_ROPE_END, j < IN_DV_END), j > IN_GK_TILE))
    def _():
        store(lambda x: x)

    @pl.when(jnp.logical_and(j >= IN_DV_END, j < IN_GQ_END))
    def _():
        store(lambda x: rope(_rms(x, gq_ref[...])))

    @pl.when(j == IN_GK_TILE)
    def _():
        store(lambda x: rope(_rms(x, gk_ref[...])))


def _inproj(h, w_in, cos2, sin2, g_q, g_k):
    m = h.shape[0]
    const = lambda i, j: (0, 0)
    return pl.pallas_call(
        _inproj_kernel, grid=(m // TOK, ATTN_IN // IN_TN),
        in_specs=[pl.BlockSpec((TOK, D), lambda i, j: (i, 0)),
                  pl.BlockSpec((D, IN_TN), lambda i, j: (0, j)),
                  pl.BlockSpec((TOK, HD), const), pl.BlockSpec((TOK, HD), const),
                  pl.BlockSpec((1, HD), const), pl.BlockSpec((1, HD), const)],
        out_specs=pl.BlockSpec((TOK, IN_TN), lambda i, j: (i, j)),
        out_shape=jax.ShapeDtypeStruct((m, ATTN_IN), jnp.bfloat16),
        compiler_params=_cparams("parallel", "arbitrary"), name="attn_inproj",
    )(h, w_in, cos2, sin2, g_q.reshape(1, HD), g_k.reshape(1, HD))


ATT_TQ = 256
ATT_SCALE = HD ** -0.5


def _softmax_pv(q, k, v):
    s = lax.dot_general(q, k, (((1,), (1,)), ((), ())), preferred_element_type=jnp.float32)
    m = jnp.max(s, axis=-1, keepdims=True)
    e = jnp.exp((s - m) * ATT_SCALE)
    l = jnp.sum(e, axis=-1, keepdims=True)
    return jnp.dot(e.astype(jnp.bfloat16), v, preferred_element_type=jnp.float32), l


def _diff_attn_kernel(lam_ref, q_ref, k_ref, v_ref, g_ref, o_ref, *, lambda_init):
    lp = lam_ref[...]
    lam = (jnp.exp(jnp.sum(lp[0:1] * lp[1:2], axis=-1, keepdims=True))
           - jnp.exp(jnp.sum(lp[2:3] * lp[3:4], axis=-1, keepdims=True)) + lambda_init)

    def run(nk):
        q = q_ref[...]
        k = k_ref[0:nk, :]
        v = v_ref[0:nk, :]
        pv1, l1 = _softmax_pv(q[:, :HD], k[:, :HD], v)
        pv2, l2 = _softmax_pv(q[:, HD:], k[:, HD:], v)
        o = pv1 / l1 - lam * (pv2 / l2)
        o_ref[...] = (_rms(o, g_ref[...]) * (1.0 - lambda_init)).astype(o_ref.dtype)

    @pl.when(pl.program_id(2) == 0)
    def _():
        run(CTX)

    @pl.when(pl.program_id(2) > 0)
    def _():
        run(TOK)


def _gqa_attn_kernel(q_ref, k_ref, v_ref, o_ref):
    def run(nk):
        k = k_ref[0:nk, :]
        v = v_ref[0:nk, :]
        for g in range(GQA_GROUP):
            pv, l = _softmax_pv(q_ref[:, g * HD:(g + 1) * HD], k, v)
            o_ref[:, g * HD:(g + 1) * HD] = (pv / l).astype(o_ref.dtype)

    @pl.when(pl.program_id(2) == 0)
    def _():
        run(CTX)

    @pl.when(pl.program_id(2) > 0)
    def _():
        run(TOK)


def _attention(p, lam_params, g_subln, lambda_init):
    nq = TOK // ATT_TQ
    dv = 2 * HD
    od = pl.pallas_call(
        functools.partial(_diff_attn_kernel, lambda_init=lambda_init),
        grid=(B, DIFF_HEADS, nq),
        in_specs=[pl.BlockSpec((4, HD), lambda b, h, i: (0, 0)),
                  pl.BlockSpec((None, ATT_TQ, dv), lambda b, h, i: (b, i, h)),
                  pl.BlockSpec((None, TOK, dv), lambda b, h, i: (b, 0, DIFF_HEADS + h)),
                  pl.BlockSpec((None, TOK, dv), lambda b, h, i: (b, 0, 2 * DIFF_HEADS + h)),
                  pl.BlockSpec((1, dv), lambda b, h, i: (0, 0))],
        out_specs=pl.BlockSpec((None, ATT_TQ, dv), lambda b, h, i: (b, i, h)),
        out_shape=jax.ShapeDtypeStruct((B, TOK, DIFF_HEADS * dv), jnp.bfloat16),
        compiler_params=_cparams("parallel", "parallel", "arbitrary"), name="diff_attention",
    )(lam_params, p, p, p, g_subln.reshape(1, dv))
    gq_w = GQA_GROUP * HD
    gq0 = 3072 // gq_w
    gk0 = 4096 // HD
    gv0 = 4352 // HD
    og = pl.pallas_call(
        _gqa_attn_kernel,
        grid=(B, GQA_KV_HEADS, nq),
        in_specs=[pl.BlockSpec((None, ATT_TQ, gq_w), lambda b, n, i: (b, i, gq0 + n)),
                  pl.BlockSpec((None, TOK, HD), lambda b, n, i: (b, 0, gk0 + n)),
                  pl.BlockSpec((None, TOK, HD), lambda b, n, i: (b, 0, gv0 + n))],
        out_specs=pl.BlockSpec((None, ATT_TQ, gq_w), lambda b, n, i: (b, i, n)),
        out_shape=jax.ShapeDtypeStruct((B, TOK, GQA_Q_HEADS * HD), jnp.bfloat16),
        compiler_params=_cparams("parallel", "parallel", "arbitrary"), name="gqa_attention",
    )(p, p, p)
    return jnp.concatenate([od, og], axis=-1)


def _rope_tables():
    rows = SEQ // GRID_W
    row_id, col_id = jnp.meshgrid(jnp.arange(rows), jnp.arange(GRID_W), indexing="ij")
    inv_freq = ROPE_THETA ** (-jnp.arange(ROPE_FREQS, dtype=jnp.float32) / ROPE_FREQS)
    ang = jnp.concatenate([row_id.reshape(-1, 1) * inv_freq, col_id.reshape(-1, 1) * inv_freq], axis=-1)
    cos, sin = jnp.cos(ang), jnp.sin(ang)
    cos2 = jnp.concatenate([cos, cos], axis=-1)
    sin2 = jnp.concatenate([-sin, sin], axis=-1)
    cos2 = jnp.concatenate([jnp.ones((CTX, HD), jnp.float32), cos2], axis=0)
    sin2 = jnp.concatenate([jnp.zeros((CTX, HD), jnp.float32), sin2], axis=0)
    return cos2, sin2


MOE_TM = 512
MOE_SUB = 256
MOE_TF = 1024
DISPATCH_TT = 256
COMBINE_TT = 128


def _for_valid_rows(rows, compute, o_ref):
    half = MOE_SUB // 2
    width = o_ref.shape[1]
    for s in range(MOE_TM // MOE_SUB):
        base = s * MOE_SUB

        @pl.when(rows > base + half)
        def _():
            compute(slice(base, base + MOE_SUB))

        @pl.when(jnp.logical_and(rows > base, rows <= base + half))
        def _():
            compute(slice(base, base + half))
            o_ref[base + half:base + MOE_SUB, :] = jnp.zeros((half, width), o_ref.dtype)

        @pl.when(rows <= base)
        def _():
            o_ref[base:base + MOE_SUB, :] = jnp.zeros((MOE_SUB, width), o_ref.dtype)


def _moe_up_kernel(be_ref, first_ref, rows_ref, nxt_ref, x_ref, w_hbm, bg_ref, bl_ref, o_ref,
                   wst, wbf, sem):
    f = pl.program_id(0)
    b = pl.program_id(1)
    nf = pl.num_programs(0)

    def copies(e, ff):
        col = pl.multiple_of(ff * MOE_TF, MOE_TF)
        return [pltpu.make_async_copy(w_hbm.at[e, :, pl.ds(part * D_FF + col, MOE_TF)],
                                      wst.at[part], sem.at[part]) for part in range(2)]

    @pl.when(first_ref[b] == 1)
    def _():
        @pl.when(jnp.logical_and(f == 0, b == 0))
        def _():
            for c in copies(be_ref[0], 0):
                c.start()

        for c in copies(be_ref[b], f):
            c.wait()
        wbf[...] = wst[...].astype(jnp.bfloat16)
        e_next = nxt_ref[b]

        @pl.when(e_next >= 0)
        def _():
            for c in copies(e_next, f):
                c.start()

        @pl.when(jnp.logical_and(e_next < 0, f + 1 < nf))
        def _():
            for c in copies(be_ref[0], f + 1):
                c.start()

    def compute(sl):
        lo, hi = _unpack_bf16_pair(x_ref[sl, :])
        glu = (jnp.dot(lo, wbf[0, :HALF_D, :], preferred_element_type=jnp.float32)
               + jnp.dot(hi, wbf[0, HALF_D:, :], preferred_element_type=jnp.float32) + bg_ref[...])
        lin = (jnp.dot(lo, wbf[1, :HALF_D, :], preferred_element_type=jnp.float32)
               + jnp.dot(hi, wbf[1, HALF_D:, :], preferred_element_type=jnp.float32) + bl_ref[...])
        glu = jnp.minimum(glu, SWIGLU_LIMIT)
        lin = jnp.clip(lin, -SWIGLU_LIMIT, SWIGLU_LIMIT)
        o_ref[sl, :] = (glu * _sigmoid(SWIGLU_ALPHA * glu) * (lin + 1.0)).astype(o_ref.dtype)

    _for_valid_rows(rows_ref[b], compute, o_ref)


def _moe_down_kernel(be_ref, first_ref, rows_ref, nxt_ref, a_ref, w_hbm, bias_ref, o_ref, wst, wbf, sem):
    b = pl.program_id(0)

    def copy(e):
        return pltpu.make_async_copy(w_hbm.at[e], wst, sem)

    @pl.when(first_ref[b] == 1)
    def _():
        @pl.when(b == 0)
        def _():
            copy(be_ref[0]).start()

        copy(be_ref[b]).wait()
        wbf[...] = wst[...].astype(jnp.bfloat16)
        e_next = nxt_ref[b]

        @pl.when(e_next >= 0)
        def _():
            copy(e_next).start()

    def compute(sl):
        o_ref[sl, :] = jnp.dot(a_ref[sl, :], wbf[...], preferred_element_type=jnp.float32) + bias_ref[...]

    _for_valid_rows(rows_ref[b], compute, o_ref)


def _moe_experts(x_sorted, tables, w_gate_up, b_gate_up, w_down, b_down):
    r = x_sorted.shape[0]
    nb = r // MOE_TM
    nf = D_FF // MOE_TF
    bgu = b_gate_up.reshape(N_EXPERTS, 1, 2 * D_FF)
    act = pl.pallas_call(
        _moe_up_kernel,
        grid_spec=pltpu.PrefetchScalarGridSpec(
            num_scalar_prefetch=4, grid=(nf, nb),
            in_specs=[pl.BlockSpec((MOE_TM, HALF_D), lambda f, b, be, fi, ro, nx: (b, 0)),
                      pl.BlockSpec(memory_space=pl.ANY),
                      pl.BlockSpec((None, 1, MOE_TF), lambda f, b, be, fi, ro, nx: (be[b], 0, f)),
                      pl.BlockSpec((None, 1, MOE_TF), lambda f, b, be, fi, ro, nx: (be[b], 0, nf + f))],
            out_specs=pl.BlockSpec((MOE_TM, MOE_TF), lambda f, b, be, fi, ro, nx: (b, f)),
            scratch_shapes=[pltpu.VMEM((2, D, MOE_TF), jnp.float32), pltpu.VMEM((2, D, MOE_TF), jnp.bfloat16),
                            pltpu.SemaphoreType.DMA((2,))]),
        out_shape=jax.ShapeDtypeStruct((r, D_FF), jnp.bfloat16),
        compiler_params=_cparams("arbitrary", "arbitrary"), name="moe_gate_up",
    )(*tables, x_sorted, w_gate_up, bgu, bgu)
    return pl.pallas_call(
        _moe_down_kernel,
        grid_spec=pltpu.PrefetchScalarGridSpec(
            num_scalar_prefetch=4, grid=(nb,),
            in_specs=[pl.BlockSpec((MOE_TM, D_FF), lambda b, be, fi, ro, nx: (b, 0)),
                      pl.BlockSpec(memory_space=pl.ANY),
                      pl.BlockSpec((None, 1, D), lambda b, be, fi, ro, nx: (be[b], 0, 0))],
            out_specs=pl.BlockSpec((MOE_TM, D), lambda b, be, fi, ro, nx: (b, 0)),
            scratch_shapes=[pltpu.VMEM((D_FF, D), jnp.float32), pltpu.VMEM((D_FF, D), jnp.bfloat16),
                            pltpu.SemaphoreType.DMA]),
        out_shape=jax.ShapeDtypeStruct((r, D), jnp.float32),
        compiler_params=_cparams("arbitrary"), name="moe_down",
    )(*tables, act, w_down, b_down.reshape(N_EXPERTS, 1, D))


def _dispatch_kernel(dest_ref, hp_ref, xs_in_hbm, xs_hbm, idx_smem, sem_idx, sem_rows):
    del xs_in_hbm
    idx_copy = pltpu.make_async_copy(dest_ref, idx_smem, sem_idx)
    idx_copy.start()
    idx_copy.wait()

    def issue(t, carry):
        for k in range(TOP_K):
            d = idx_smem[0, 0, t * TOP_K + k]
            pltpu.make_async_copy(hp_ref.at[t], xs_hbm.at[d], sem_rows).start(priority=k % 2)
        return carry

    lax.fori_loop(0, DISPATCH_TT, issue, 0, unroll=4)
    for k in range(TOP_K):
        pltpu.make_async_copy(hp_ref, xs_hbm.at[pl.ds(0, DISPATCH_TT)], sem_rows).wait()


def _dispatch(dest, hp, n_rows):
    n_tok = hp.shape[0]
    n = n_tok // DISPATCH_TT
    width = DISPATCH_TT * TOP_K
    zeros = jnp.zeros((n_rows, HALF_D), jnp.uint32)
    return pl.pallas_call(
        _dispatch_kernel, grid=(n,),
        in_specs=[pl.BlockSpec((1, 1, width), lambda i: (i, 0, 0)),
                  pl.BlockSpec((DISPATCH_TT, HALF_D), lambda i: (i, 0)), pl.BlockSpec(memory_space=pl.ANY)],
        out_specs=pl.BlockSpec(memory_space=pl.ANY),
        out_shape=jax.ShapeDtypeStruct((n_rows, HALF_D), jnp.uint32),
        scratch_shapes=[pltpu.SMEM((1, 1, width), jnp.int32), pltpu.SemaphoreType.DMA, pltpu.SemaphoreType.DMA],
        input_output_aliases={2: 0},
        compiler_params=_cparams("arbitrary"), name="moe_dispatch",
    )(dest.reshape(n, 1, width), hp, zeros)


def _combine_kernel(dcur_ref, dnext_ref, out_hbm, meta_ref, x_ref, g_ref, gt_ref, o_ref,
                    idx_smem, buf, sem_idx, sem_rows):
    i = pl.program_id(0)
    n = pl.num_programs(0)

    def row_copy(d, slot, k, t):
        return pltpu.make_async_copy(out_hbm.at[d], buf.at[slot, k, t], sem_rows.at[slot])

    def gather(d_ref, slot):
        idx_copy = pltpu.make_async_copy(d_ref, idx_smem, sem_idx)
        idx_copy.start()
        idx_copy.wait()

        def issue(t, carry):
            for k in range(TOP_K):
                row_copy(idx_smem[0, 0, t * TOP_K + k], slot, k, t).start(priority=k % 2)
            return carry
        lax.fori_loop(0, COMBINE_TT, issue, 0, unroll=4)

    @pl.when(i == 0)
    def _():
        gather(dcur_ref, 0)

    @pl.when(i + 1 < n)
    def _():
        gather(dnext_ref, (i + 1) % 2)

    slot = i % 2
    pltpu.make_async_copy(buf.at[slot], buf.at[slot], sem_rows.at[slot]).wait()

    meta = meta_ref[...]
    f = meta[:, META_GATE:META_GATE + 1] * buf[slot, 0]
    for k in range(1, TOP_K):
        f = f + meta[:, META_GATE + k:META_GATE + k + 1] * buf[slot, k]
    o_ref[...] = x_ref[...] + gt_ref[...] * _rms(f, g_ref[...])


def _combine(dest, out_sorted, meta, x, g, mod, gate_idx, rows_per_batch, ctx_rows):
    n_tok = dest.shape[0]
    n = n_tok // COMBINE_TT
    width = COMBINE_TT * TOP_K
    per_batch = rows_per_batch // COMBINE_TT
    x_per_batch = x.shape[0] // B // COMBINE_TT
    skip = x_per_batch - per_batch
    ctx_blocks = ctx_rows // COMBINE_TT

    def gate_row(i):
        r = jnp.where(i % per_batch < ctx_blocks, MOD_CTX_ROW, i // per_batch)
        return (r * 6 + gate_idx, 0, 0)

    row = pl.BlockSpec((COMBINE_TT, D), lambda i: (i, 0))
    xrow = pl.BlockSpec((COMBINE_TT, D), lambda i: ((i // per_batch) * x_per_batch + skip + i % per_batch, 0))
    return pl.pallas_call(
        _combine_kernel, grid=(n,),
        in_specs=[pl.BlockSpec((1, 1, width), lambda i: (i, 0, 0)),
                  pl.BlockSpec((1, 1, width), lambda i: (jnp.minimum(i + 1, n - 1), 0, 0)),
                  pl.BlockSpec(memory_space=pl.ANY),
                  pl.BlockSpec((COMBINE_TT, META_LANES), lambda i: (i, 0)),
                  xrow, pl.BlockSpec((1, D), lambda i: (0, 0)),
                  pl.BlockSpec((None, 1, D), gate_row)],
        out_specs=row,
        out_shape=jax.ShapeDtypeStruct((n_tok, D), jnp.float32),
        scratch_shapes=[pltpu.SMEM((1, 1, width), jnp.int32),
                        pltpu.VMEM((2, TOP_K, COMBINE_TT, D), jnp.float32),
                        pltpu.SemaphoreType.DMA, pltpu.SemaphoreType.DMA((2,))],
        input_output_aliases={4: 0} if skip == 0 else {},
        compiler_params=_cparams("arbitrary"), name="moe_combine",
    )(dest.reshape(n, 1, width), dest.reshape(n, 1, width), out_sorted, meta, x, g.reshape(1, D), mod)


def _moe_block_tables(counts, n_assign):
    padded = (counts + MOE_TM - 1) // MOE_TM * MOE_TM
    pad_end = jnp.cumsum(padded)
    pad_start = pad_end - padded
    nb = -(-(n_assign + N_EXPERTS * (MOE_TM - 1)) // MOE_TM)
    block_start = jnp.arange(nb, dtype=jnp.int32) * MOE_TM
    block_e_raw = jnp.minimum(jnp.sum(block_start[:, None] >= pad_end[None, :], axis=1), N_EXPERTS - 1)
    block_e_raw = block_e_raw.astype(jnp.int32)
    onehot_e = block_e_raw[:, None] == jnp.arange(N_EXPERTS)[None, :]
    valid_end = jnp.sum(jnp.where(onehot_e, (pad_start + counts)[None, :], 0), axis=1)
    block_rows = jnp.clip(valid_end - block_start, 0, MOE_TM).astype(jnp.int32)
    block_rows = jnp.where(block_start < pad_end[-1], block_rows, 0)
    last_e = jnp.max(jnp.where(block_rows > 0, block_e_raw, 0))
    block_e = jnp.where(block_rows > 0, block_e_raw, last_e)
    block_first = jnp.concatenate([jnp.ones((1,), jnp.int32),
                                   (block_e[1:] != block_e[:-1]).astype(jnp.int32)])
    idx = jnp.arange(nb, dtype=jnp.int32)
    first_pos = jnp.where(block_first == 1, idx, nb)
    later = jnp.where(idx[None, :] > idx[:, None], first_pos[None, :], nb)
    next_pos = jnp.min(later, axis=1)
    next_e = jnp.sum(jnp.where(idx[None, :] == next_pos[:, None], block_e[None, :], 0), axis=1)
    block_next = jnp.where(next_pos < nb, next_e, -1).astype(jnp.int32)
    return pad_start, nb * MOE_TM, (block_e, block_first, block_rows, block_next)


def _moe_ffn(x_res, hp, meta, cnt, g_post, mod, gate_idx, rows_per_batch, ctx_rows,
             w_gate_up, b_gate_up, w_down, b_down):
    n_tok = hp.shape[0]
    counts = cnt[0, :N_EXPERTS].astype(jnp.int32)
    pad_start, n_rows, tables = _moe_block_tables(counts, n_tok * TOP_K)
    top_e = meta[:, META_E:META_E + TOP_K].astype(jnp.int32)
    rank = meta[:, META_RANK:META_RANK + TOP_K].astype(jnp.int32)
    hot = top_e[:, :, None] == jnp.arange(N_EXPERTS)[None, None, :]
    dest = jnp.sum(jnp.where(hot, pad_start[None, None, :], 0), axis=-1) + rank
    x_sorted = _dispatch(dest, hp, n_rows)
    out_sorted = _moe_experts(x_sorted, tables, w_gate_up, b_gate_up, w_down, b_down)
    return _combine(dest, out_sorted, meta, x_res, g_post, mod, gate_idx, rows_per_batch, ctx_rows)


S5_GB = 8
S5_LANES = SSM_CHUNK * SSM_GROUP
S5_NK_CTX = CTX // SSM_CHUNK
S5_NK_LAT = SEQ // SSM_CHUNK
S5_PAIR = 2 * B


def _s5_direction(reverse, uc_ref, ul_ref, wb_ref, m_ref, wc_ref, coef_ref, y_ref,
                  s1c, s2c, s1l, s2l, xin, accumulate):
    half = 2 * SSM_STATE
    for g in range(S5_GB):
        sc = jnp.dot(uc_ref[g], wb_ref[g], preferred_element_type=jnp.float32)
        s1c[g] = sc[:, :half]
        s2c[g] = sc[:, half:]
        sl = jnp.dot(ul_ref[g], wb_ref[g], preferred_element_type=jnp.float32)
        s1l[g] = sl[:, :half]
        s2l[g] = sl[:, half:]

    lower = lax.broadcasted_iota(jnp.int32, (S5_PAIR, half), 0) < B
    p1 = [jnp.broadcast_to(coef_ref[g, 0:1, :], (S5_PAIR, half)) for g in range(S5_GB)]
    p2 = [jnp.broadcast_to(coef_ref[g, 1:2, :], (S5_PAIR, half)) for g in range(S5_GB)]

    def tile_step(g, t1, t2, v1, v2):
        y1a = p1[g] * v1 + p2[g] * v2 + t1
        y1b = p1[g] * v2 - p2[g] * v1 + t2
        r1a = pltpu.roll(y1a, B, 0)
        r1b = pltpu.roll(y1b, B, 0)
        y2a = p1[g] * r1a + p2[g] * r1b + t1
        y2b = p1[g] * r1b - p2[g] * r1a + t2
        r2a = pltpu.roll(y2a, B, 0)
        r2b = pltpu.roll(y2b, B, 0)
        if not reverse:
            x_in = jnp.where(lower, v1, r1a)
            return x_in, jnp.where(lower, r2a, y2a), jnp.where(lower, r2b, y2b)
        x_in = jnp.where(lower, r1a, v1)
        return x_in, jnp.where(lower, y2a, r2a), jnp.where(lower, y2b, r2b)

    def scan(s1, s2, n_tiles, state, record):
        def body(j, carry):
            jj = (n_tiles - 1 - j) if reverse else j
            r0 = pl.multiple_of(jj * S5_PAIR, S5_PAIR)
            new = []
            for g in range(S5_GB):
                v1, v2 = carry[2 * g], carry[2 * g + 1]
                x_in, v1, v2 = tile_step(g, s1[g, pl.ds(r0, S5_PAIR), :], s2[g, pl.ds(r0, S5_PAIR), :], v1, v2)
                if record:
                    xin[g, pl.ds(r0, S5_PAIR), :] = x_in
                new += [v1, v2]
            return tuple(new)
        return lax.fori_loop(0, n_tiles, body, state)

    zero = jnp.zeros((S5_PAIR, half), jnp.float32)
    state = tuple(zero for _ in range(2 * S5_GB))
    state = scan(s1c, s2c, S5_NK_CTX * B // S5_PAIR, state, False)
    scan(s1l, s2l, S5_NK_LAT * B // S5_PAIR, state, True)

    for g in range(S5_GB):
        y = (jnp.dot(ul_ref[g], m_ref[g], preferred_element_type=jnp.float32)
             + jnp.dot(xin[g].astype(jnp.bfloat16), wc_ref[g], preferred_element_type=jnp.float32))
        if accumulate:
            y_ref[g] = y_ref[g] + y
        else:
            y_ref[g] = y


def _s5_kernel(uc_ref, ul_ref, wb_ref, m_ref, wc_ref, coef_ref, y_ref, s1c, s2c, s1l, s2l, xin):
    args = (uc_ref, ul_ref, wb_ref, m_ref, wc_ref, coef_ref, y_ref, s1c, s2c, s1l, s2l, xin)

    @pl.when(pl.program_id(1) == 0)
    def _():
        _s5_direction(False, *args, accumulate=False)

    @pl.when(pl.program_id(1) == 1)
    def _():
        _s5_direction(True, *args, accumulate=True)


def _s5_scan(u_ctx, u_lat, wb, m, wc, coef):
    rc, rl = u_ctx.shape[1], u_lat.shape[1]
    half = 2 * SSM_STATE
    wspec = lambda k, n: pl.BlockSpec((None, S5_GB, k, n), lambda gi, d: (d, gi, 0, 0))
    return pl.pallas_call(
        _s5_kernel, grid=(SSM_GROUPS // S5_GB, 2),
        in_specs=[pl.BlockSpec((S5_GB, rc, S5_LANES), lambda gi, d: (gi, 0, 0)),
                  pl.BlockSpec((S5_GB, rl, S5_LANES), lambda gi, d: (gi, 0, 0)),
                  wspec(S5_LANES, 2 * half), wspec(S5_LANES, S5_LANES), wspec(half, S5_LANES),
                  wspec(2, half)],
        out_specs=pl.BlockSpec((S5_GB, rl, S5_LANES), lambda gi, d: (gi, 0, 0)),
        out_shape=jax.ShapeDtypeStruct((SSM_GROUPS, rl, S5_LANES), jnp.float32),
        scratch_shapes=[pltpu.VMEM((S5_GB, rc, half), jnp.float32), pltpu.VMEM((S5_GB, rc, half), jnp.float32),
                        pltpu.VMEM((S5_GB, rl, half), jnp.float32), pltpu.VMEM((S5_GB, rl, half), jnp.float32),
                        pltpu.VMEM((S5_GB, rl, half), jnp.float32)],
        compiler_params=_cparams("parallel", "arbitrary"), name="s5_scan",
    )(u_ctx, u_lat, wb, m, wc, coef)


def _s5_matrices(a_re, a_im, b_re, b_im, c_re, c_im, log_dt, reverse):
    hp = lax.Precision.HIGHEST
    n = SSM_CHUNK
    g, p = SSM_GROUPS, SSM_STATE
    dt = jnp.exp(log_dt)[:, None]
    mag = jnp.exp(a_re * dt)
    ab_re, ab_im = mag * jnp.cos(a_im * dt), mag * jnp.sin(a_im * dt)
    den = a_re * a_re + a_im * a_im
    f_re = ((ab_re - 1.0) * a_re + ab_im * a_im) / den
    f_im = (ab_im * a_re - (ab_re - 1.0) * a_im) / den
    b_re_t, b_im_t = b_re.transpose(0, 2, 1), b_im.transpose(0, 2, 1)
    bb_re = f_re[:, None, :] * b_re_t - f_im[:, None, :] * b_im_t
    bb_im = f_re[:, None, :] * b_im_t + f_im[:, None, :] * b_re_t
    tau = jnp.arange(n + 1, dtype=jnp.float32)[None, None, :]
    ang_re, ang_im = (a_re * dt)[:, :, None], (a_im * dt)[:, :, None]
    pmag = jnp.exp(tau * ang_re)
    pw_re, pw_im = pmag * jnp.cos(tau * ang_im), pmag * jnp.sin(tau * ang_im)
    c_re_t, c_im_t = c_re.transpose(0, 2, 1), c_im.transpose(0, 2, 1)

    def c_times_power(exps):
        q_re, q_im = pw_re[:, :, exps], pw_im[:, :, exps]
        re = c_re_t[:, :, None, :] * q_re[:, :, :, None] - c_im_t[:, :, None, :] * q_im[:, :, :, None]
        im = c_re_t[:, :, None, :] * q_im[:, :, :, None] + c_im_t[:, :, None, :] * q_re[:, :, :, None]
        return re.reshape(g, p, -1), im.reshape(g, p, -1)

    lags = np.arange(n)[::-1] if reverse else np.arange(n)
    ca_re, ca_im = c_times_power(lags)
    kst = jnp.matmul(jnp.concatenate([bb_re, bb_im], axis=-1), jnp.concatenate([ca_re, -ca_im], axis=1),
                     precision=hp)
    zeros = jnp.zeros_like(kst)
    if reverse:
        z = jnp.concatenate([kst, zeros], axis=-1)
        m = jnp.stack([z[:, :, (n - 1 - s) * SSM_GROUP:(n - 1 - s) * SSM_GROUP + S5_LANES] for s in range(n)], axis=1)
    else:
        z = jnp.concatenate([zeros, kst], axis=-1)
        m = jnp.stack([z[:, :, S5_LANES - s * SSM_GROUP:2 * S5_LANES - s * SSM_GROUP] for s in range(n)], axis=1)
    m = m.reshape(g, S5_LANES, S5_LANES)
    e_idx = np.arange(n) if reverse else (n - 1 - np.arange(n))
    ae_re = pw_re[:, :, e_idx].transpose(0, 2, 1)[:, :, None, :]
    ae_im = pw_im[:, :, e_idx].transpose(0, 2, 1)[:, :, None, :]
    wb_re = (ae_re * bb_re[:, None] - ae_im * bb_im[:, None]).reshape(g, S5_LANES, p)
    wb_im = (ae_re * bb_im[:, None] + ae_im * bb_re[:, None]).reshape(g, S5_LANES, p)
    wb = jnp.concatenate([wb_re, wb_im, wb_im, wb_re], axis=-1)
    f_idx = (n - np.arange(n)) if reverse else (np.arange(n) + 1)
    cf_re, cf_im = c_times_power(f_idx)
    wc = jnp.concatenate([cf_re, -cf_im], axis=1)
    an_re, an_im = pw_re[:, :, n], pw_im[:, :, n]
    coef = jnp.stack([jnp.concatenate([an_re, an_re], axis=-1),
                      jnp.concatenate([-an_im, an_im], axis=-1)], axis=1)
    return wb.astype(jnp.bfloat16), m.astype(jnp.bfloat16), wc.astype(jnp.bfloat16), coef


def _s5_chunks(h):
    t = h.shape[1]
    nk = t // SSM_CHUNK
    u = h.reshape(B, nk, SSM_CHUNK, SSM_GROUPS, SSM_GROUP).transpose(3, 1, 0, 2, 4)
    return u.reshape(SSM_GROUPS, nk * B, S5_LANES)


def _s5_unchunk(y):
    nk = y.shape[1] // B
    y = y.reshape(SSM_GROUPS, nk, B, SSM_CHUNK, SSM_GROUP).transpose(2, 1, 3, 0, 4)
    return y.reshape(B, nk * SSM_CHUNK, D)


def _s5_post_kernel(x_ref, y_ref, g_ref, sh_ref, sc_ref, d_ref, o_ref):
    h = _rms(x_ref[...], g_ref[...]) * (1.0 + sc_ref[...]) + sh_ref[...]
    y = d_ref[...] * h + y_ref[...]
    z = 0.5 * y * (1.0 + jnp.tanh(math.sqrt(2.0 / math.pi) * (y + 0.044715 * (y * y * y))))
    o_ref[...] = z.astype(o_ref.dtype)


def _s5_post(xs, y_ssm, g, mod, d_skip):
    skip = CTX // ROW_TILE
    row = pl.BlockSpec((None, ROW_TILE, D), lambda b, i: (b, i, 0))
    xrow = pl.BlockSpec((None, ROW_TILE, D), lambda b, i: (b, i + skip, 0))
    vec = pl.BlockSpec((1, D), lambda b, i: (0, 0))
    return pl.pallas_call(
        _s5_post_kernel, grid=(B, SEQ // ROW_TILE),
        in_specs=[xrow, row, vec, _mod_spec(0, 0), _mod_spec(1, 0), vec],
        out_specs=row,
        out_shape=jax.ShapeDtypeStruct((B, SEQ, D), jnp.bfloat16),
        compiler_params=_cparams("parallel", "parallel"), name="s5_skip_gelu",
    )(xs, y_ssm, g.reshape(1, D), mod, mod, d_skip.reshape(1, D))


def _layer_modulation(c, c_ctx, w_mod, b_mod):
    cond = jnp.concatenate([c, c_ctx[None, :], jnp.zeros((MOD_ROWS - B - 1, D), jnp.float32)], axis=0)
    return _modulation(cond, w_mod, b_mod).reshape(MOD_ROWS * 6, 1, D)


def kernel(x, c, ctx, c_ctx, l0_w_mod, l0_b_mod, l0_g_pre_mix, l0_g_post_mix, l0_g_pre_ffn, l0_g_post_ffn, l0_w_in, l0_w_out, l0_lambda_q1, l0_lambda_k1, l0_lambda_q2, l0_lambda_k2, l0_g_subln, l0_g_qnorm, l0_g_knorm, l0_w_router, l0_b_router, l0_w_gate_up, l0_b_gate_up, l0_w_down, l0_b_down, l1_w_mod, l1_b_mod, l1_g_pre_mix, l1_g_post_mix, l1_g_pre_ffn, l1_g_post_ffn, l1_ssm_a_re, l1_ssm_a_im, l1_ssm_b_re, l1_ssm_b_im, l1_ssm_c_re, l1_ssm_c_im, l1_ssm_log_dt, l1_ssm_d, l1_w_glu, l1_w_router, l1_b_router, l1_w_gate_up, l1_b_gate_up, l1_w_down, l1_b_down):
    xs = jnp.concatenate([ctx, x], axis=1)

    mod = _layer_modulation(c, c_ctx, l0_w_mod, l0_b_mod)
    h = _norm_mod(xs, l0_g_pre_mix, mod, 0, 1, 1)
    cos2, sin2 = _rope_tables()
    p = _inproj(h.reshape(B * TOK, D), l0_w_in, cos2, sin2, l0_g_qnorm, l0_g_knorm)
    lam_params = jnp.stack([l0_lambda_q1, l0_lambda_k1, l0_lambda_q2, l0_lambda_k2])
    lambda_init = 0.8 - 0.6 * math.exp(-0.3 * 0)
    att = _attention(p.reshape(B, TOK, ATTN_IN), lam_params, l0_g_subln, lambda_init)
    y = _matmul(att.reshape(B * TOK, D), l0_w_out, 1024, 512, name="attn_outproj")
    xs = _post_norm_residual(xs, y.reshape(B, TOK, D), l0_g_post_mix, mod, 2, 1)
    hp, meta, cnt = _norm_mod(xs, l0_g_pre_ffn, mod, 3, 4, 1, router=(l0_w_router, l0_b_router))
    xs = _moe_ffn(xs.reshape(B * TOK, D), hp.reshape(B * TOK, HALF_D), meta.reshape(B * TOK, META_LANES), cnt,
                  l0_g_post_ffn, mod, 5, TOK, CTX,
                  l0_w_gate_up, l0_b_gate_up, l0_w_down, l0_b_down).reshape(B, TOK, D)

    mod = _layer_modulation(c, c_ctx, l1_w_mod, l1_b_mod)
    h = _norm_mod(xs, l1_g_pre_mix, mod, 0, 1, 1)
    u_ctx = _s5_chunks(h[:, :CTX])
    u_lat = _s5_chunks(h[:, CTX:])
    mats = [_s5_matrices(l1_ssm_a_re[d], l1_ssm_a_im[d], l1_ssm_b_re[d], l1_ssm_b_im[d],
                         l1_ssm_c_re[d], l1_ssm_c_im[d], l1_ssm_log_dt[d], reverse=bool(d))
            for d in range(2)]
    wb, m, wc, coef = (jnp.stack([mats[0][i], mats[1][i]]) for i in range(4))
    y_ssm = _s5_unchunk(_s5_scan(u_ctx, u_lat, wb, m, wc, coef))
    z = _s5_post(xs, y_ssm, l1_g_pre_mix, mod, l1_ssm_d)
    y = _glu_matmul(z.reshape(B * SEQ, D), l1_w_glu, 1024, 512)
    xs = _post_norm_residual(xs, y.reshape(B, SEQ, D), l1_g_post_mix, mod, 2, 0)
    hp, meta, cnt = _norm_mod(xs, l1_g_pre_ffn, mod, 3, 4, 0, router=(l1_w_router, l1_b_router), skip_rows=CTX)
    return _moe_ffn(xs.reshape(B * TOK, D), hp.reshape(B * SEQ, HALF_D), meta.reshape(B * SEQ, META_LANES), cnt,
                    l1_g_post_ffn, mod, 5, SEQ, 0,
                    l1_w_gate_up, l1_b_gate_up, l1_w_down, l1_b_down).reshape(B, SEQ, D)
```

```python
import functools
import math

import jax
import jax.numpy as jnp
import numpy as np
from jax import lax
from jax.experimental import pallas as pl
from jax.experimental.pallas import tpu as pltpu

D = 2048
B = 4
SEQ = 2048
CTX = 256
TOK = CTX + SEQ
GRID_W = 64
HD = 128
DIFF_HEADS = 4
GQA_Q_HEADS = 8
GQA_KV_HEADS = 2
GQA_GROUP = GQA_Q_HEADS // GQA_KV_HEADS
ROPE_THETA = 10000.0
ROPE_FREQS = HD // 4
ATTN_IN = 4608
N_EXPERTS = 32
TOP_K = 4
D_FF = D
SWIGLU_LIMIT = 7.0
SWIGLU_ALPHA = 1.702
RMS_EPS = 1e-6
SSM_GROUP = 16
SSM_STATE = 64
SSM_GROUPS = D // SSM_GROUP
SSM_CHUNK = 16

ROW_TILE = 256
MOD_ROWS = 8
MOD_CTX_ROW = B

V7X_VMEM_BYTES = 64 * 1024 * 1024
VMEM_LIMIT = 56 * 1024 * 1024


def _cparams(*sem):
    return pltpu.CompilerParams(dimension_semantics=sem, vmem_limit_bytes=VMEM_LIMIT)


def _rms(x, g):
    return x * lax.rsqrt(jnp.mean(x * x, axis=-1, keepdims=True) + RMS_EPS) * g


def _sigmoid(x):
    return 1.0 / (1.0 + jnp.exp(-x))


def _mod_kernel(c_ref, w_ref, b_ref, o_ref):
    c = c_ref[...]
    a = (c * _sigmoid(c)).astype(jnp.bfloat16)
    o_ref[...] = jnp.dot(a, w_ref[...].astype(jnp.bfloat16),
                         preferred_element_type=jnp.float32) + b_ref[...]


def _modulation(cond, w_mod, b_mod):
    tn = 1024
    n = w_mod.shape[1]
    return pl.pallas_call(
        _mod_kernel,
        grid=(n // tn,),
        in_specs=[pl.BlockSpec((MOD_ROWS, D), lambda j: (0, 0)),
                  pl.BlockSpec((D, tn), lambda j: (0, j)),
                  pl.BlockSpec((1, tn), lambda j: (0, j))],
        out_specs=pl.BlockSpec((MOD_ROWS, tn), lambda j: (0, j)),
        out_shape=jax.ShapeDtypeStruct((MOD_ROWS, n), jnp.float32),
        compiler_params=_cparams("arbitrary"),
        name="adaln_modulation",
    )(cond, w_mod, b_mod.reshape(1, n))


def _mod_spec(which, n_ctx_blocks):
    def idx(b, i):
        r = jnp.where(i < n_ctx_blocks, MOD_CTX_ROW, b)
        return (r * 6 + which, 0, 0)
    return pl.BlockSpec((None, 1, D), idx)


def _norm_mod_kernel(x_ref, g_ref, sh_ref, sc_ref, o_ref):
    h = _rms(x_ref[...], g_ref[...]) * (1.0 + sc_ref[...]) + sh_ref[...]
    o_ref[...] = h.astype(o_ref.dtype)


META_LANES = 128
META_E = 0
META_RANK = TOP_K
META_GATE = 2 * TOP_K
HALF_D = D // 2
HI_MASK = 0xFFFF0000


def _pack_bf16_pair(lo, hi):
    ulo = pltpu.bitcast(lo.astype(jnp.bfloat16).astype(jnp.float32), jnp.uint32)
    uhi = pltpu.bitcast(hi.astype(jnp.bfloat16).astype(jnp.float32), jnp.uint32)
    return lax.shift_right_logical(ulo, jnp.uint32(16)) | (uhi & jnp.uint32(HI_MASK))


def _unpack_bf16_pair(w):
    lo = pltpu.bitcast(lax.shift_left(w, jnp.uint32(16)), jnp.float32).astype(jnp.bfloat16)
    hi = pltpu.bitcast(w & jnp.uint32(HI_MASK), jnp.float32).astype(jnp.bfloat16)
    return lo, hi


def _norm_mod_router_kernel(x_ref, g_ref, sh_ref, sc_ref, wr_ref, br_ref, o_ref, meta_ref, cnt_ref, run_ref):
    first = jnp.logical_and(pl.program_id(0) == 0, pl.program_id(1) == 0)

    @pl.when(first)
    def _():
        run_ref[...] = jnp.zeros_like(run_ref)

    h = _rms(x_ref[...], g_ref[...]) * (1.0 + sc_ref[...]) + sh_ref[...]
    o_ref[...] = _pack_bf16_pair(h[:, :HALF_D], h[:, HALF_D:])
    logits = jnp.dot(h, wr_ref[...], preferred_element_type=jnp.float32,
                     precision=lax.Precision.HIGHEST) + br_ref[...]
    lane = lax.broadcasted_iota(jnp.int32, (ROW_TILE, N_EXPERTS), 1)
    vals, hots = [], []
    l = logits
    for _ in range(TOP_K):
        m = jnp.max(l, axis=-1, keepdims=True)
        idx = jnp.min(jnp.where(l == m, lane, N_EXPERTS), axis=-1, keepdims=True)
        hot = lane == idx
        vals.append(m)
        hots.append(hot)
        l = jnp.where(hot, -jnp.inf, l)
    es = [jnp.exp(v - vals[0]) for v in vals]
    den = es[0] + es[1] + es[2] + es[3]
    onehot = sum(hot.astype(jnp.float32) for hot in hots)
    r_i = lax.broadcasted_iota(jnp.int32, (ROW_TILE, ROW_TILE), 0)
    c_i = lax.broadcasted_iota(jnp.int32, (ROW_TILE, ROW_TILE), 1)
    lower = jnp.where(r_i > c_i, 1.0, 0.0).astype(jnp.bfloat16)
    before = jnp.dot(lower, onehot.astype(jnp.bfloat16), preferred_element_type=jnp.float32) + run_ref[0:1, 0:N_EXPERTS]
    mlane = lax.broadcasted_iota(jnp.int32, (ROW_TILE, META_LANES), 1)
    lane_f = lane.astype(jnp.float32)
    meta = jnp.zeros((ROW_TILE, META_LANES), jnp.float32)
    for k in range(TOP_K):
        hot_f = hots[k].astype(jnp.float32)
        e_k = jnp.sum(hot_f * lane_f, axis=-1, keepdims=True)
        rank_k = jnp.sum(hot_f * before, axis=-1, keepdims=True)
        meta = jnp.where(mlane == META_E + k, e_k, meta)
        meta = jnp.where(mlane == META_RANK + k, rank_k, meta)
        meta = jnp.where(mlane == META_GATE + k, es[k] / den, meta)
    meta_ref[...] = meta
    total = run_ref[0:1, 0:N_EXPERTS] + jnp.sum(onehot, axis=0, keepdims=True)
    run_ref[0:1, 0:N_EXPERTS] = total
    cnt_ref[...] = jnp.broadcast_to(run_ref[0:1, :], cnt_ref.shape)


def _norm_mod(x, g, mod, shift_idx, scale_idx, n_ctx_blocks, router=None, skip_rows=0):
    t = x.shape[1] - skip_rows
    skip = skip_rows // ROW_TILE
    grid = (B, t // ROW_TILE)
    row = pl.BlockSpec((None, ROW_TILE, D), lambda b, i: (b, i, 0))
    in_specs = [pl.BlockSpec((None, ROW_TILE, D), lambda b, i: (b, i + skip, 0)),
                pl.BlockSpec((1, D), lambda b, i: (0, 0)),
                _mod_spec(shift_idx, n_ctx_blocks), _mod_spec(scale_idx, n_ctx_blocks)]
    args = [x, g.reshape(1, D), mod, mod]
    if router is None:
        return pl.pallas_call(
            _norm_mod_kernel, grid=grid, in_specs=in_specs, out_specs=row,
            out_shape=jax.ShapeDtypeStruct((B, t, D), jnp.bfloat16),
            compiler_params=_cparams("parallel", "parallel"), name="norm_mod",
        )(*args)
    w_router, b_router = router
    in_specs += [pl.BlockSpec((D, N_EXPERTS), lambda b, i: (0, 0)),
                 pl.BlockSpec((1, N_EXPERTS), lambda b, i: (0, 0))]
    args += [w_router, b_router.reshape(1, N_EXPERTS)]
    return pl.pallas_call(
        _norm_mod_router_kernel, grid=grid, in_specs=in_specs,
        out_specs=[pl.BlockSpec((None, ROW_TILE, HALF_D), lambda b, i: (b, i, 0)),
                   pl.BlockSpec((None, ROW_TILE, META_LANES), lambda b, i: (b, i, 0)),
                   pl.BlockSpec((8, META_LANES), lambda b, i: (0, 0))],
        out_shape=[jax.ShapeDtypeStruct((B, t, HALF_D), jnp.uint32),
                   jax.ShapeDtypeStruct((B, t, META_LANES), jnp.float32),
                   jax.ShapeDtypeStruct((8, META_LANES), jnp.float32)],
        scratch_shapes=[pltpu.VMEM((8, META_LANES), jnp.float32)],
        compiler_params=_cparams("arbitrary", "arbitrary"), name="norm_mod_router",
    )(*args)


def _post_norm_kernel(x_ref, y_ref, g_ref, gt_ref, o_ref):
    o_ref[...] = x_ref[...] + gt_ref[...] * _rms(y_ref[...], g_ref[...])


def _post_norm_residual(x, y, g, mod, gate_idx, n_ctx_blocks):
    t = y.shape[1]
    skip = (x.shape[1] - t) // ROW_TILE
    row = pl.BlockSpec((None, ROW_TILE, D), lambda b, i: (b, i, 0))
    xrow = pl.BlockSpec((None, ROW_TILE, D), lambda b, i: (b, i + skip, 0))
    return pl.pallas_call(
        _post_norm_kernel, grid=(B, t // ROW_TILE),
        in_specs=[xrow, row, pl.BlockSpec((1, D), lambda b, i: (0, 0)),
                  _mod_spec(gate_idx, n_ctx_blocks)],
        out_specs=xrow,
        out_shape=jax.ShapeDtypeStruct(x.shape, jnp.float32),
        input_output_aliases={0: 0},
        compiler_params=_cparams("parallel", "parallel"), name="post_norm_residual",
    )(x, y, g.reshape(1, D), mod)


def _matmul_kernel(a_ref, w_ref, o_ref):
    o_ref[...] = jnp.dot(a_ref[...], w_ref[...].astype(jnp.bfloat16),
                         preferred_element_type=jnp.float32).astype(o_ref.dtype)


def _matmul(a, w, tm, tn, out_dtype=jnp.float32, name="matmul"):
    m, k = a.shape
    n = w.shape[1]
    return pl.pallas_call(
        _matmul_kernel, grid=(m // tm, n // tn),
        in_specs=[pl.BlockSpec((tm, k), lambda i, j: (i, 0)),
                  pl.BlockSpec((k, tn), lambda i, j: (0, j))],
        out_specs=pl.BlockSpec((tm, tn), lambda i, j: (i, j)),
        out_shape=jax.ShapeDtypeStruct((m, n), out_dtype),
        compiler_params=_cparams("parallel", "arbitrary"), name=name,
    )(a, w)


def _glu_matmul_kernel(a_ref, wv_ref, wg_ref, o_ref):
    a = a_ref[...]
    val = jnp.dot(a, wv_ref[...].astype(jnp.bfloat16), preferred_element_type=jnp.float32)
    gate = jnp.dot(a, wg_ref[...].astype(jnp.bfloat16), preferred_element_type=jnp.float32)
    o_ref[...] = val * _sigmoid(gate)


def _glu_matmul(a, w_glu, tm, tn):
    m, k = a.shape
    n = w_glu.shape[1] // 2
    nj = n // tn
    return pl.pallas_call(
        _glu_matmul_kernel, grid=(m // tm, nj),
        in_specs=[pl.BlockSpec((tm, k), lambda i, j: (i, 0)),
                  pl.BlockSpec((k, tn), lambda i, j: (0, j)),
                  pl.BlockSpec((k, tn), lambda i, j: (0, nj + j))],
        out_specs=pl.BlockSpec((tm, tn), lambda i, j: (i, j)),
        out_shape=jax.ShapeDtypeStruct((m, n), jnp.float32),
        compiler_params=_cparams("parallel", "arbitrary"), name="glu_matmul",
    )(a, w_glu, w_glu)


IN_TN = 256
IN_ROPE_END = 8
IN_DV_END = 12
IN_GQ_END = 16
IN_GK_TILE = 16


def _inproj_kernel(a_ref, w_ref, cos_ref, sin_ref, gq_ref, gk_ref, o_ref):
    j = pl.program_id(1)
    acc = jnp.dot(a_ref[...], w_ref[...].astype(jnp.bfloat16), preferred_element_type=jnp.float32)

    def rope(x):
        return x * cos_ref[...] + pltpu.roll(x, HD // 2, 1) * sin_ref[...]

    def store(fn):
        for c in range(IN_TN // HD):
            o_ref[:, c * HD:(c + 1) * HD] = fn(acc[:, c * HD:(c + 1) * HD]).astype(o_ref.dtype)

    @pl.when(j < IN_ROPE_END)
    def _():
        store(rope)

    @pl.when(jnp.logical_or(jnp.logical_and(j >= IN_ROPE_END, j < IN_DV_END), j > IN_GK_TILE))
    def _():
        store(lambda x: x)

    @pl.when(jnp.logical_and(j >= IN_DV_END, j < IN_GQ_END))
    def _():
        store(lambda x: rope(_rms(x, gq_ref[...])))

    @pl.when(j == IN_GK_TILE)
    def _():
        store(lambda x: rope(_rms(x, gk_ref[...])))


def _inproj(h, w_in, cos2, sin2, g_q, g_k):
    m = h.shape[0]
    const = lambda i, j: (0, 0)
    return pl.pallas_call(
        _inproj_kernel, grid=(m // TOK, ATTN_IN // IN_TN),
        in_specs=[pl.BlockSpec((TOK, D), lambda i, j: (i, 0)),
                  pl.BlockSpec((D, IN_TN), lambda i, j: (0, j)),
                  pl.BlockSpec((TOK, HD), const), pl.BlockSpec((TOK, HD), const),
                  pl.BlockSpec((1, HD), const), pl.BlockSpec((1, HD), const)],
        out_specs=pl.BlockSpec((TOK, IN_TN), lambda i, j: (i, j)),
        out_shape=jax.ShapeDtypeStruct((m, ATTN_IN), jnp.bfloat16),
        compiler_params=_cparams("parallel", "arbitrary"), name="attn_inproj",
    )(h, w_in, cos2, sin2, g_q.reshape(1, HD), g_k.reshape(1, HD))


ATT_TQ = 256
ATT_SCALE = HD ** -0.5


def _softmax_pv(q, k, v):
    s = lax.dot_general(q, k, (((1,), (1,)), ((), ())), preferred_element_type=jnp.float32)
    m = jnp.max(s, axis=-1, keepdims=True)
    e = jnp.exp((s - m) * ATT_SCALE)
    l = jnp.sum(e, axis=-1, keepdims=True)
    return jnp.dot(e.astype(jnp.bfloat16), v, preferred_element_type=jnp.float32), l


def _diff_attn_kernel(lam_ref, q_ref, k_ref, v_ref, g_ref, o_ref, *, lambda_init):
    lp = lam_ref[...]
    lam = (jnp.exp(jnp.sum(lp[0:1] * lp[1:2], axis=-1, keepdims=True))
           - jnp.exp(jnp.sum(lp[2:3] * lp[3:4], axis=-1, keepdims=True)) + lambda_init)

    def run(nk):
        q = q_ref[...]
        k = k_ref[0:nk, :]
        v = v_ref[0:nk, :]
        pv1, l1 = _softmax_pv(q[:, :HD], k[:, :HD], v)
        pv2, l2 = _softmax_pv(q[:, HD:], k[:, HD:], v)
        o = pv1 / l1 - lam * (pv2 / l2)
        o_ref[...] = (_rms(o, g_ref[...]) * (1.0 - lambda_init)).astype(o_ref.dtype)

    @pl.when(pl.program_id(2) == 0)
    def _():
        run(CTX)

    @pl.when(pl.program_id(2) > 0)
    def _():
        run(TOK)


def _gqa_attn_kernel(q_ref, k_ref, v_ref, o_ref):
    def run(nk):
        k = k_ref[0:nk, :]
        v = v_ref[0:nk, :]
        for g in range(GQA_GROUP):
            pv, l = _softmax_pv(q_ref[:, g * HD:(g + 1) * HD], k, v)
            o_ref[:, g * HD:(g + 1) * HD] = (pv / l).astype(o_ref.dtype)

    @pl.when(pl.program_id(2) == 0)
    def _():
        run(CTX)

    @pl.when(pl.program_id(2) > 0)
    def _():
        run(TOK)


def _attention(p, lam_params, g_subln, lambda_init):
    nq = TOK // ATT_TQ
    dv = 2 * HD
    od = pl.pallas_call(
        functools.partial(_diff_attn_kernel, lambda_init=lambda_init),
        grid=(B, DIFF_HEADS, nq),
        in_specs=[pl.BlockSpec((4, HD), lambda b, h, i: (0, 0)),
                  pl.BlockSpec((None, ATT_TQ, dv), lambda b, h, i: (b, i, h)),
                  pl.BlockSpec((None, TOK, dv), lambda b, h, i: (b, 0, DIFF_HEADS + h)),
                  pl.BlockSpec((None, TOK, dv), lambda b, h, i: (b, 0, 2 * DIFF_HEADS + h)),
                  pl.BlockSpec((1, dv), lambda b, h, i: (0, 0))],
        out_specs=pl.BlockSpec((None, ATT_TQ, dv), lambda b, h, i: (b, i, h)),
        out_shape=jax.ShapeDtypeStruct((B, TOK, DIFF_HEADS * dv), jnp.bfloat16),
        compiler_params=_cparams("parallel", "parallel", "arbitrary"), name="diff_attention",
    )(lam_params, p, p, p, g_subln.reshape(1, dv))
    gq_w = GQA_GROUP * HD
    gq0 = 3072 // gq_w
    gk0 = 4096 // HD
    gv0 = 4352 // HD
    og = pl.pallas_call(
        _gqa_attn_kernel,
        grid=(B, GQA_KV_HEADS, nq),
        in_specs=[pl.BlockSpec((None, ATT_TQ, gq_w), lambda b, n, i: (b, i, gq0 + n)),
                  pl.BlockSpec((None, TOK, HD), lambda b, n, i: (b, 0, gk0 + n)),
                  pl.BlockSpec((None, TOK, HD), lambda b, n, i: (b, 0, gv0 + n))],
        out_specs=pl.BlockSpec((None, ATT_TQ, gq_w), lambda b, n, i: (b, i, n)),
        out_shape=jax.ShapeDtypeStruct((B, TOK, GQA_Q_HEADS * HD), jnp.bfloat16),
        compiler_params=_cparams("parallel", "parallel", "arbitrary"), name="gqa_attention",
    )(p, p, p)
    return jnp.concatenate([od, og], axis=-1)


def _rope_tables():
    rows = SEQ // GRID_W
    row_id, col_id = jnp.meshgrid(jnp.arange(rows), jnp.arange(GRID_W), indexing="ij")
    inv_freq = ROPE_THETA ** (-jnp.arange(ROPE_FREQS, dtype=jnp.float32) / ROPE_FREQS)
    ang = jnp.concatenate([row_id.reshape(-1, 1) * inv_freq, col_id.reshape(-1, 1) * inv_freq], axis=-1)
    cos, sin = jnp.cos(ang), jnp.sin(ang)
    cos2 = jnp.concatenate([cos, cos], axis=-1)
    sin2 = jnp.concatenate([-sin, sin], axis=-1)
    cos2 = jnp.concatenate([jnp.ones((CTX, HD), jnp.float32), cos2], axis=0)
    sin2 = jnp.concatenate([jnp.zeros((CTX, HD), jnp.float32), sin2], axis=0)
    return cos2, sin2


MOE_TM = 512
MOE_SUB = 256
MOE_TF = 1024
DISPATCH_TT = 256
COMBINE_TT = 128


def _for_valid_rows(rows, compute, o_ref):
    half = MOE_SUB // 2
    width = o_ref.shape[1]
    for s in range(MOE_TM // MOE_SUB):
        base = s * MOE_SUB

        @pl.when(rows > base + half)
        def _():
            compute(slice(base, base + MOE_SUB))

        @pl.when(jnp.logical_and(rows > base, rows <= base + half))
        def _():
            compute(slice(base, base + half))
            o_ref[base + half:base + MOE_SUB, :] = jnp.zeros((half, width), o_ref.dtype)

        @pl.when(rows <= base)
        def _():
            o_ref[base:base + MOE_SUB, :] = jnp.zeros((MOE_SUB, width), o_ref.dtype)


def _mxu_dot(a_bf16, w_f32):
    return lax.dot_general(a_bf16, w_f32, (((1,), (0,)), ((), ())), preferred_element_type=jnp.float32)


def _moe_up_kernel(be_ref, first_ref, rows_ref, nxt_ref, x_ref, w_hbm, bg_ref, bl_ref, o_ref,
                   wst, slot_ref, sem):
    f = pl.program_id(0)
    b = pl.program_id(1)
    nf = pl.num_programs(0)

    def copies(e, ff, slot):
        col = pl.multiple_of(ff * MOE_TF, MOE_TF)
        return [pltpu.make_async_copy(w_hbm.at[e, :, pl.ds(part * D_FF + col, MOE_TF)],
                                      wst.at[slot, part], sem.at[slot, part]) for part in range(2)]

    @pl.when(first_ref[b] == 1)
    def _():
        @pl.when(jnp.logical_and(f == 0, b == 0))
        def _():
            slot_ref[0] = 1
            for c in copies(be_ref[0], 0, 0):
                c.start()

        slot = 1 - slot_ref[0]
        slot_ref[0] = slot
        for c in copies(be_ref[b], f, slot):
            c.wait()
        e_next = nxt_ref[b]

        @pl.when(e_next >= 0)
        def _():
            for c in copies(e_next, f, 1 - slot):
                c.start()

        @pl.when(jnp.logical_and(e_next < 0, f + 1 < nf))
        def _():
            for c in copies(be_ref[0], f + 1, 1 - slot):
                c.start()

    def compute(sl):
        slot = slot_ref[0]
        lo, hi = _unpack_bf16_pair(x_ref[sl, :])
        glu = (_mxu_dot(lo, wst[slot, 0, :HALF_D, :]) + _mxu_dot(hi, wst[slot, 0, HALF_D:, :]) + bg_ref[...])
        lin = (_mxu_dot(lo, wst[slot, 1, :HALF_D, :]) + _mxu_dot(hi, wst[slot, 1, HALF_D:, :]) + bl_ref[...])
        glu = jnp.minimum(glu, SWIGLU_LIMIT)
        lin = jnp.clip(lin, -SWIGLU_LIMIT, SWIGLU_LIMIT)
        o_ref[sl, :] = (glu * _sigmoid(SWIGLU_ALPHA * glu) * (lin + 1.0)).astype(o_ref.dtype)

    _for_valid_rows(rows_ref[b], compute, o_ref)


def _moe_down_kernel(be_ref, first_ref, rows_ref, nxt_ref, a_ref, w_hbm, bias_ref, o_ref, wst, slot_ref, sem):
    b = pl.program_id(0)

    def copy(e, slot):
        return pltpu.make_async_copy(w_hbm.at[e], wst.at[slot], sem.at[slot])

    @pl.when(first_ref[b] == 1)
    def _():
        @pl.when(b == 0)
        def _():
            slot_ref[0] = 1
            copy(be_ref[0], 0).start()

        slot = 1 - slot_ref[0]
        slot_ref[0] = slot
        copy(be_ref[b], slot).wait()
        e_next = nxt_ref[b]

        @pl.when(e_next >= 0)
        def _():
            copy(e_next, 1 - slot).start()

    def compute(sl):
        o_ref[sl, :] = _mxu_dot(a_ref[sl, :], wst[slot_ref[0]]) + bias_ref[...]

    _for_valid_rows(rows_ref[b], compute, o_ref)


def _moe_experts(x_sorted, tables, w_gate_up, b_gate_up, w_down, b_down):
    r = x_sorted.shape[0]
    nb = r // MOE_TM
    nf = D_FF // MOE_TF
    bgu = b_gate_up.reshape(N_EXPERTS, 1, 2 * D_FF)
    act = pl.pallas_call(
        _moe_up_kernel,
        grid_spec=pltpu.PrefetchScalarGridSpec(
            num_scalar_prefetch=4, grid=(nf, nb),
            in_specs=[pl.BlockSpec((MOE_TM, HALF_D), lambda f, b, be, fi, ro, nx: (b, 0)),
                      pl.BlockSpec(memory_space=pl.ANY),
                      pl.BlockSpec((None, 1, MOE_TF), lambda f, b, be, fi, ro, nx: (be[b], 0, f)),
                      pl.BlockSpec((None, 1, MOE_TF), lambda f, b, be, fi, ro, nx: (be[b], 0, nf + f))],
            out_specs=pl.BlockSpec((MOE_TM, MOE_TF), lambda f, b, be, fi, ro, nx: (b, f)),
            scratch_shapes=[pltpu.VMEM((2, 2, D, MOE_TF), jnp.float32), pltpu.SMEM((1,), jnp.int32),
                            pltpu.SemaphoreType.DMA((2, 2))]),
        out_shape=jax.ShapeDtypeStruct((r, D_FF), jnp.bfloat16),
        compiler_params=_cparams("arbitrary", "arbitrary"), name="moe_gate_up",
    )(*tables, x_sorted, w_gate_up, bgu, bgu)
    return pl.pallas_call(
        _moe_down_kernel,
        grid_spec=pltpu.PrefetchScalarGridSpec(
            num_scalar_prefetch=4, grid=(nb,),
            in_specs=[pl.BlockSpec((MOE_TM, D_FF), lambda b, be, fi, ro, nx: (b, 0)),
                      pl.BlockSpec(memory_space=pl.ANY),
                      pl.BlockSpec((None, 1, D), lambda b, be, fi, ro, nx: (be[b], 0, 0))],
            out_specs=pl.BlockSpec((MOE_TM, D), lambda b, be, fi, ro, nx: (b, 0)),
            scratch_shapes=[pltpu.VMEM((2, D_FF, D), jnp.float32), pltpu.SMEM((1,), jnp.int32),
                            pltpu.SemaphoreType.DMA((2,))]),
        out_shape=jax.ShapeDtypeStruct((r, D), jnp.float32),
        compiler_params=_cparams("arbitrary"), name="moe_down",
    )(*tables, act, w_down, b_down.reshape(N_EXPERTS, 1, D))


def _dispatch_kernel(dest_ref, hp_ref, xs_in_hbm, xs_hbm, idx_smem, sem_idx, sem_rows):
    del xs_in_hbm
    idx_copy = pltpu.make_async_copy(dest_ref, idx_smem, sem_idx)
    idx_copy.start()
    idx_copy.wait()

    def issue(t, carry):
        for k in range(TOP_K):
            d = idx_smem[0, 0, t * TOP_K + k]
            pltpu.make_async_copy(hp_ref.at[t], xs_hbm.at[d], sem_rows).start(priority=k % 2)
        return carry

    lax.fori_loop(0, DISPATCH_TT, issue, 0, unroll=4)
    for k in range(TOP_K):
        pltpu.make_async_copy(hp_ref, xs_hbm.at[pl.ds(0, DISPATCH_TT)], sem_rows).wait()


def _dispatch(dest, hp, n_rows):
    n_tok = hp.shape[0]
    n = n_tok // DISPATCH_TT
    width = DISPATCH_TT * TOP_K
    zeros = jnp.zeros((n_rows, HALF_D), jnp.uint32)
    return pl.pallas_call(
        _dispatch_kernel, grid=(n,),
        in_specs=[pl.BlockSpec((1, 1, width), lambda i: (i, 0, 0)),
                  pl.BlockSpec((DISPATCH_TT, HALF_D), lambda i: (i, 0)), pl.BlockSpec(memory_space=pl.ANY)],
        out_specs=pl.BlockSpec(memory_space=pl.ANY),
        out_shape=jax.ShapeDtypeStruct((n_rows, HALF_D), jnp.uint32),
        scratch_shapes=[pltpu.SMEM((1, 1, width), jnp.int32), pltpu.SemaphoreType.DMA, pltpu.SemaphoreType.DMA],
        input_output_aliases={2: 0},
        compiler_params=_cparams("arbitrary"), name="moe_dispatch",
    )(dest.reshape(n, 1, width), hp, zeros)


def _combine_kernel(dcur_ref, dnext_ref, out_hbm, meta_ref, x_ref, g_ref, gt_ref, o_ref,
                    idx_smem, buf, sem_idx, sem_rows):
    i = pl.program_id(0)
    n = pl.num_programs(0)

    def row_copy(d, slot, k, t):
        return pltpu.make_async_copy(out_hbm.at[d], buf.at[slot, k, t], sem_rows.at[slot])

    def gather(d_ref, slot):
        idx_copy = pltpu.make_async_copy(d_ref, idx_smem, sem_idx)
        idx_copy.start()
        idx_copy.wait()

        def issue(t, carry):
            for k in range(TOP_K):
                row_copy(idx_smem[0, 0, t * TOP_K + k], slot, k, t).start(priority=k % 2)
            return carry
        lax.fori_loop(0, COMBINE_TT, issue, 0, unroll=4)

    @pl.when(i == 0)
    def _():
        gather(dcur_ref, 0)

    @pl.when(i + 1 < n)
    def _():
        gather(dnext_ref, (i + 1) % 2)

    slot = i % 2
    pltpu.make_async_copy(buf.at[slot], buf.at[slot], sem_rows.at[slot]).wait()

    meta = meta_ref[...]
    f = meta[:, META_GATE:META_GATE + 1] * buf[slot, 0]
    for k in range(1, TOP_K):
        f = f + meta[:, META_GATE + k:META_GATE + k + 1] * buf[slot, k]
    o_ref[...] = x_ref[...] + gt_ref[...] * _rms(f, g_ref[...])


def _combine(dest, out_sorted, meta, x, g, mod, gate_idx, rows_per_batch, ctx_rows):
    n_tok = dest.shape[0]
    n = n_tok // COMBINE_TT
    width = COMBINE_TT * TOP_K
    per_batch = rows_per_batch // COMBINE_TT
    x_per_batch = x.shape[0] // B // COMBINE_TT
    skip = x_per_batch - per_batch
    ctx_blocks = ctx_rows // COMBINE_TT

    def gate_row(i):
        r = jnp.where(i % per_batch < ctx_blocks, MOD_CTX_ROW, i // per_batch)
        return (r * 6 + gate_idx, 0, 0)

    row = pl.BlockSpec((COMBINE_TT, D), lambda i: (i, 0))
    xrow = pl.BlockSpec((COMBINE_TT, D), lambda i: ((i // per_batch) * x_per_batch + skip + i % per_batch, 0))
    return pl.pallas_call(
        _combine_kernel, grid=(n,),
        in_specs=[pl.BlockSpec((1, 1, width), lambda i: (i, 0, 0)),
                  pl.BlockSpec((1, 1, width), lambda i: (jnp.minimum(i + 1, n - 1), 0, 0)),
                  pl.BlockSpec(memory_space=pl.ANY),
                  pl.BlockSpec((COMBINE_TT, META_LANES), lambda i: (i, 0)),
                  xrow, pl.BlockSpec((1, D), lambda i: (0, 0)),
                  pl.BlockSpec((None, 1, D), gate_row)],
        out_specs=row,
        out_shape=jax.ShapeDtypeStruct((n_tok, D), jnp.float32),
        scratch_shapes=[pltpu.SMEM((1, 1, width), jnp.int32),
                        pltpu.VMEM((2, TOP_K, COMBINE_TT, D), jnp.float32),
                        pltpu.SemaphoreType.DMA, pltpu.SemaphoreType.DMA((2,))],
        input_output_aliases={4: 0} if skip == 0 else {},
        compiler_params=_cparams("arbitrary"), name="moe_combine",
    )(dest.reshape(n, 1, width), dest.reshape(n, 1, width), out_sorted, meta, x, g.reshape(1, D), mod)


def _moe_block_tables(counts, n_assign):
    padded = (counts + MOE_TM - 1) // MOE_TM * MOE_TM
    pad_end = jnp.cumsum(padded)
    pad_start = pad_end - padded
    nb = -(-(n_assign + N_EXPERTS * (MOE_TM - 1)) // MOE_TM)
    block_start = jnp.arange(nb, dtype=jnp.int32) * MOE_TM
    block_e_raw = jnp.minimum(jnp.sum(block_start[:, None] >= pad_end[None, :], axis=1), N_EXPERTS - 1)
    block_e_raw = block_e_raw.astype(jnp.int32)
    onehot_e = block_e_raw[:, None] == jnp.arange(N_EXPERTS)[None, :]
    valid_end = jnp.sum(jnp.where(onehot_e, (pad_start + counts)[None, :], 0), axis=1)
    block_rows = jnp.clip(valid_end - block_start, 0, MOE_TM).astype(jnp.int32)
    block_rows = jnp.where(block_start < pad_end[-1], block_rows, 0)
    last_e = jnp.max(jnp.where(block_rows > 0, block_e_raw, 0))
    block_e = jnp.where(block_rows > 0, block_e_raw, last_e)
    block_first = jnp.concatenate([jnp.ones((1,), jnp.int32),
                                   (block_e[1:] != block_e[:-1]).astype(jnp.int32)])
    idx = jnp.arange(nb, dtype=jnp.int32)
    first_pos = jnp.where(block_first == 1, idx, nb)
    later = jnp.where(idx[None, :] > idx[:, None], first_pos[None, :], nb)
    next_pos = jnp.min(later, axis=1)
    next_e = jnp.sum(jnp.where(idx[None, :] == next_pos[:, None], block_e[None, :], 0), axis=1)
    block_next = jnp.where(next_pos < nb, next_e, -1).astype(jnp.int32)
    return pad_start, nb * MOE_TM, (block_e, block_first, block_rows, block_next)


def _moe_ffn(x_res, hp, meta, cnt, g_post, mod, gate_idx, rows_per_batch, ctx_rows,
             w_gate_up, b_gate_up, w_down, b_down):
    n_tok = hp.shape[0]
    counts = cnt[0, :N_EXPERTS].astype(jnp.int32)
    pad_start, n_rows, tables = _moe_block_tables(counts, n_tok * TOP_K)
    top_e = meta[:, META_E:META_E + TOP_K].astype(jnp.int32)
    rank = meta[:, META_RANK:META_RANK + TOP_K].astype(jnp.int32)
    hot = top_e[:, :, None] == jnp.arange(N_EXPERTS)[None, None, :]
    dest = jnp.sum(jnp.where(hot, pad_start[None, None, :], 0), axis=-1) + rank
    x_sorted = _dispatch(dest, hp, n_rows)
    out_sorted = _moe_experts(x_sorted, tables, w_gate_up, b_gate_up, w_down, b_down)
    return _combine(dest, out_sorted, meta, x_res, g_post, mod, gate_idx, rows_per_batch, ctx_rows)


S5_GB = 8
S5_LANES = SSM_CHUNK * SSM_GROUP
S5_NK_CTX = CTX // SSM_CHUNK
S5_NK_LAT = SEQ // SSM_CHUNK
S5_PAIR = 2 * B


def _s5_direction(reverse, uc_ref, ul_ref, wb_ref, m_ref, wc_ref, coef_ref, y_ref,
                  s1c, s2c, s1l, s2l, xin, accumulate):
    half = 2 * SSM_STATE
    for g in range(S5_GB):
        sc = jnp.dot(uc_ref[g], wb_ref[g], preferred_element_type=jnp.float32)
        s1c[g] = sc[:, :half]
        s2c[g] = sc[:, half:]
        sl = jnp.dot(ul_ref[g], wb_ref[g], preferred_element_type=jnp.float32)
        s1l[g] = sl[:, :half]
        s2l[g] = sl[:, half:]

    lower = lax.broadcasted_iota(jnp.int32, (S5_PAIR, half), 0) < B
    p1 = [jnp.broadcast_to(coef_ref[g, 0:1, :], (S5_PAIR, half)) for g in range(S5_GB)]
    p2 = [jnp.broadcast_to(coef_ref[g, 1:2, :], (S5_PAIR, half)) for g in range(S5_GB)]

    def tile_step(g, t1, t2, v1, v2):
        y1a = p1[g] * v1 + p2[g] * v2 + t1
        y1b = p1[g] * v2 - p2[g] * v1 + t2
        r1a = pltpu.roll(y1a, B, 0)
        r1b = pltpu.roll(y1b, B, 0)
        y2a = p1[g] * r1a + p2[g] * r1b + t1
        y2b = p1[g] * r1b - p2[g] * r1a + t2
        r2a = pltpu.roll(y2a, B, 0)
        r2b = pltpu.roll(y2b, B, 0)
        if not reverse:
            x_in = jnp.where(lower, v1, r1a)
            return x_in, jnp.where(lower, r2a, y2a), jnp.where(lower, r2b, y2b)
        x_in = jnp.where(lower, r1a, v1)
        return x_in, jnp.where(lower, y2a, r2a), jnp.where(lower, y2b, r2b)

    def scan(s1, s2, n_tiles, state, record):
        def body(j, carry):
            jj = (n_tiles - 1 - j) if reverse else j
            r0 = pl.multiple_of(jj * S5_PAIR, S5_PAIR)
            new = []
            for g in range(S5_GB):
                v1, v2 = carry[2 * g], carry[2 * g + 1]
                x_in, v1, v2 = tile_step(g, s1[g, pl.ds(r0, S5_PAIR), :], s2[g, pl.ds(r0, S5_PAIR), :], v1, v2)
                if record:
                    xin[g, pl.ds(r0, S5_PAIR), :] = x_in
                new += [v1, v2]
            return tuple(new)
        return lax.fori_loop(0, n_tiles, body, state)

    zero = jnp.zeros((S5_PAIR, half), jnp.float32)
    state = tuple(zero for _ in range(2 * S5_GB))
    state = scan(s1c, s2c, S5_NK_CTX * B // S5_PAIR, state, False)
    scan(s1l, s2l, S5_NK_LAT * B // S5_PAIR, state, True)

    for g in range(S5_GB):
        y = (jnp.dot(ul_ref[g], m_ref[g], preferred_element_type=jnp.float32)
             + jnp.dot(xin[g].astype(jnp.bfloat16), wc_ref[g], preferred_element_type=jnp.float32))
        if accumulate:
            y_ref[g] = y_ref[g] + y
        else:
            y_ref[g] = y


def _s5_kernel(uc_ref, ul_ref, wb_ref, m_ref, wc_ref, coef_ref, y_ref, s1c, s2c, s1l, s2l, xin):
    args = (uc_ref, ul_ref, wb_ref, m_ref, wc_ref, coef_ref, y_ref, s1c, s2c, s1l, s2l, xin)

    @pl.when(pl.program_id(1) == 0)
    def _():
        _s5_direction(False, *args, accumulate=False)

    @pl.when(pl.program_id(1) == 1)
    def _():
        _s5_direction(True, *args, accumulate=True)


def _s5_scan(u_ctx, u_lat, wb, m, wc, coef):
    rc, rl = u_ctx.shape[1], u_lat.shape[1]
    half = 2 * SSM_STATE
    wspec = lambda k, n: pl.BlockSpec((None, S5_GB, k, n), lambda gi, d: (d, gi, 0, 0))
    return pl.pallas_call(
        _s5_kernel, grid=(SSM_GROUPS // S5_GB, 2),
        in_specs=[pl.BlockSpec((S5_GB, rc, S5_LANES), lambda gi, d: (gi, 0, 0)),
                  pl.BlockSpec((S5_GB, rl, S5_LANES), lambda gi, d: (gi, 0, 0)),
                  wspec(S5_LANES, 2 * half), wspec(S5_LANES, S5_LANES), wspec(half, S5_LANES),
                  wspec(2, half)],
        out_specs=pl.BlockSpec((S5_GB, rl, S5_LANES), lambda gi, d: (gi, 0, 0)),
        out_shape=jax.ShapeDtypeStruct((SSM_GROUPS, rl, S5_LANES), jnp.float32),
        scratch_shapes=[pltpu.VMEM((S5_GB, rc, half), jnp.float32), pltpu.VMEM((S5_GB, rc, half), jnp.float32),
                        pltpu.VMEM((S5_GB, rl, half), jnp.float32), pltpu.VMEM((S5_GB, rl, half), jnp.float32),
                        pltpu.VMEM((S5_GB, rl, half), jnp.float32)],
        compiler_params=_cparams("parallel", "arbitrary"), name="s5_scan",
    )(u_ctx, u_lat, wb, m, wc, coef)


def _s5_matrices(a_re, a_im, b_re, b_im, c_re, c_im, log_dt, reverse):
    hp = lax.Precision.HIGHEST
    n = SSM_CHUNK
    g, p = SSM_GROUPS, SSM_STATE
    dt = jnp.exp(log_dt)[:, None]
    mag = jnp.exp(a_re * dt)
    ab_re, ab_im = mag * jnp.cos(a_im * dt), mag * jnp.sin(a_im * dt)
    den = a_re * a_re + a_im * a_im
    f_re = ((ab_re - 1.0) * a_re + ab_im * a_im) / den
    f_im = (ab_im * a_re - (ab_re - 1.0) * a_im) / den
    b_re_t, b_im_t = b_re.transpose(0, 2, 1), b_im.transpose(0, 2, 1)
    bb_re = f_re[:, None, :] * b_re_t - f_im[:, None, :] * b_im_t
    bb_im = f_re[:, None, :] * b_im_t + f_im[:, None, :] * b_re_t
    tau = jnp.arange(n + 1, dtype=jnp.float32)[None, None, :]
    ang_re, ang_im = (a_re * dt)[:, :, None], (a_im * dt)[:, :, None]
    pmag = jnp.exp(tau * ang_re)
    pw_re, pw_im = pmag * jnp.cos(tau * ang_im), pmag * jnp.sin(tau * ang_im)
    c_re_t, c_im_t = c_re.transpose(0, 2, 1), c_im.transpose(0, 2, 1)

    def c_times_power(exps):
        q_re, q_im = pw_re[:, :, exps], pw_im[:, :, exps]
        re = c_re_t[:, :, None, :] * q_re[:, :, :, None] - c_im_t[:, :, None, :] * q_im[:, :, :, None]
        im = c_re_t[:, :, None, :] * q_im[:, :, :, None] + c_im_t[:, :, None, :] * q_re[:, :, :, None]
        return re.reshape(g, p, -1), im.reshape(g, p, -1)

    lags = np.arange(n)[::-1] if reverse else np.arange(n)
    ca_re, ca_im = c_times_power(lags)
    kst = jnp.matmul(jnp.concatenate([bb_re, bb_im], axis=-1), jnp.concatenate([ca_re, -ca_im], axis=1),
                     precision=hp)
    zeros = jnp.zeros_like(kst)
    if reverse:
        z = jnp.concatenate([kst, zeros], axis=-1)
        m = jnp.stack([z[:, :, (n - 1 - s) * SSM_GROUP:(n - 1 - s) * SSM_GROUP + S5_LANES] for s in range(n)], axis=1)
    else:
        z = jnp.concatenate([zeros, kst], axis=-1)
        m = jnp.stack([z[:, :, S5_LANES - s * SSM_GROUP:2 * S5_LANES - s * SSM_GROUP] for s in range(n)], axis=1)
    m = m.reshape(g, S5_LANES, S5_LANES)
    e_idx = np.arange(n) if reverse else (n - 1 - np.arange(n))
    ae_re = pw_re[:, :, e_idx].transpose(0, 2, 1)[:, :, None, :]
    ae_im = pw_im[:, :, e_idx].transpose(0, 2, 1)[:, :, None, :]
    wb_re = (ae_re * bb_re[:, None] - ae_im * bb_im[:, None]).reshape(g, S5_LANES, p)
    wb_im = (ae_re * bb_im[:, None] + ae_im * bb_re[:, None]).reshape(g, S5_LANES, p)
    wb = jnp.concatenate([wb_re, wb_im, wb_im, wb_re], axis=-1)
    f_idx = (n - np.arange(n)) if reverse else (np.arange(n) + 1)
    cf_re, cf_im = c_times_power(f_idx)
    wc = jnp.concatenate([cf_re, -cf_im], axis=1)
    an_re, an_im = pw_re[:, :, n], pw_im[:, :, n]
    coef = jnp.stack([jnp.concatenate([an_re, an_re], axis=-1),
                      jnp.concatenate([-an_im, an_im], axis=-1)], axis=1)
    return wb.astype(jnp.bfloat16), m.astype(jnp.bfloat16), wc.astype(jnp.bfloat16), coef


def _s5_chunks(h):
    t = h.shape[1]
    nk = t // SSM_CHUNK
    u = h.reshape(B, nk, SSM_CHUNK, SSM_GROUPS, SSM_GROUP).transpose(3, 1, 0, 2, 4)
    return u.reshape(SSM_GROUPS, nk * B, S5_LANES)


def _s5_unchunk(y):
    nk = y.shape[1] // B
    y = y.reshape(SSM_GROUPS, nk, B, SSM_CHUNK, SSM_GROUP).transpose(2, 1, 3, 0, 4)
    return y.reshape(B, nk * SSM_CHUNK, D)


def _s5_post_kernel(x_ref, y_ref, g_ref, sh_ref, sc_ref, d_ref, o_ref):
    h = _rms(x_ref[...], g_ref[...]) * (1.0 + sc_ref[...]) + sh_ref[...]
    y = d_ref[...] * h + y_ref[...]
    z = 0.5 * y * (1.0 + jnp.tanh(math.sqrt(2.0 / math.pi) * (y + 0.044715 * (y * y * y))))
    o_ref[...] = z.astype(o_ref.dtype)


def _s5_post(xs, y_ssm, g, mod, d_skip):
    skip = CTX // ROW_TILE
    row = pl.BlockSpec((None, ROW_TILE, D), lambda b, i: (b, i, 0))
    xrow = pl.BlockSpec((None, ROW_TILE, D), lambda b, i: (b, i + skip, 0))
    vec = pl.BlockSpec((1, D), lambda b, i: (0, 0))
    return pl.pallas_call(
        _s5_post_kernel, grid=(B, SEQ // ROW_TILE),
        in_specs=[xrow, row, vec, _mod_spec(0, 0), _mod_spec(1, 0), vec],
        out_specs=row,
        out_shape=jax.ShapeDtypeStruct((B, SEQ, D), jnp.bfloat16),
        compiler_params=_cparams("parallel", "parallel"), name="s5_skip_gelu",
    )(xs, y_ssm, g.reshape(1, D), mod, mod, d_skip.reshape(1, D))


def _layer_modulation(c, c_ctx, w_mod, b_mod):
    cond = jnp.concatenate([c, c_ctx[None, :], jnp.zeros((MOD_ROWS - B - 1, D), jnp.float32)], axis=0)
    return _modulation(cond, w_mod, b_mod).reshape(MOD_ROWS * 6, 1, D)


def kernel(x, c, ctx, c_ctx, l0_w_mod, l0_b_mod, l0_g_pre_mix, l0_g_post_mix, l0_g_pre_ffn, l0_g_post_ffn, l0_w_in, l0_w_out, l0_lambda_q1, l0_lambda_k1, l0_lambda_q2, l0_lambda_k2, l0_g_subln, l0_g_qnorm, l0_g_knorm, l0_w_router, l0_b_router, l0_w_gate_up, l0_b_gate_up, l0_w_down, l0_b_down, l1_w_mod, l1_b_mod, l1_g_pre_mix, l1_g_post_mix, l1_g_pre_ffn, l1_g_post_ffn, l1_ssm_a_re, l1_ssm_a_im, l1_ssm_b_re, l1_ssm_b_im, l1_ssm_c_re, l1_ssm_c_im, l1_ssm_log_dt, l1_ssm_d, l1_w_glu, l1_w_router, l1_b_router, l1_w_gate_up, l1_b_gate_up, l1_w_down, l1_b_down):
    xs = jnp.concatenate([ctx, x], axis=1)

    mod = _layer_modulation(c, c_ctx, l0_w_mod, l0_b_mod)
    h = _norm_mod(xs, l0_g_pre_mix, mod, 0, 1, 1)
    cos2, sin2 = _rope_tables()
    p = _inproj(h.reshape(B * TOK, D), l0_w_in, cos2, sin2, l0_g_qnorm, l0_g_knorm)
    lam_params = jnp.stack([l0_lambda_q1, l0_lambda_k1, l0_lambda_q2, l0_lambda_k2])
    lambda_init = 0.8 - 0.6 * math.exp(-0.3 * 0)
    att = _attention(p.reshape(B, TOK, ATTN_IN), lam_params, l0_g_subln, lambda_init)
    y = _matmul(att.reshape(B * TOK, D), l0_w_out, 1024, 512, name="attn_outproj")
    xs = _post_norm_residual(xs, y.reshape(B, TOK, D), l0_g_post_mix, mod, 2, 1)
    hp, meta, cnt = _norm_mod(xs, l0_g_pre_ffn, mod, 3, 4, 1, router=(l0_w_router, l0_b_router))
    xs = _moe_ffn(xs.reshape(B * TOK, D), hp.reshape(B * TOK, HALF_D), meta.reshape(B * TOK, META_LANES), cnt,
                  l0_g_post_ffn, mod, 5, TOK, CTX,
                  l0_w_gate_up, l0_b_gate_up, l0_w_down, l0_b_down).reshape(B, TOK, D)

    mod = _layer_modulation(c, c_ctx, l1_w_mod, l1_b_mod)
    h = _norm_mod(xs, l1_g_pre_mix, mod, 0, 1, 1)
    u_ctx = _s5_chunks(h[:, :CTX])
    u_lat = _s5_chunks(h[:, CTX:])
    mats = [_s5_matrices(l1_ssm_a_re[d], l1_ssm_a_im[d], l1_ssm_b_re[d], l1_ssm_b_im[d],
                         l1_ssm_c_re[d], l1_ssm_c_im[d], l1_ssm_log_dt[d], reverse=bool(d))
            for d in range(2)]
    wb, m, wc, coef = (jnp.stack([mats[0][i], mats[1][i]]) for i in range(4))
    y_ssm = _s5_unchunk(_s5_scan(u_ctx, u_lat, wb, m, wc, coef))
    z = _s5_post(xs, y_ssm, l1_g_pre_mix, mod, l1_ssm_d)
    y = _glu_matmul(z.reshape(B * SEQ, D), l1_w_glu, 1024, 512)
    xs = _post_norm_residual(xs, y.reshape(B, SEQ, D), l1_g_post_mix, mod, 2, 0)
    hp, meta, cnt = _norm_mod(xs, l1_g_pre_ffn, mod, 3, 4, 0, router=(l1_w_router, l1_b_router), skip_rows=CTX)
    return _moe_ffn(xs.reshape(B * TOK, D), hp.reshape(B * SEQ, HALF_D), meta.reshape(B * SEQ, META_LANES), cnt,
                    l1_g_post_ffn, mod, 5, SEQ, 0,
                    l1_w_gate_up, l1_b_gate_up, l1_w_down, l1_b_down).reshape(B, SEQ, D)
```

```python
import functools
import math

import jax
import jax.numpy as jnp
import numpy as np
from jax import lax
from jax.experimental import pallas as pl
from jax.experimental.pallas import tpu as pltpu

D = 2048
B = 4
SEQ = 2048
CTX = 256
TOK = CTX + SEQ
GRID_W = 64
HD = 128
DIFF_HEADS = 4
GQA_Q_HEADS = 8
GQA_KV_HEADS = 2
GQA_GROUP = GQA_Q_HEADS // GQA_KV_HEADS
ROPE_THETA = 10000.0
ROPE_FREQS = HD // 4
ATTN_IN = 4608
N_EXPERTS = 32
TOP_K = 4
D_FF = D
SWIGLU_LIMIT = 7.0
SWIGLU_ALPHA = 1.702
RMS_EPS = 1e-6
SSM_GROUP = 16
SSM_STATE = 64
SSM_GROUPS = D // SSM_GROUP
SSM_CHUNK = 16

ROW_TILE = 256
MOD_ROWS = 8
MOD_CTX_ROW = B

V7X_VMEM_BYTES = 64 * 1024 * 1024
VMEM_LIMIT = 56 * 1024 * 1024


def _cparams(*sem):
    return pltpu.CompilerParams(dimension_semantics=sem, vmem_limit_bytes=VMEM_LIMIT)


def _rms(x, g):
    return x * lax.rsqrt(jnp.mean(x * x, axis=-1, keepdims=True) + RMS_EPS) * g


def _sigmoid(x):
    return 1.0 / (1.0 + jnp.exp(-x))


def _mod_kernel(c_ref, w_ref, b_ref, o_ref):
    c = c_ref[...]
    a = (c * _sigmoid(c)).astype(jnp.bfloat16)
    o_ref[...] = jnp.dot(a, w_ref[...].astype(jnp.bfloat16),
                         preferred_element_type=jnp.float32) + b_ref[...]


def _modulation(cond, w_mod, b_mod):
    tn = 1024
    n = w_mod.shape[1]
    return pl.pallas_call(
        _mod_kernel,
        grid=(n // tn,),
        in_specs=[pl.BlockSpec((MOD_ROWS, D), lambda j: (0, 0)),
                  pl.BlockSpec((D, tn), lambda j: (0, j)),
                  pl.BlockSpec((1, tn), lambda j: (0, j))],
        out_specs=pl.BlockSpec((MOD_ROWS, tn), lambda j: (0, j)),
        out_shape=jax.ShapeDtypeStruct((MOD_ROWS, n), jnp.float32),
        compiler_params=_cparams("arbitrary"),
        name="adaln_modulation",
    )(cond, w_mod, b_mod.reshape(1, n))


def _mod_spec(which, n_ctx_blocks):
    def idx(b, i):
        r = jnp.where(i < n_ctx_blocks, MOD_CTX_ROW, b)
        return (r * 6 + which, 0, 0)
    return pl.BlockSpec((None, 1, D), idx)


def _norm_mod_kernel(x_ref, g_ref, sh_ref, sc_ref, o_ref):
    h = _rms(x_ref[...], g_ref[...]) * (1.0 + sc_ref[...]) + sh_ref[...]
    o_ref[...] = h.astype(o_ref.dtype)


META_LANES = 128
META_E = 0
META_RANK = TOP_K
META_GATE = 2 * TOP_K
HALF_D = D // 2
HI_MASK = 0xFFFF0000


def _pack_bf16_pair(lo, hi):
    ulo = pltpu.bitcast(lo.astype(jnp.bfloat16).astype(jnp.float32), jnp.uint32)
    uhi = pltpu.bitcast(hi.astype(jnp.bfloat16).astype(jnp.float32), jnp.uint32)
    return lax.shift_right_logical(ulo, jnp.uint32(16)) | (uhi & jnp.uint32(HI_MASK))


def _unpack_bf16_pair(w):
    lo = pltpu.bitcast(lax.shift_left(w, jnp.uint32(16)), jnp.float32).astype(jnp.bfloat16)
    hi = pltpu.bitcast(w & jnp.uint32(HI_MASK), jnp.float32).astype(jnp.bfloat16)
    return lo, hi


def _norm_mod_router_kernel(x_ref, g_ref, sh_ref, sc_ref, wr_ref, br_ref, o_ref, meta_ref, cnt_ref, run_ref):
    first = jnp.logical_and(pl.program_id(0) == 0, pl.program_id(1) == 0)

    @pl.when(first)
    def _():
        run_ref[...] = jnp.zeros_like(run_ref)

    h = _rms(x_ref[...], g_ref[...]) * (1.0 + sc_ref[...]) + sh_ref[...]
    o_ref[...] = _pack_bf16_pair(h[:, :HALF_D], h[:, HALF_D:])
    w = wr_ref[...]
    h_hi, w_hi = h.astype(jnp.bfloat16), w.astype(jnp.bfloat16)
    h_lo = (h - h_hi.astype(jnp.float32)).astype(jnp.bfloat16)
    w_lo = (w - w_hi.astype(jnp.float32)).astype(jnp.bfloat16)
    logits = (jnp.dot(h_hi, w_hi, preferred_element_type=jnp.float32)
              + jnp.dot(h_lo, w_hi, preferred_element_type=jnp.float32)
              + jnp.dot(h_hi, w_lo, preferred_element_type=jnp.float32) + br_ref[...])
    lane = lax.broadcasted_iota(jnp.int32, (ROW_TILE, N_EXPERTS), 1)
    vals, hots = [], []
    l = logits
    for _ in range(TOP_K):
        m = jnp.max(l, axis=-1, keepdims=True)
        idx = jnp.min(jnp.where(l == m, lane, N_EXPERTS), axis=-1, keepdims=True)
        hot = lane == idx
        vals.append(m)
        hots.append(hot)
        l = jnp.where(hot, -jnp.inf, l)
    es = [jnp.exp(v - vals[0]) for v in vals]
    den = es[0] + es[1] + es[2] + es[3]
    onehot = sum(hot.astype(jnp.float32) for hot in hots)
    r_i = lax.broadcasted_iota(jnp.int32, (ROW_TILE, ROW_TILE), 0)
    c_i = lax.broadcasted_iota(jnp.int32, (ROW_TILE, ROW_TILE), 1)
    lower = jnp.where(r_i > c_i, 1.0, 0.0).astype(jnp.bfloat16)
    before = jnp.dot(lower, onehot.astype(jnp.bfloat16), preferred_element_type=jnp.float32) + run_ref[0:1, 0:N_EXPERTS]
    mlane = lax.broadcasted_iota(jnp.int32, (ROW_TILE, META_LANES), 1)
    lane_f = lane.astype(jnp.float32)
    meta = jnp.zeros((ROW_TILE, META_LANES), jnp.float32)
    for k in range(TOP_K):
        hot_f = hots[k].astype(jnp.float32)
        e_k = jnp.sum(hot_f * lane_f, axis=-1, keepdims=True)
        rank_k = jnp.sum(hot_f * before, axis=-1, keepdims=True)
        meta = jnp.where(mlane == META_E + k, e_k, meta)
        meta = jnp.where(mlane == META_RANK + k, rank_k, meta)
        meta = jnp.where(mlane == META_GATE + k, es[k] / den, meta)
    meta_ref[...] = meta
    total = run_ref[0:1, 0:N_EXPERTS] + jnp.sum(onehot, axis=0, keepdims=True)
    run_ref[0:1, 0:N_EXPERTS] = total
    cnt_ref[...] = jnp.broadcast_to(run_ref[0:1, :], cnt_ref.shape)


def _norm_mod(x, g, mod, shift_idx, scale_idx, n_ctx_blocks, router=None, skip_rows=0):
    t = x.shape[1] - skip_rows
    skip = skip_rows // ROW_TILE
    grid = (B, t // ROW_TILE)
    row = pl.BlockSpec((None, ROW_TILE, D), lambda b, i: (b, i, 0))
    in_specs = [pl.BlockSpec((None, ROW_TILE, D), lambda b, i: (b, i + skip, 0)),
                pl.BlockSpec((1, D), lambda b, i: (0, 0)),
                _mod_spec(shift_idx, n_ctx_blocks), _mod_spec(scale_idx, n_ctx_blocks)]
    args = [x, g.reshape(1, D), mod, mod]
    if router is None:
        return pl.pallas_call(
            _norm_mod_kernel, grid=grid, in_specs=in_specs, out_specs=row,
            out_shape=jax.ShapeDtypeStruct((B, t, D), jnp.bfloat16),
            compiler_params=_cparams("parallel", "parallel"), name="norm_mod",
        )(*args)
    w_router, b_router = router
    in_specs += [pl.BlockSpec((D, N_EXPERTS), lambda b, i: (0, 0)),
                 pl.BlockSpec((1, N_EXPERTS), lambda b, i: (0, 0))]
    args += [w_router, b_router.reshape(1, N_EXPERTS)]
    return pl.pallas_call(
        _norm_mod_router_kernel, grid=grid, in_specs=in_specs,
        out_specs=[pl.BlockSpec((None, ROW_TILE, HALF_D), lambda b, i: (b, i, 0)),
                   pl.BlockSpec((None, ROW_TILE, META_LANES), lambda b, i: (b, i, 0)),
                   pl.BlockSpec((8, META_LANES), lambda b, i: (0, 0))],
        out_shape=[jax.ShapeDtypeStruct((B, t, HALF_D), jnp.uint32),
                   jax.ShapeDtypeStruct((B, t, META_LANES), jnp.float32),
                   jax.ShapeDtypeStruct((8, META_LANES), jnp.float32)],
        scratch_shapes=[pltpu.VMEM((8, META_LANES), jnp.float32)],
        compiler_params=_cparams("arbitrary", "arbitrary"), name="norm_mod_router",
    )(*args)


def _post_norm_kernel(x_ref, y_ref, g_ref, gt_ref, o_ref):
    o_ref[...] = x_ref[...] + gt_ref[...] * _rms(y_ref[...], g_ref[...])


def _post_norm_residual(x, y, g, mod, gate_idx, n_ctx_blocks):
    t = y.shape[1]
    skip = (x.shape[1] - t) // ROW_TILE
    row = pl.BlockSpec((None, ROW_TILE, D), lambda b, i: (b, i, 0))
    xrow = pl.BlockSpec((None, ROW_TILE, D), lambda b, i: (b, i + skip, 0))
    return pl.pallas_call(
        _post_norm_kernel, grid=(B, t // ROW_TILE),
        in_specs=[xrow, row, pl.BlockSpec((1, D), lambda b, i: (0, 0)),
                  _mod_spec(gate_idx, n_ctx_blocks)],
        out_specs=xrow,
        out_shape=jax.ShapeDtypeStruct(x.shape, jnp.float32),
        input_output_aliases={0: 0},
        compiler_params=_cparams("parallel", "parallel"), name="post_norm_residual",
    )(x, y, g.reshape(1, D), mod)


def _matmul_kernel(a_ref, w_ref, o_ref):
    o_ref[...] = jnp.dot(a_ref[...], w_ref[...].astype(jnp.bfloat16),
                         preferred_element_type=jnp.float32).astype(o_ref.dtype)


def _matmul(a, w, tm, tn, out_dtype=jnp.float32, name="matmul"):
    m, k = a.shape
    n = w.shape[1]
    return pl.pallas_call(
        _matmul_kernel, grid=(m // tm, n // tn),
        in_specs=[pl.BlockSpec((tm, k), lambda i, j: (i, 0)),
                  pl.BlockSpec((k, tn), lambda i, j: (0, j))],
        out_specs=pl.BlockSpec((tm, tn), lambda i, j: (i, j)),
        out_shape=jax.ShapeDtypeStruct((m, n), out_dtype),
        compiler_params=_cparams("parallel", "arbitrary"), name=name,
    )(a, w)


def _glu_matmul_kernel(a_ref, wv_ref, wg_ref, o_ref):
    a = a_ref[...]
    val = jnp.dot(a, wv_ref[...].astype(jnp.bfloat16), preferred_element_type=jnp.float32)
    gate = jnp.dot(a, wg_ref[...].astype(jnp.bfloat16), preferred_element_type=jnp.float32)
    o_ref[...] = val * _sigmoid(gate)


def _glu_matmul(a, w_glu, tm, tn):
    m, k = a.shape
    n = w_glu.shape[1] // 2
    nj = n // tn
    return pl.pallas_call(
        _glu_matmul_kernel, grid=(m // tm, nj),
        in_specs=[pl.BlockSpec((tm, k), lambda i, j: (i, 0)),
                  pl.BlockSpec((k, tn), lambda i, j: (0, j)),
                  pl.BlockSpec((k, tn), lambda i, j: (0, nj + j))],
        out_specs=pl.BlockSpec((tm, tn), lambda i, j: (i, j)),
        out_shape=jax.ShapeDtypeStruct((m, n), jnp.float32),
        compiler_params=_cparams("parallel", "arbitrary"), name="glu_matmul",
    )(a, w_glu, w_glu)


IN_TN = 256
IN_ROPE_END = 8
IN_DV_END = 12
IN_GQ_END = 16
IN_GK_TILE = 16


def _inproj_kernel(a_ref, w_ref, cos_ref, sin_ref, gq_ref, gk_ref, o_ref):
    j = pl.program_id(1)
    acc = jnp.dot(a_ref[...], w_ref[...].astype(jnp.bfloat16), preferred_element_type=jnp.float32)

    def rope(x):
        return x * cos_ref[...] + pltpu.roll(x, HD // 2, 1) * sin_ref[...]

    def store(fn):
        for c in range(IN_TN // HD):
            o_ref[:, c * HD:(c + 1) * HD] = fn(acc[:, c * HD:(c + 1) * HD]).astype(o_ref.dtype)

    @pl.when(j < IN_ROPE_END)
    def _():
        store(rope)

    @pl.when(jnp.logical_or(jnp.logical_and(j >= IN_ROPE_END, j < IN_DV_END), j > IN_GK_TILE))
    def _():
        store(lambda x: x)

    @pl.when(jnp.logical_and(j >= IN_DV_END, j < IN_GQ_END))
    def _():
        store(lambda x: rope(_rms(x, gq_ref[...])))

    @pl.when(j == IN_GK_TILE)
    def _():
        store(lambda x: rope(_rms(x, gk_ref[...])))


def _inproj(h, w_in, cos2, sin2, g_q, g_k):
    m = h.shape[0]
    const = lambda i, j: (0, 0)
    return pl.pallas_call(
        _inproj_kernel, grid=(m // TOK, ATTN_IN // IN_TN),
        in_specs=[pl.BlockSpec((TOK, D), lambda i, j: (i, 0)),
                  pl.BlockSpec((D, IN_TN), lambda i, j: (0, j)),
                  pl.BlockSpec((TOK, HD), const), pl.BlockSpec((TOK, HD), const),
                  pl.BlockSpec((1, HD), const), pl.BlockSpec((1, HD), const)],
        out_specs=pl.BlockSpec((TOK, IN_TN), lambda i, j: (i, j)),
        out_shape=jax.ShapeDtypeStruct((m, ATTN_IN), jnp.bfloat16),
        compiler_params=_cparams("parallel", "arbitrary"), name="attn_inproj",
    )(h, w_in, cos2, sin2, g_q.reshape(1, HD), g_k.reshape(1, HD))


ATT_TQ = 256
ATT_SCALE_LOG2E = HD ** -0.5 * math.log2(math.e)


def _softmax_pv(q, k, v):
    s = lax.dot_general(q, k, (((1,), (1,)), ((), ())), preferred_element_type=jnp.float32)
    m = jnp.max(s, axis=-1, keepdims=True)
    e = jnp.exp2((s - m) * ATT_SCALE_LOG2E)
    l = jnp.sum(e, axis=-1, keepdims=True)
    return jnp.dot(e.astype(jnp.bfloat16), v, preferred_element_type=jnp.float32), l


def _diff_attn_kernel(lam_ref, q_ref, k_ref, v_ref, g_ref, o_ref, *, lambda_init):
    lp = lam_ref[...]
    lam = (jnp.exp(jnp.sum(lp[0:1] * lp[1:2], axis=-1, keepdims=True))
           - jnp.exp(jnp.sum(lp[2:3] * lp[3:4], axis=-1, keepdims=True)) + lambda_init)

    def run(nk):
        q = q_ref[...]
        k = k_ref[0:nk, :]
        v = v_ref[0:nk, :]
        pv1, l1 = _softmax_pv(q[:, :HD], k[:, :HD], v)
        pv2, l2 = _softmax_pv(q[:, HD:], k[:, HD:], v)
        o = pv1 / l1 - lam * (pv2 / l2)
        o_ref[...] = (_rms(o, g_ref[...]) * (1.0 - lambda_init)).astype(o_ref.dtype)

    @pl.when(pl.program_id(2) == 0)
    def _():
        run(CTX)

    @pl.when(pl.program_id(2) > 0)
    def _():
        run(TOK)


def _gqa_attn_kernel(q_ref, k_ref, v_ref, o_ref):
    def run(nk):
        k = k_ref[0:nk, :]
        v = v_ref[0:nk, :]
        for g in range(GQA_GROUP):
            pv, l = _softmax_pv(q_ref[:, g * HD:(g + 1) * HD], k, v)
            o_ref[:, g * HD:(g + 1) * HD] = (pv / l).astype(o_ref.dtype)

    @pl.when(pl.program_id(2) == 0)
    def _():
        run(CTX)

    @pl.when(pl.program_id(2) > 0)
    def _():
        run(TOK)


def _attention(p, lam_params, g_subln, lambda_init):
    nq = TOK // ATT_TQ
    dv = 2 * HD
    od = pl.pallas_call(
        functools.partial(_diff_attn_kernel, lambda_init=lambda_init),
        grid=(B, DIFF_HEADS, nq),
        in_specs=[pl.BlockSpec((4, HD), lambda b, h, i: (0, 0)),
                  pl.BlockSpec((None, ATT_TQ, dv), lambda b, h, i: (b, i, h)),
                  pl.BlockSpec((None, TOK, dv), lambda b, h, i: (b, 0, DIFF_HEADS + h)),
                  pl.BlockSpec((None, TOK, dv), lambda b, h, i: (b, 0, 2 * DIFF_HEADS + h)),
                  pl.BlockSpec((1, dv), lambda b, h, i: (0, 0))],
        out_specs=pl.BlockSpec((None, ATT_TQ, dv), lambda b, h, i: (b, i, h)),
        out_shape=jax.ShapeDtypeStruct((B, TOK, DIFF_HEADS * dv), jnp.bfloat16),
        compiler_params=_cparams("parallel", "parallel", "arbitrary"), name="diff_attention",
    )(lam_params, p, p, p, g_subln.reshape(1, dv))
    gq_w = GQA_GROUP * HD
    gq0 = 3072 // gq_w
    gk0 = 4096 // HD
    gv0 = 4352 // HD
    og = pl.pallas_call(
        _gqa_attn_kernel,
        grid=(B, GQA_KV_HEADS, nq),
        in_specs=[pl.BlockSpec((None, ATT_TQ, gq_w), lambda b, n, i: (b, i, gq0 + n)),
                  pl.BlockSpec((None, TOK, HD), lambda b, n, i: (b, 0, gk0 + n)),
                  pl.BlockSpec((None, TOK, HD), lambda b, n, i: (b, 0, gv0 + n))],
        out_specs=pl.BlockSpec((None, ATT_TQ, gq_w), lambda b, n, i: (b, i, n)),
        out_shape=jax.ShapeDtypeStruct((B, TOK, GQA_Q_HEADS * HD), jnp.bfloat16),
        compiler_params=_cparams("parallel", "parallel", "arbitrary"), name="gqa_attention",
    )(p, p, p)
    return jnp.concatenate([od, og], axis=-1)


def _rope_tables():
    rows = SEQ // GRID_W
    row_id, col_id = jnp.meshgrid(jnp.arange(rows), jnp.arange(GRID_W), indexing="ij")
    inv_freq = ROPE_THETA ** (-jnp.arange(ROPE_FREQS, dtype=jnp.float32) / ROPE_FREQS)
    ang = jnp.concatenate([row_id.reshape(-1, 1) * inv_freq, col_id.reshape(-1, 1) * inv_freq], axis=-1)
    cos, sin = jnp.cos(ang), jnp.sin(ang)
    cos2 = jnp.concatenate([cos, cos], axis=-1)
    sin2 = jnp.concatenate([-sin, sin], axis=-1)
    cos2 = jnp.concatenate([jnp.ones((CTX, HD), jnp.float32), cos2], axis=0)
    sin2 = jnp.concatenate([jnp.zeros((CTX, HD), jnp.float32), sin2], axis=0)
    return cos2, sin2


MOE_TM = 512
MOE_SUB = 256
MOE_TF = 1024
DISPATCH_TT = 256
COMBINE_TT = 128


def _for_valid_rows(rows, compute, o_ref):
    half = MOE_SUB // 2
    width = o_ref.shape[1]
    for s in range(MOE_TM // MOE_SUB):
        base = s * MOE_SUB

        @pl.when(rows > base + half)
        def _():
            compute(slice(base, base + MOE_SUB))

        @pl.when(jnp.logical_and(rows > base, rows <= base + half))
        def _():
            compute(slice(base, base + half))
            o_ref[base + half:base + MOE_SUB, :] = jnp.zeros((half, width), o_ref.dtype)

        @pl.when(rows <= base)
        def _():
            o_ref[base:base + MOE_SUB, :] = jnp.zeros((MOE_SUB, width), o_ref.dtype)


def _mxu_dot(a_bf16, w_f32):
    return lax.dot_general(a_bf16, w_f32, (((1,), (0,)), ((), ())), preferred_element_type=jnp.float32)


def _moe_up_kernel(be_ref, first_ref, rows_ref, nxt_ref, x_ref, w_hbm, bg_ref, bl_ref, o_ref,
                   wst, slot_ref, sem):
    f = pl.program_id(0)
    b = pl.program_id(1)
    nf = pl.num_programs(0)

    def copies(e, ff, slot):
        col = pl.multiple_of(ff * MOE_TF, MOE_TF)
        return [pltpu.make_async_copy(w_hbm.at[e, :, pl.ds(part * D_FF + col, MOE_TF)],
                                      wst.at[slot, part], sem.at[slot, part]) for part in range(2)]

    @pl.when(first_ref[b] == 1)
    def _():
        @pl.when(jnp.logical_and(f == 0, b == 0))
        def _():
            slot_ref[0] = 1
            for c in copies(be_ref[0], 0, 0):
                c.start()

        slot = 1 - slot_ref[0]
        slot_ref[0] = slot
        for c in copies(be_ref[b], f, slot):
            c.wait()
        e_next = nxt_ref[b]

        @pl.when(e_next >= 0)
        def _():
            for c in copies(e_next, f, 1 - slot):
                c.start()

        @pl.when(jnp.logical_and(e_next < 0, f + 1 < nf))
        def _():
            for c in copies(be_ref[0], f + 1, 1 - slot):
                c.start()

    def compute(sl):
        slot = slot_ref[0]
        lo, hi = _unpack_bf16_pair(x_ref[sl, :])
        glu = (_mxu_dot(lo, wst[slot, 0, :HALF_D, :]) + _mxu_dot(hi, wst[slot, 0, HALF_D:, :]) + bg_ref[...])
        lin = (_mxu_dot(lo, wst[slot, 1, :HALF_D, :]) + _mxu_dot(hi, wst[slot, 1, HALF_D:, :]) + bl_ref[...])
        glu = jnp.minimum(glu, SWIGLU_LIMIT)
        lin = jnp.clip(lin, -SWIGLU_LIMIT, SWIGLU_LIMIT)
        o_ref[sl, :] = (glu * _sigmoid(SWIGLU_ALPHA * glu) * (lin + 1.0)).astype(o_ref.dtype)

    _for_valid_rows(rows_ref[b], compute, o_ref)


def _moe_down_kernel(be_ref, first_ref, rows_ref, nxt_ref, a_ref, w_hbm, bias_ref, o_ref, wst, slot_ref, sem):
    b = pl.program_id(0)

    def copy(e, slot):
        return pltpu.make_async_copy(w_hbm.at[e], wst.at[slot], sem.at[slot])

    @pl.when(first_ref[b] == 1)
    def _():
        @pl.when(b == 0)
        def _():
            slot_ref[0] = 1
            copy(be_ref[0], 0).start()

        slot = 1 - slot_ref[0]
        slot_ref[0] = slot
        copy(be_ref[b], slot).wait()
        e_next = nxt_ref[b]

        @pl.when(e_next >= 0)
        def _():
            copy(e_next, 1 - slot).start()

    def compute(sl):
        o_ref[sl, :] = _mxu_dot(a_ref[sl, :], wst[slot_ref[0]]) + bias_ref[...]

    _for_valid_rows(rows_ref[b], compute, o_ref)


def _moe_experts(x_sorted, tables, w_gate_up, b_gate_up, w_down, b_down):
    r = x_sorted.shape[0]
    nb = r // MOE_TM
    nf = D_FF // MOE_TF
    bgu = b_gate_up.reshape(N_EXPERTS, 1, 2 * D_FF)
    act = pl.pallas_call(
        _moe_up_kernel,
        grid_spec=pltpu.PrefetchScalarGridSpec(
            num_scalar_prefetch=4, grid=(nf, nb),
            in_specs=[pl.BlockSpec((MOE_TM, HALF_D), lambda f, b, be, fi, ro, nx: (b, 0)),
                      pl.BlockSpec(memory_space=pl.ANY),
                      pl.BlockSpec((None, 1, MOE_TF), lambda f, b, be, fi, ro, nx: (be[b], 0, f)),
                      pl.BlockSpec((None, 1, MOE_TF), lambda f, b, be, fi, ro, nx: (be[b], 0, nf + f))],
            out_specs=pl.BlockSpec((MOE_TM, MOE_TF), lambda f, b, be, fi, ro, nx: (b, f)),
            scratch_shapes=[pltpu.VMEM((2, 2, D, MOE_TF), jnp.float32), pltpu.SMEM((1,), jnp.int32),
                            pltpu.SemaphoreType.DMA((2, 2))]),
        out_shape=jax.ShapeDtypeStruct((r, D_FF), jnp.bfloat16),
        compiler_params=_cparams("arbitrary", "arbitrary"), name="moe_gate_up",
    )(*tables, x_sorted, w_gate_up, bgu, bgu)
    return pl.pallas_call(
        _moe_down_kernel,
        grid_spec=pltpu.PrefetchScalarGridSpec(
            num_scalar_prefetch=4, grid=(nb,),
            in_specs=[pl.BlockSpec((MOE_TM, D_FF), lambda b, be, fi, ro, nx: (b, 0)),
                      pl.BlockSpec(memory_space=pl.ANY),
                      pl.BlockSpec((None, 1, D), lambda b, be, fi, ro, nx: (be[b], 0, 0))],
            out_specs=pl.BlockSpec((MOE_TM, D), lambda b, be, fi, ro, nx: (b, 0)),
            scratch_shapes=[pltpu.VMEM((2, D_FF, D), jnp.float32), pltpu.SMEM((1,), jnp.int32),
                            pltpu.SemaphoreType.DMA((2,))]),
        out_shape=jax.ShapeDtypeStruct((r, D), jnp.float32),
        compiler_params=_cparams("arbitrary"), name="moe_down",
    )(*tables, act, w_down, b_down.reshape(N_EXPERTS, 1, D))


def _dispatch_kernel(dest_ref, hp_ref, xs_in_hbm, xs_hbm, idx_smem, sem_idx, sem_rows):
    del xs_in_hbm
    idx_copy = pltpu.make_async_copy(dest_ref, idx_smem, sem_idx)
    idx_copy.start()
    idx_copy.wait()

    def issue(t, carry):
        for k in range(TOP_K):
            d = idx_smem[0, 0, t * TOP_K + k]
            pltpu.make_async_copy(hp_ref.at[t], xs_hbm.at[d], sem_rows).start(priority=k % 2)
        return carry

    lax.fori_loop(0, DISPATCH_TT, issue, 0, unroll=4)
    for k in range(TOP_K):
        pltpu.make_async_copy(hp_ref, xs_hbm.at[pl.ds(0, DISPATCH_TT)], sem_rows).wait()


def _dispatch(dest, hp, n_rows):
    n_tok = hp.shape[0]
    n = n_tok // DISPATCH_TT
    width = DISPATCH_TT * TOP_K
    zeros = jnp.zeros((n_rows, HALF_D), jnp.uint32)
    return pl.pallas_call(
        _dispatch_kernel, grid=(n,),
        in_specs=[pl.BlockSpec((1, 1, width), lambda i: (i, 0, 0)),
                  pl.BlockSpec((DISPATCH_TT, HALF_D), lambda i: (i, 0)), pl.BlockSpec(memory_space=pl.ANY)],
        out_specs=pl.BlockSpec(memory_space=pl.ANY),
        out_shape=jax.ShapeDtypeStruct((n_rows, HALF_D), jnp.uint32),
        scratch_shapes=[pltpu.SMEM((1, 1, width), jnp.int32), pltpu.SemaphoreType.DMA, pltpu.SemaphoreType.DMA],
        input_output_aliases={2: 0},
        compiler_params=_cparams("arbitrary"), name="moe_dispatch",
    )(dest.reshape(n, 1, width), hp, zeros)


def _combine_kernel(dcur_ref, dnext_ref, out_hbm, meta_ref, x_ref, g_ref, gt_ref, o_ref,
                    idx_smem, buf, sem_idx, sem_rows):
    i = pl.program_id(0)
    n = pl.num_programs(0)

    def row_copy(d, slot, k, t):
        return pltpu.make_async_copy(out_hbm.at[d], buf.at[slot, k, t], sem_rows.at[slot])

    def gather(d_ref, slot):
        idx_copy = pltpu.make_async_copy(d_ref, idx_smem, sem_idx)
        idx_copy.start()
        idx_copy.wait()

        def issue(t, carry):
            for k in range(TOP_K):
                row_copy(idx_smem[0, 0, t * TOP_K + k], slot, k, t).start(priority=k % 2)
            return carry
        lax.fori_loop(0, COMBINE_TT, issue, 0, unroll=4)

    @pl.when(i == 0)
    def _():
        gather(dcur_ref, 0)

    @pl.when(i + 1 < n)
    def _():
        gather(dnext_ref, (i + 1) % 2)

    slot = i % 2
    pltpu.make_async_copy(buf.at[slot], buf.at[slot], sem_rows.at[slot]).wait()

    meta = meta_ref[...]
    f = meta[:, META_GATE:META_GATE + 1] * buf[slot, 0]
    for k in range(1, TOP_K):
        f = f + meta[:, META_GATE + k:META_GATE + k + 1] * buf[slot, k]
    o_ref[...] = x_ref[...] + gt_ref[...] * _rms(f, g_ref[...])


def _combine(dest, out_sorted, meta, x, g, mod, gate_idx, rows_per_batch, ctx_rows):
    n_tok = dest.shape[0]
    n = n_tok // COMBINE_TT
    width = COMBINE_TT * TOP_K
    per_batch = rows_per_batch // COMBINE_TT
    x_per_batch = x.shape[0] // B // COMBINE_TT
    skip = x_per_batch - per_batch
    ctx_blocks = ctx_rows // COMBINE_TT

    def gate_row(i):
        r = jnp.where(i % per_batch < ctx_blocks, MOD_CTX_ROW, i // per_batch)
        return (r * 6 + gate_idx, 0, 0)

    row = pl.BlockSpec((COMBINE_TT, D), lambda i: (i, 0))
    xrow = pl.BlockSpec((COMBINE_TT, D), lambda i: ((i // per_batch) * x_per_batch + skip + i % per_batch, 0))
    return pl.pallas_call(
        _combine_kernel, grid=(n,),
        in_specs=[pl.BlockSpec((1, 1, width), lambda i: (i, 0, 0)),
                  pl.BlockSpec((1, 1, width), lambda i: (jnp.minimum(i + 1, n - 1), 0, 0)),
                  pl.BlockSpec(memory_space=pl.ANY),
                  pl.BlockSpec((COMBINE_TT, META_LANES), lambda i: (i, 0)),
                  xrow, pl.BlockSpec((1, D), lambda i: (0, 0)),
                  pl.BlockSpec((None, 1, D), gate_row)],
        out_specs=row,
        out_shape=jax.ShapeDtypeStruct((n_tok, D), jnp.float32),
        scratch_shapes=[pltpu.SMEM((1, 1, width), jnp.int32),
                        pltpu.VMEM((2, TOP_K, COMBINE_TT, D), jnp.float32),
                        pltpu.SemaphoreType.DMA, pltpu.SemaphoreType.DMA((2,))],
        input_output_aliases={4: 0} if skip == 0 else {},
        compiler_params=_cparams("arbitrary"), name="moe_combine",
    )(dest.reshape(n, 1, width), dest.reshape(n, 1, width), out_sorted, meta, x, g.reshape(1, D), mod)


def _moe_block_tables(counts, n_assign):
    padded = (counts + MOE_TM - 1) // MOE_TM * MOE_TM
    pad_end = jnp.cumsum(padded)
    pad_start = pad_end - padded
    nb = -(-(n_assign + N_EXPERTS * (MOE_TM - 1)) // MOE_TM)
    block_start = jnp.arange(nb, dtype=jnp.int32) * MOE_TM
    block_e_raw = jnp.minimum(jnp.sum(block_start[:, None] >= pad_end[None, :], axis=1), N_EXPERTS - 1)
    block_e_raw = block_e_raw.astype(jnp.int32)
    onehot_e = block_e_raw[:, None] == jnp.arange(N_EXPERTS)[None, :]
    valid_end = jnp.sum(jnp.where(onehot_e, (pad_start + counts)[None, :], 0), axis=1)
    block_rows = jnp.clip(valid_end - block_start, 0, MOE_TM).astype(jnp.int32)
    block_rows = jnp.where(block_start < pad_end[-1], block_rows, 0)
    last_e = jnp.max(jnp.where(block_rows > 0, block_e_raw, 0))
    block_e = jnp.where(block_rows > 0, block_e_raw, last_e)
    block_first = jnp.concatenate([jnp.ones((1,), jnp.int32),
                                   (block_e[1:] != block_e[:-1]).astype(jnp.int32)])
    idx = jnp.arange(nb, dtype=jnp.int32)
    first_pos = jnp.where(block_first == 1, idx, nb)
    later = jnp.where(idx[None, :] > idx[:, None], first_pos[None, :], nb)
    next_pos = jnp.min(later, axis=1)
    next_e = jnp.sum(jnp.where(idx[None, :] == next_pos[:, None], block_e[None, :], 0), axis=1)
    block_next = jnp.where(next_pos < nb, next_e, -1).astype(jnp.int32)
    return pad_start, nb * MOE_TM, (block_e, block_first, block_rows, block_next)


def _moe_ffn(x_res, hp, meta, cnt, g_post, mod, gate_idx, rows_per_batch, ctx_rows,
             w_gate_up, b_gate_up, w_down, b_down):
    n_tok = hp.shape[0]
    counts = cnt[0, :N_EXPERTS].astype(jnp.int32)
    pad_start, n_rows, tables = _moe_block_tables(counts, n_tok * TOP_K)
    top_e = meta[:, META_E:META_E + TOP_K].astype(jnp.int32)
    rank = meta[:, META_RANK:META_RANK + TOP_K].astype(jnp.int32)
    hot = top_e[:, :, None] == jnp.arange(N_EXPERTS)[None, None, :]
    dest = jnp.sum(jnp.where(hot, pad_start[None, None, :], 0), axis=-1) + rank
    x_sorted = _dispatch(dest, hp, n_rows)
    out_sorted = _moe_experts(x_sorted, tables, w_gate_up, b_gate_up, w_down, b_down)
    return _combine(dest, out_sorted, meta, x_res, g_post, mod, gate_idx, rows_per_batch, ctx_rows)


S5_GB = 8
S5_LANES = SSM_CHUNK * SSM_GROUP
S5_NK_CTX = CTX // SSM_CHUNK
S5_NK_LAT = SEQ // SSM_CHUNK
S5_PAIR = 2 * B


def _s5_direction(reverse, uc_ref, ul_ref, wb_ref, m_ref, wc_ref, coef_ref, y_ref,
                  s1c, s2c, s1l, s2l, xin, accumulate):
    half = 2 * SSM_STATE
    for g in range(S5_GB):
        sc = jnp.dot(uc_ref[g], wb_ref[g], preferred_element_type=jnp.float32)
        s1c[g] = sc[:, :half]
        s2c[g] = sc[:, half:]
        sl = jnp.dot(ul_ref[g], wb_ref[g], preferred_element_type=jnp.float32)
        s1l[g] = sl[:, :half]
        s2l[g] = sl[:, half:]

    lower = lax.broadcasted_iota(jnp.int32, (S5_PAIR, half), 0) < B
    p1 = [jnp.broadcast_to(coef_ref[g, 0:1, :], (S5_PAIR, half)) for g in range(S5_GB)]
    p2 = [jnp.broadcast_to(coef_ref[g, 1:2, :], (S5_PAIR, half)) for g in range(S5_GB)]

    def tile_step(g, t1, t2, v1, v2):
        y1a = p1[g] * v1 + p2[g] * v2 + t1
        y1b = p1[g] * v2 - p2[g] * v1 + t2
        r1a = pltpu.roll(y1a, B, 0)
        r1b = pltpu.roll(y1b, B, 0)
        y2a = p1[g] * r1a + p2[g] * r1b + t1
        y2b = p1[g] * r1b - p2[g] * r1a + t2
        r2a = pltpu.roll(y2a, B, 0)
        r2b = pltpu.roll(y2b, B, 0)
        if not reverse:
            x_in = jnp.where(lower, v1, r1a)
            return x_in, jnp.where(lower, r2a, y2a), jnp.where(lower, r2b, y2b)
        x_in = jnp.where(lower, r1a, v1)
        return x_in, jnp.where(lower, y2a, r2a), jnp.where(lower, y2b, r2b)

    def scan(s1, s2, n_tiles, state, record):
        def body(j, carry):
            jj = (n_tiles - 1 - j) if reverse else j
            r0 = pl.multiple_of(jj * S5_PAIR, S5_PAIR)
            new = []
            for g in range(S5_GB):
                v1, v2 = carry[2 * g], carry[2 * g + 1]
                x_in, v1, v2 = tile_step(g, s1[g, pl.ds(r0, S5_PAIR), :], s2[g, pl.ds(r0, S5_PAIR), :], v1, v2)
                if record:
                    xin[g, pl.ds(r0, S5_PAIR), :] = x_in
                new += [v1, v2]
            return tuple(new)
        return lax.fori_loop(0, n_tiles, body, state)

    zero = jnp.zeros((S5_PAIR, half), jnp.float32)
    state = tuple(zero for _ in range(2 * S5_GB))
    state = scan(s1c, s2c, S5_NK_CTX * B // S5_PAIR, state, False)
    scan(s1l, s2l, S5_NK_LAT * B // S5_PAIR, state, True)

    for g in range(S5_GB):
        y = (jnp.dot(ul_ref[g], m_ref[g], preferred_element_type=jnp.float32)
             + jnp.dot(xin[g].astype(jnp.bfloat16), wc_ref[g], preferred_element_type=jnp.float32))
        if accumulate:
            y_ref[g] = y_ref[g] + y
        else:
            y_ref[g] = y


def _s5_kernel(uc_ref, ul_ref, wb_ref, m_ref, wc_ref, coef_ref, y_ref, s1c, s2c, s1l, s2l, xin):
    args = (uc_ref, ul_ref, wb_ref, m_ref, wc_ref, coef_ref, y_ref, s1c, s2c, s1l, s2l, xin)

    @pl.when(pl.program_id(1) == 0)
    def _():
        _s5_direction(False, *args, accumulate=False)

    @pl.when(pl.program_id(1) == 1)
    def _():
        _s5_direction(True, *args, accumulate=True)


def _s5_scan(u_ctx, u_lat, wb, m, wc, coef):
    rc, rl = u_ctx.shape[1], u_lat.shape[1]
    half = 2 * SSM_STATE
    wspec = lambda k, n: pl.BlockSpec((None, S5_GB, k, n), lambda gi, d: (d, gi, 0, 0))
    return pl.pallas_call(
        _s5_kernel, grid=(SSM_GROUPS // S5_GB, 2),
        in_specs=[pl.BlockSpec((S5_GB, rc, S5_LANES), lambda gi, d: (gi, 0, 0)),
                  pl.BlockSpec((S5_GB, rl, S5_LANES), lambda gi, d: (gi, 0, 0)),
                  wspec(S5_LANES, 2 * half), wspec(S5_LANES, S5_LANES), wspec(half, S5_LANES),
                  wspec(2, half)],
        out_specs=pl.BlockSpec((S5_GB, rl, S5_LANES), lambda gi, d: (gi, 0, 0)),
        out_shape=jax.ShapeDtypeStruct((SSM_GROUPS, rl, S5_LANES), jnp.float32),
        scratch_shapes=[pltpu.VMEM((S5_GB, rc, half), jnp.float32), pltpu.VMEM((S5_GB, rc, half), jnp.float32),
                        pltpu.VMEM((S5_GB, rl, half), jnp.float32), pltpu.VMEM((S5_GB, rl, half), jnp.float32),
                        pltpu.VMEM((S5_GB, rl, half), jnp.float32)],
        compiler_params=_cparams("parallel", "arbitrary"), name="s5_scan",
    )(u_ctx, u_lat, wb, m, wc, coef)


def _s5_matrices(a_re, a_im, b_re, b_im, c_re, c_im, log_dt, reverse):
    hp = lax.Precision.HIGHEST
    n = SSM_CHUNK
    g, p = SSM_GROUPS, SSM_STATE
    dt = jnp.exp(log_dt)[:, None]
    mag = jnp.exp(a_re * dt)
    ab_re, ab_im = mag * jnp.cos(a_im * dt), mag * jnp.sin(a_im * dt)
    den = a_re * a_re + a_im * a_im
    f_re = ((ab_re - 1.0) * a_re + ab_im * a_im) / den
    f_im = (ab_im * a_re - (ab_re - 1.0) * a_im) / den
    b_re_t, b_im_t = b_re.transpose(0, 2, 1), b_im.transpose(0, 2, 1)
    bb_re = f_re[:, None, :] * b_re_t - f_im[:, None, :] * b_im_t
    bb_im = f_re[:, None, :] * b_im_t + f_im[:, None, :] * b_re_t
    tau = jnp.arange(n + 1, dtype=jnp.float32)[None, None, :]
    ang_re, ang_im = (a_re * dt)[:, :, None], (a_im * dt)[:, :, None]
    pmag = jnp.exp(tau * ang_re)
    pw_re, pw_im = pmag * jnp.cos(tau * ang_im), pmag * jnp.sin(tau * ang_im)
    c_re_t, c_im_t = c_re.transpose(0, 2, 1), c_im.transpose(0, 2, 1)

    def c_times_power(exps):
        q_re, q_im = pw_re[:, :, exps], pw_im[:, :, exps]
        re = c_re_t[:, :, None, :] * q_re[:, :, :, None] - c_im_t[:, :, None, :] * q_im[:, :, :, None]
        im = c_re_t[:, :, None, :] * q_im[:, :, :, None] + c_im_t[:, :, None, :] * q_re[:, :, :, None]
        return re.reshape(g, p, -1), im.reshape(g, p, -1)

    lags = np.arange(n)[::-1] if reverse else np.arange(n)
    ca_re, ca_im = c_times_power(lags)
    kst = jnp.matmul(jnp.concatenate([bb_re, bb_im], axis=-1), jnp.concatenate([ca_re, -ca_im], axis=1),
                     precision=hp)
    zeros = jnp.zeros_like(kst)
    if reverse:
        z = jnp.concatenate([kst, zeros], axis=-1)
        m = jnp.stack([z[:, :, (n - 1 - s) * SSM_GROUP:(n - 1 - s) * SSM_GROUP + S5_LANES] for s in range(n)], axis=1)
    else:
        z = jnp.concatenate([zeros, kst], axis=-1)
        m = jnp.stack([z[:, :, S5_LANES - s * SSM_GROUP:2 * S5_LANES - s * SSM_GROUP] for s in range(n)], axis=1)
    m = m.reshape(g, S5_LANES, S5_LANES)
    e_idx = np.arange(n) if reverse else (n - 1 - np.arange(n))
    ae_re = pw_re[:, :, e_idx].transpose(0, 2, 1)[:, :, None, :]
    ae_im = pw_im[:, :, e_idx].transpose(0, 2, 1)[:, :, None, :]
    wb_re = (ae_re * bb_re[:, None] - ae_im * bb_im[:, None]).reshape(g, S5_LANES, p)
    wb_im = (ae_re * bb_im[:, None] + ae_im * bb_re[:, None]).reshape(g, S5_LANES, p)
    wb = jnp.concatenate([wb_re, wb_im, wb_im, wb_re], axis=-1)
    f_idx = (n - np.arange(n)) if reverse else (np.arange(n) + 1)
    cf_re, cf_im = c_times_power(f_idx)
    wc = jnp.concatenate([cf_re, -cf_im], axis=1)
    an_re, an_im = pw_re[:, :, n], pw_im[:, :, n]
    coef = jnp.stack([jnp.concatenate([an_re, an_re], axis=-1),
                      jnp.concatenate([-an_im, an_im], axis=-1)], axis=1)
    return wb.astype(jnp.bfloat16), m.astype(jnp.bfloat16), wc.astype(jnp.bfloat16), coef


def _s5_chunks(h):
    t = h.shape[1]
    nk = t // SSM_CHUNK
    u = h.reshape(B, nk, SSM_CHUNK, SSM_GROUPS, SSM_GROUP).transpose(3, 1, 0, 2, 4)
    return u.reshape(SSM_GROUPS, nk * B, S5_LANES)


def _s5_unchunk(y):
    nk = y.shape[1] // B
    y = y.reshape(SSM_GROUPS, nk, B, SSM_CHUNK, SSM_GROUP).transpose(2, 1, 3, 0, 4)
    return y.reshape(B, nk * SSM_CHUNK, D)


def _s5_post_kernel(x_ref, y_ref, g_ref, sh_ref, sc_ref, d_ref, o_ref):
    h = _rms(x_ref[...], g_ref[...]) * (1.0 + sc_ref[...]) + sh_ref[...]
    y = d_ref[...] * h + y_ref[...]
    z = 0.5 * y * (1.0 + jnp.tanh(math.sqrt(2.0 / math.pi) * (y + 0.044715 * (y * y * y))))
    o_ref[...] = z.astype(o_ref.dtype)


def _s5_post(xs, y_ssm, g, mod, d_skip):
    skip = CTX // ROW_TILE
    row = pl.BlockSpec((None, ROW_TILE, D), lambda b, i: (b, i, 0))
    xrow = pl.BlockSpec((None, ROW_TILE, D), lambda b, i: (b, i + skip, 0))
    vec = pl.BlockSpec((1, D), lambda b, i: (0, 0))
    return pl.pallas_call(
        _s5_post_kernel, grid=(B, SEQ // ROW_TILE),
        in_specs=[xrow, row, vec, _mod_spec(0, 0), _mod_spec(1, 0), vec],
        out_specs=row,
        out_shape=jax.ShapeDtypeStruct((B, SEQ, D), jnp.bfloat16),
        compiler_params=_cparams("parallel", "parallel"), name="s5_skip_gelu",
    )(xs, y_ssm, g.reshape(1, D), mod, mod, d_skip.reshape(1, D))


def _layer_modulation(c, c_ctx, w_mod, b_mod):
    cond = jnp.concatenate([c, c_ctx[None, :], jnp.zeros((MOD_ROWS - B - 1, D), jnp.float32)], axis=0)
    return _modulation(cond, w_mod, b_mod).reshape(MOD_ROWS * 6, 1, D)


def kernel(x, c, ctx, c_ctx, l0_w_mod, l0_b_mod, l0_g_pre_mix, l0_g_post_mix, l0_g_pre_ffn, l0_g_post_ffn, l0_w_in, l0_w_out, l0_lambda_q1, l0_lambda_k1, l0_lambda_q2, l0_lambda_k2, l0_g_subln, l0_g_qnorm, l0_g_knorm, l0_w_router, l0_b_router, l0_w_gate_up, l0_b_gate_up, l0_w_down, l0_b_down, l1_w_mod, l1_b_mod, l1_g_pre_mix, l1_g_post_mix, l1_g_pre_ffn, l1_g_post_ffn, l1_ssm_a_re, l1_ssm_a_im, l1_ssm_b_re, l1_ssm_b_im, l1_ssm_c_re, l1_ssm_c_im, l1_ssm_log_dt, l1_ssm_d, l1_w_glu, l1_w_router, l1_b_router, l1_w_gate_up, l1_b_gate_up, l1_w_down, l1_b_down):
    xs = jnp.concatenate([ctx, x], axis=1)

    mod = _layer_modulation(c, c_ctx, l0_w_mod, l0_b_mod)
    h = _norm_mod(xs, l0_g_pre_mix, mod, 0, 1, 1)
    cos2, sin2 = _rope_tables()
    p = _inproj(h.reshape(B * TOK, D), l0_w_in, cos2, sin2, l0_g_qnorm, l0_g_knorm)
    lam_params = jnp.stack([l0_lambda_q1, l0_lambda_k1, l0_lambda_q2, l0_lambda_k2])
    lambda_init = 0.8 - 0.6 * math.exp(-0.3 * 0)
    att = _attention(p.reshape(B, TOK, ATTN_IN), lam_params, l0_g_subln, lambda_init)
    y = _matmul(att.reshape(B * TOK, D), l0_w_out, 1024, 512, name="attn_outproj")
    xs = _post_norm_residual(xs, y.reshape(B, TOK, D), l0_g_post_mix, mod, 2, 1)
    hp, meta, cnt = _norm_mod(xs, l0_g_pre_ffn, mod, 3, 4, 1, router=(l0_w_router, l0_b_router))
    xs = _moe_ffn(xs.reshape(B * TOK, D), hp.reshape(B * TOK, HALF_D), meta.reshape(B * TOK, META_LANES), cnt,
                  l0_g_post_ffn, mod, 5, TOK, CTX,
                  l0_w_gate_up, l0_b_gate_up, l0_w_down, l0_b_down).reshape(B, TOK, D)

    mod = _layer_modulation(c, c_ctx, l1_w_mod, l1_b_mod)
    h = _norm_mod(xs, l1_g_pre_mix, mod, 0, 1, 1)
    u_ctx = _s5_chunks(h[:, :CTX])
    u_lat = _s5_chunks(h[:, CTX:])
    mats = [_s5_matrices(l1_ssm_a_re[d], l1_ssm_a_im[d], l1_ssm_b_re[d], l1_ssm_b_im[d],
                         l1_ssm_c_re[d], l1_ssm_c_im[d], l1_ssm_log_dt[d], reverse=bool(d))
            for d in range(2)]
    wb, m, wc, coef = (jnp.stack([mats[0][i], mats[1][i]]) for i in range(4))
    y_ssm = _s5_unchunk(_s5_scan(u_ctx, u_lat, wb, m, wc, coef))
    z = _s5_post(xs, y_ssm, l1_g_pre_mix, mod, l1_ssm_d)
    y = _glu_matmul(z.reshape(B * SEQ, D), l1_w_glu, 1024, 512)
    xs = _post_norm_residual(xs, y.reshape(B, SEQ, D), l1_g_post_mix, mod, 2, 0)
    hp, meta, cnt = _norm_mod(xs, l1_g_pre_ffn, mod, 3, 4, 0, router=(l1_w_router, l1_b_router), skip_rows=CTX)
    return _moe_ffn(xs.reshape(B * TOK, D), hp.reshape(B * SEQ, HALF_D), meta.reshape(B * SEQ, META_LANES), cnt,
                    l1_g_post_ffn, mod, 5, SEQ, 0,
                    l1_w_gate_up, l1_b_gate_up, l1_w_down, l1_b_down).reshape(B, SEQ, D)
```

```python
import functools
import math

import jax
import jax.numpy as jnp
import numpy as np
from jax import lax
from jax.experimental import pallas as pl
from jax.experimental.pallas import tpu as pltpu

D = 2048
B = 4
SEQ = 2048
CTX = 256
TOK = CTX + SEQ
GRID_W = 64
HD = 128
DIFF_HEADS = 4
GQA_Q_HEADS = 8
GQA_KV_HEADS = 2
GQA_GROUP = GQA_Q_HEADS // GQA_KV_HEADS
ROPE_THETA = 10000.0
ROPE_FREQS = HD // 4
ATTN_IN = 4608
N_EXPERTS = 32
TOP_K = 4
D_FF = D
SWIGLU_LIMIT = 7.0
SWIGLU_ALPHA = 1.702
RMS_EPS = 1e-6
SSM_GROUP = 16
SSM_STATE = 64
SSM_GROUPS = D // SSM_GROUP
SSM_CHUNK = 16

ROW_TILE = 256
MOD_ROWS = 8
MOD_CTX_ROW = B

V7X_VMEM_BYTES = 64 * 1024 * 1024
VMEM_LIMIT = 56 * 1024 * 1024


def _cparams(*sem):
    return pltpu.CompilerParams(dimension_semantics=sem, vmem_limit_bytes=VMEM_LIMIT)


def _rms(x, g):
    return x * lax.rsqrt(jnp.mean(x * x, axis=-1, keepdims=True) + RMS_EPS) * g


def _sigmoid(x):
    return 1.0 / (1.0 + jnp.exp(-x))


def _mod_kernel(c_ref, w_ref, b_ref, o_ref):
    c = c_ref[...]
    a = (c * _sigmoid(c)).astype(jnp.bfloat16)
    o_ref[...] = jnp.dot(a, w_ref[...].astype(jnp.bfloat16),
                         preferred_element_type=jnp.float32) + b_ref[...]


def _modulation(cond, w_mod, b_mod):
    tn = 1024
    n = w_mod.shape[1]
    return pl.pallas_call(
        _mod_kernel,
        grid=(n // tn,),
        in_specs=[pl.BlockSpec((MOD_ROWS, D), lambda j: (0, 0)),
                  pl.BlockSpec((D, tn), lambda j: (0, j)),
                  pl.BlockSpec((1, tn), lambda j: (0, j))],
        out_specs=pl.BlockSpec((MOD_ROWS, tn), lambda j: (0, j)),
        out_shape=jax.ShapeDtypeStruct((MOD_ROWS, n), jnp.float32),
        compiler_params=_cparams("arbitrary"),
        name="adaln_modulation",
    )(cond, w_mod, b_mod.reshape(1, n))


def _mod_spec(which, n_ctx_blocks):
    def idx(b, i):
        r = jnp.where(i < n_ctx_blocks, MOD_CTX_ROW, b)
        return (r * 6 + which, 0, 0)
    return pl.BlockSpec((None, 1, D), idx)


def _norm_mod_kernel(x_ref, g_ref, sh_ref, sc_ref, o_ref):
    h = _rms(x_ref[...], g_ref[...]) * (1.0 + sc_ref[...]) + sh_ref[...]
    o_ref[...] = h.astype(o_ref.dtype)


META_LANES = 128
META_E = 0
META_RANK = TOP_K
META_GATE = 2 * TOP_K
HALF_D = D // 2
HI_MASK = 0xFFFF0000


def _pack_bf16_pair(lo, hi):
    ulo = pltpu.bitcast(lo.astype(jnp.bfloat16).astype(jnp.float32), jnp.uint32)
    uhi = pltpu.bitcast(hi.astype(jnp.bfloat16).astype(jnp.float32), jnp.uint32)
    return lax.shift_right_logical(ulo, jnp.uint32(16)) | (uhi & jnp.uint32(HI_MASK))


def _unpack_bf16_pair(w):
    lo = pltpu.bitcast(lax.shift_left(w, jnp.uint32(16)), jnp.float32).astype(jnp.bfloat16)
    hi = pltpu.bitcast(w & jnp.uint32(HI_MASK), jnp.float32).astype(jnp.bfloat16)
    return lo, hi


def _norm_mod_router_kernel(x_ref, g_ref, sh_ref, sc_ref, wr_ref, br_ref, o_ref, meta_ref, cnt_ref, run_ref):
    first = jnp.logical_and(pl.program_id(0) == 0, pl.program_id(1) == 0)

    @pl.when(first)
    def _():
        run_ref[...] = jnp.zeros_like(run_ref)

    h = _rms(x_ref[...], g_ref[...]) * (1.0 + sc_ref[...]) + sh_ref[...]
    o_ref[...] = _pack_bf16_pair(h[:, :HALF_D], h[:, HALF_D:])
    w = wr_ref[...]
    h_hi, w_hi = h.astype(jnp.bfloat16), w.astype(jnp.bfloat16)
    h_lo = (h - h_hi.astype(jnp.float32)).astype(jnp.bfloat16)
    w_lo = (w - w_hi.astype(jnp.float32)).astype(jnp.bfloat16)
    logits = (jnp.dot(h_hi, w_hi, preferred_element_type=jnp.float32)
              + jnp.dot(h_lo, w_hi, preferred_element_type=jnp.float32)
              + jnp.dot(h_hi, w_lo, preferred_element_type=jnp.float32) + br_ref[...])
    lane = lax.broadcasted_iota(jnp.int32, (ROW_TILE, N_EXPERTS), 1)
    vals, hots = [], []
    l = logits
    for _ in range(TOP_K):
        m = jnp.max(l, axis=-1, keepdims=True)
        idx = jnp.min(jnp.where(l == m, lane, N_EXPERTS), axis=-1, keepdims=True)
        hot = lane == idx
        vals.append(m)
        hots.append(hot)
        l = jnp.where(hot, -jnp.inf, l)
    es = [jnp.exp(v - vals[0]) for v in vals]
    den = es[0] + es[1] + es[2] + es[3]
    onehot = sum(hot.astype(jnp.float32) for hot in hots)
    r_i = lax.broadcasted_iota(jnp.int32, (ROW_TILE, ROW_TILE), 0)
    c_i = lax.broadcasted_iota(jnp.int32, (ROW_TILE, ROW_TILE), 1)
    lower = jnp.where(r_i > c_i, 1.0, 0.0).astype(jnp.bfloat16)
    before = jnp.dot(lower, onehot.astype(jnp.bfloat16), preferred_element_type=jnp.float32) + run_ref[0:1, 0:N_EXPERTS]
    mlane = lax.broadcasted_iota(jnp.int32, (ROW_TILE, META_LANES), 1)
    lane_f = lane.astype(jnp.float32)
    meta = jnp.zeros((ROW_TILE, META_LANES), jnp.float32)
    for k in range(TOP_K):
        hot_f = hots[k].astype(jnp.float32)
        e_k = jnp.sum(hot_f * lane_f, axis=-1, keepdims=True)
        rank_k = jnp.sum(hot_f * before, axis=-1, keepdims=True)
        meta = jnp.where(mlane == META_E + k, e_k, meta)
        meta = jnp.where(mlane == META_RANK + k, rank_k, meta)
        meta = jnp.where(mlane == META_GATE + k, es[k] / den, meta)
    meta_ref[...] = meta
    total = run_ref[0:1, 0:N_EXPERTS] + jnp.sum(onehot, axis=0, keepdims=True)
    run_ref[0:1, 0:N_EXPERTS] = total
    cnt_ref[...] = jnp.broadcast_to(run_ref[0:1, :], cnt_ref.shape)


def _norm_mod(x, g, mod, shift_idx, scale_idx, n_ctx_blocks, router=None, skip_rows=0):
    t = x.shape[1] - skip_rows
    skip = skip_rows // ROW_TILE
    grid = (B, t // ROW_TILE)
    row = pl.BlockSpec((None, ROW_TILE, D), lambda b, i: (b, i, 0))
    in_specs = [pl.BlockSpec((None, ROW_TILE, D), lambda b, i: (b, i + skip, 0)),
                pl.BlockSpec((1, D), lambda b, i: (0, 0)),
                _mod_spec(shift_idx, n_ctx_blocks), _mod_spec(scale_idx, n_ctx_blocks)]
    args = [x, g.reshape(1, D), mod, mod]
    if router is None:
        return pl.pallas_call(
            _norm_mod_kernel, grid=grid, in_specs=in_specs, out_specs=row,
            out_shape=jax.ShapeDtypeStruct((B, t, D), jnp.bfloat16),
            compiler_params=_cparams("parallel", "parallel"), name="norm_mod",
        )(*args)
    w_router, b_router = router
    in_specs += [pl.BlockSpec((D, N_EXPERTS), lambda b, i: (0, 0)),
                 pl.BlockSpec((1, N_EXPERTS), lambda b, i: (0, 0))]
    args += [w_router, b_router.reshape(1, N_EXPERTS)]
    return pl.pallas_call(
        _norm_mod_router_kernel, grid=grid, in_specs=in_specs,
        out_specs=[pl.BlockSpec((None, ROW_TILE, HALF_D), lambda b, i: (b, i, 0)),
                   pl.BlockSpec((None, ROW_TILE, META_LANES), lambda b, i: (b, i, 0)),
                   pl.BlockSpec((8, META_LANES), lambda b, i: (0, 0))],
        out_shape=[jax.ShapeDtypeStruct((B, t, HALF_D), jnp.uint32),
                   jax.ShapeDtypeStruct((B, t, META_LANES), jnp.float32),
                   jax.ShapeDtypeStruct((8, META_LANES), jnp.float32)],
        scratch_shapes=[pltpu.VMEM((8, META_LANES), jnp.float32)],
        compiler_params=_cparams("arbitrary", "arbitrary"), name="norm_mod_router",
    )(*args)


def _post_norm_kernel(x_ref, y_ref, g_ref, gt_ref, o_ref):
    o_ref[...] = x_ref[...] + gt_ref[...] * _rms(y_ref[...], g_ref[...])


def _stream_row(ctx_ref, x_ref):
    return jnp.where(pl.program_id(1) == 0, ctx_ref[...], x_ref[...])


def _split_row_specs():
    return [pl.BlockSpec((None, ROW_TILE, D), lambda b, i: (b, 0, 0)),
            pl.BlockSpec((None, ROW_TILE, D), lambda b, i: (b, jnp.maximum(i - 1, 0), 0))]


def _norm_mod_split_kernel(ctx_ref, x_ref, g_ref, sh_ref, sc_ref, o_ref):
    h = _rms(_stream_row(ctx_ref, x_ref), g_ref[...]) * (1.0 + sc_ref[...]) + sh_ref[...]
    o_ref[...] = h.astype(o_ref.dtype)


def _norm_mod_split(ctx, x, g, mod, shift_idx, scale_idx):
    row = pl.BlockSpec((None, ROW_TILE, D), lambda b, i: (b, i, 0))
    return pl.pallas_call(
        _norm_mod_split_kernel, grid=(B, TOK // ROW_TILE),
        in_specs=_split_row_specs() + [pl.BlockSpec((1, D), lambda b, i: (0, 0)),
                                       _mod_spec(shift_idx, 1), _mod_spec(scale_idx, 1)],
        out_specs=row,
        out_shape=jax.ShapeDtypeStruct((B, TOK, D), jnp.bfloat16),
        compiler_params=_cparams("parallel", "parallel"), name="norm_mod_split",
    )(ctx, x, g.reshape(1, D), mod, mod)


def _post_norm_split_kernel(ctx_ref, x_ref, y_ref, g_ref, gt_ref, o_ref):
    o_ref[...] = _stream_row(ctx_ref, x_ref) + gt_ref[...] * _rms(y_ref[...], g_ref[...])


def _post_norm_residual_split(ctx, x, y, g, mod, gate_idx):
    row = pl.BlockSpec((None, ROW_TILE, D), lambda b, i: (b, i, 0))
    return pl.pallas_call(
        _post_norm_split_kernel, grid=(B, TOK // ROW_TILE),
        in_specs=_split_row_specs() + [row, pl.BlockSpec((1, D), lambda b, i: (0, 0)), _mod_spec(gate_idx, 1)],
        out_specs=row,
        out_shape=jax.ShapeDtypeStruct((B, TOK, D), jnp.float32),
        compiler_params=_cparams("parallel", "parallel"), name="post_norm_residual_split",
    )(ctx, x, y, g.reshape(1, D), mod)


def _post_norm_residual(x, y, g, mod, gate_idx, n_ctx_blocks):
    t = y.shape[1]
    skip = (x.shape[1] - t) // ROW_TILE
    row = pl.BlockSpec((None, ROW_TILE, D), lambda b, i: (b, i, 0))
    xrow = pl.BlockSpec((None, ROW_TILE, D), lambda b, i: (b, i + skip, 0))
    return pl.pallas_call(
        _post_norm_kernel, grid=(B, t // ROW_TILE),
        in_specs=[xrow, row, pl.BlockSpec((1, D), lambda b, i: (0, 0)),
                  _mod_spec(gate_idx, n_ctx_blocks)],
        out_specs=xrow,
        out_shape=jax.ShapeDtypeStruct(x.shape, jnp.float32),
        input_output_aliases={0: 0},
        compiler_params=_cparams("parallel", "parallel"), name="post_norm_residual",
    )(x, y, g.reshape(1, D), mod)


def _matmul_kernel(a_ref, w_ref, o_ref):
    o_ref[...] = jnp.dot(a_ref[...], w_ref[...].astype(jnp.bfloat16),
                         preferred_element_type=jnp.float32).astype(o_ref.dtype)


def _matmul(a, w, tm, tn, out_dtype=jnp.float32, name="matmul"):
    m, k = a.shape
    n = w.shape[1]
    return pl.pallas_call(
        _matmul_kernel, grid=(m // tm, n // tn),
        in_specs=[pl.BlockSpec((tm, k), lambda i, j: (i, 0)),
                  pl.BlockSpec((k, tn), lambda i, j: (0, j))],
        out_specs=pl.BlockSpec((tm, tn), lambda i, j: (i, j)),
        out_shape=jax.ShapeDtypeStruct((m, n), out_dtype),
        compiler_params=_cparams("parallel", "arbitrary"), name=name,
    )(a, w)


def _glu_matmul_kernel(a_ref, wv_ref, wg_ref, o_ref):
    a = a_ref[...]
    val = jnp.dot(a, wv_ref[...].astype(jnp.bfloat16), preferred_element_type=jnp.float32)
    gate = jnp.dot(a, wg_ref[...].astype(jnp.bfloat16), preferred_element_type=jnp.float32)
    o_ref[...] = val * _sigmoid(gate)


def _glu_matmul(a, w_glu, tm, tn):
    m, k = a.shape
    n = w_glu.shape[1] // 2
    nj = n // tn
    return pl.pallas_call(
        _glu_matmul_kernel, grid=(m // tm, nj),
        in_specs=[pl.BlockSpec((tm, k), lambda i, j: (i, 0)),
                  pl.BlockSpec((k, tn), lambda i, j: (0, j)),
                  pl.BlockSpec((k, tn), lambda i, j: (0, nj + j))],
        out_specs=pl.BlockSpec((tm, tn), lambda i, j: (i, j)),
        out_shape=jax.ShapeDtypeStruct((m, n), jnp.float32),
        compiler_params=_cparams("parallel", "arbitrary"), name="glu_matmul",
    )(a, w_glu, w_glu)


IN_TN = 256
IN_ROPE_END = 8
IN_DV_END = 12
IN_GQ_END = 16
IN_GK_TILE = 16


def _inproj_kernel(a_ref, w_ref, cos_ref, sin_ref, gq_ref, gk_ref, o_ref):
    j = pl.program_id(1)
    acc = jnp.dot(a_ref[...], w_ref[...].astype(jnp.bfloat16), preferred_element_type=jnp.float32)

    def rope(x):
        return x * cos_ref[...] + pltpu.roll(x, HD // 2, 1) * sin_ref[...]

    def store(fn):
        for c in range(IN_TN // HD):
            o_ref[:, c * HD:(c + 1) * HD] = fn(acc[:, c * HD:(c + 1) * HD]).astype(o_ref.dtype)

    @pl.when(j < IN_ROPE_END)
    def _():
        store(rope)

    @pl.when(jnp.logical_or(jnp.logical_and(j >= IN_ROPE_END, j < IN_DV_END), j > IN_GK_TILE))
    def _():
        store(lambda x: x)

    @pl.when(jnp.logical_and(j >= IN_DV_END, j < IN_GQ_END))
    def _():
        store(lambda x: rope(_rms(x, gq_ref[...])))

    @pl.when(j == IN_GK_TILE)
    def _():
        store(lambda x: rope(_rms(x, gk_ref[...])))


def _inproj(h, w_in, cos2, sin2, g_q, g_k):
    m = h.shape[0]
    const = lambda i, j: (0, 0)
    return pl.pallas_call(
        _inproj_kernel, grid=(m // TOK, ATTN_IN // IN_TN),
        in_specs=[pl.BlockSpec((TOK, D), lambda i, j: (i, 0)),
                  pl.BlockSpec((D, IN_TN), lambda i, j: (0, j)),
                  pl.BlockSpec((TOK, HD), const), pl.BlockSpec((TOK, HD), const),
                  pl.BlockSpec((1, HD), const), pl.BlockSpec((1, HD), const)],
        out_specs=pl.BlockSpec((TOK, IN_TN), lambda i, j: (i, j)),
        out_shape=jax.ShapeDtypeStruct((m, ATTN_IN), jnp.bfloat16),
        compiler_params=_cparams("parallel", "arbitrary"), name="attn_inproj",
    )(h, w_in, cos2, sin2, g_q.reshape(1, HD), g_k.reshape(1, HD))


ATT_TQ = 256
ATT_SCALE_LOG2E = HD ** -0.5 * math.log2(math.e)


def _softmax_pv(q, k, v):
    s = lax.dot_general(q, k, (((1,), (1,)), ((), ())), preferred_element_type=jnp.float32)
    m = jnp.max(s, axis=-1, keepdims=True)
    e = jnp.exp2((s - m) * ATT_SCALE_LOG2E)
    l = jnp.sum(e, axis=-1, keepdims=True)
    return jnp.dot(e.astype(jnp.bfloat16), v, preferred_element_type=jnp.float32), l


def _diff_attn_kernel(lam_ref, q_ref, k_ref, v_ref, g_ref, o_ref, *, lambda_init):
    lp = lam_ref[...]
    lam = (jnp.exp(jnp.sum(lp[0:1] * lp[1:2], axis=-1, keepdims=True))
           - jnp.exp(jnp.sum(lp[2:3] * lp[3:4], axis=-1, keepdims=True)) + lambda_init)

    def run(nk):
        q = q_ref[...]
        k = k_ref[0:nk, :]
        v = v_ref[0:nk, :]
        pv1, l1 = _softmax_pv(q[:, :HD], k[:, :HD], v)
        pv2, l2 = _softmax_pv(q[:, HD:], k[:, HD:], v)
        o = pv1 / l1 - lam * (pv2 / l2)
        o_ref[...] = (_rms(o, g_ref[...]) * (1.0 - lambda_init)).astype(o_ref.dtype)

    @pl.when(pl.program_id(2) == 0)
    def _():
        run(CTX)

    @pl.when(pl.program_id(2) > 0)
    def _():
        run(TOK)


def _gqa_attn_kernel(q_ref, k_ref, v_ref, o_ref):
    def run(nk):
        k = k_ref[0:nk, :]
        v = v_ref[0:nk, :]
        for g in range(GQA_GROUP):
            pv, l = _softmax_pv(q_ref[:, g * HD:(g + 1) * HD], k, v)
            o_ref[:, g * HD:(g + 1) * HD] = (pv / l).astype(o_ref.dtype)

    @pl.when(pl.program_id(2) == 0)
    def _():
        run(CTX)

    @pl.when(pl.program_id(2) > 0)
    def _():
        run(TOK)


def _attention(p, lam_params, g_subln, lambda_init):
    nq = TOK // ATT_TQ
    dv = 2 * HD
    od = pl.pallas_call(
        functools.partial(_diff_attn_kernel, lambda_init=lambda_init),
        grid=(B, DIFF_HEADS, nq),
        in_specs=[pl.BlockSpec((4, HD), lambda b, h, i: (0, 0)),
                  pl.BlockSpec((None, ATT_TQ, dv), lambda b, h, i: (b, i, h)),
                  pl.BlockSpec((None, TOK, dv), lambda b, h, i: (b, 0, DIFF_HEADS + h)),
                  pl.BlockSpec((None, TOK, dv), lambda b, h, i: (b, 0, 2 * DIFF_HEADS + h)),
                  pl.BlockSpec((1, dv), lambda b, h, i: (0, 0))],
        out_specs=pl.BlockSpec((None, ATT_TQ, dv), lambda b, h, i: (b, i, h)),
        out_shape=jax.ShapeDtypeStruct((B, TOK, DIFF_HEADS * dv), jnp.bfloat16),
        compiler_params=_cparams("parallel", "parallel", "arbitrary"), name="diff_attention",
    )(lam_params, p, p, p, g_subln.reshape(1, dv))
    gq_w = GQA_GROUP * HD
    gq0 = 3072 // gq_w
    gk0 = 4096 // HD
    gv0 = 4352 // HD
    og = pl.pallas_call(
        _gqa_attn_kernel,
        grid=(B, GQA_KV_HEADS, nq),
        in_specs=[pl.BlockSpec((None, ATT_TQ, gq_w), lambda b, n, i: (b, i, gq0 + n)),
                  pl.BlockSpec((None, TOK, HD), lambda b, n, i: (b, 0, gk0 + n)),
                  pl.BlockSpec((None, TOK, HD), lambda b, n, i: (b, 0, gv0 + n))],
        out_specs=pl.BlockSpec((None, ATT_TQ, gq_w), lambda b, n, i: (b, i, n)),
        out_shape=jax.ShapeDtypeStruct((B, TOK, GQA_Q_HEADS * HD), jnp.bfloat16),
        compiler_params=_cparams("parallel", "parallel", "arbitrary"), name="gqa_attention",
    )(p, p, p)
    return jnp.concatenate([od, og], axis=-1)


def _rope_tables():
    rows = SEQ // GRID_W
    row_id, col_id = jnp.meshgrid(jnp.arange(rows), jnp.arange(GRID_W), indexing="ij")
    inv_freq = ROPE_THETA ** (-jnp.arange(ROPE_FREQS, dtype=jnp.float32) / ROPE_FREQS)
    ang = jnp.concatenate([row_id.reshape(-1, 1) * inv_freq, col_id.reshape(-1, 1) * inv_freq], axis=-1)
    cos, sin = jnp.cos(ang), jnp.sin(ang)
    cos2 = jnp.concatenate([cos, cos], axis=-1)
    sin2 = jnp.concatenate([-sin, sin], axis=-1)
    cos2 = jnp.concatenate([jnp.ones((CTX, HD), jnp.float32), cos2], axis=0)
    sin2 = jnp.concatenate([jnp.zeros((CTX, HD), jnp.float32), sin2], axis=0)
    return cos2, sin2


MOE_TM = 512
MOE_SUB = 256
MOE_TF = 1024
DISPATCH_TT = 512
COMBINE_TT = 256


def _for_valid_rows(rows, compute, o_ref):
    half = MOE_SUB // 2
    width = o_ref.shape[1]
    for s in range(MOE_TM // MOE_SUB):
        base = s * MOE_SUB

        @pl.when(rows > base + half)
        def _():
            compute(slice(base, base + MOE_SUB))

        @pl.when(jnp.logical_and(rows > base, rows <= base + half))
        def _():
            compute(slice(base, base + half))
            o_ref[base + half:base + MOE_SUB, :] = jnp.zeros((half, width), o_ref.dtype)

        @pl.when(rows <= base)
        def _():
            o_ref[base:base + MOE_SUB, :] = jnp.zeros((MOE_SUB, width), o_ref.dtype)


def _mxu_dot(a_bf16, w_f32):
    return lax.dot_general(a_bf16, w_f32, (((1,), (0,)), ((), ())), preferred_element_type=jnp.float32)


def _moe_up_kernel(be_ref, first_ref, rows_ref, nxt_ref, x_ref, w_hbm, bg_ref, bl_ref, o_ref,
                   wst, slot_ref, sem):
    f = pl.program_id(0)
    b = pl.program_id(1)
    nf = pl.num_programs(0)

    def copies(e, ff, slot):
        col = pl.multiple_of(ff * MOE_TF, MOE_TF)
        return [pltpu.make_async_copy(w_hbm.at[e, :, pl.ds(part * D_FF + col, MOE_TF)],
                                      wst.at[slot, part], sem.at[slot, part]) for part in range(2)]

    @pl.when(first_ref[b] == 1)
    def _():
        @pl.when(jnp.logical_and(f == 0, b == 0))
        def _():
            slot_ref[0] = 1
            for c in copies(be_ref[0], 0, 0):
                c.start()

        slot = 1 - slot_ref[0]
        slot_ref[0] = slot
        for c in copies(be_ref[b], f, slot):
            c.wait()
        e_next = nxt_ref[b]

        @pl.when(e_next >= 0)
        def _():
            for c in copies(e_next, f, 1 - slot):
                c.start()

        @pl.when(jnp.logical_and(e_next < 0, f + 1 < nf))
        def _():
            for c in copies(be_ref[0], f + 1, 1 - slot):
                c.start()

    def compute(sl):
        slot = slot_ref[0]
        lo, hi = _unpack_bf16_pair(x_ref[sl, :])
        glu = (_mxu_dot(lo, wst[slot, 0, :HALF_D, :]) + _mxu_dot(hi, wst[slot, 0, HALF_D:, :]) + bg_ref[...])
        lin = (_mxu_dot(lo, wst[slot, 1, :HALF_D, :]) + _mxu_dot(hi, wst[slot, 1, HALF_D:, :]) + bl_ref[...])
        glu = jnp.minimum(glu, SWIGLU_LIMIT)
        lin = jnp.clip(lin, -SWIGLU_LIMIT, SWIGLU_LIMIT)
        o_ref[sl, :] = (glu * _sigmoid(SWIGLU_ALPHA * glu) * (lin + 1.0)).astype(o_ref.dtype)

    _for_valid_rows(rows_ref[b], compute, o_ref)


def _moe_down_kernel(be_ref, first_ref, rows_ref, nxt_ref, a_ref, w_hbm, bias_ref, o_ref, wst, slot_ref, sem):
    b = pl.program_id(0)

    def copy(e, slot):
        return pltpu.make_async_copy(w_hbm.at[e], wst.at[slot], sem.at[slot])

    @pl.when(first_ref[b] == 1)
    def _():
        @pl.when(b == 0)
        def _():
            slot_ref[0] = 1
            copy(be_ref[0], 0).start()

        slot = 1 - slot_ref[0]
        slot_ref[0] = slot
        copy(be_ref[b], slot).wait()
        e_next = nxt_ref[b]

        @pl.when(e_next >= 0)
        def _():
            copy(e_next, 1 - slot).start()

    def compute(sl):
        o_ref[sl, :] = _mxu_dot(a_ref[sl, :], wst[slot_ref[0]]) + bias_ref[...]

    _for_valid_rows(rows_ref[b], compute, o_ref)


def _moe_experts(x_sorted, tables, w_gate_up, b_gate_up, w_down, b_down):
    r = x_sorted.shape[0]
    nb = r // MOE_TM
    nf = D_FF // MOE_TF
    bgu = b_gate_up.reshape(N_EXPERTS, 1, 2 * D_FF)
    act = pl.pallas_call(
        _moe_up_kernel,
        grid_spec=pltpu.PrefetchScalarGridSpec(
            num_scalar_prefetch=4, grid=(nf, nb),
            in_specs=[pl.BlockSpec((MOE_TM, HALF_D), lambda f, b, be, fi, ro, nx: (b, 0)),
                      pl.BlockSpec(memory_space=pl.ANY),
                      pl.BlockSpec((None, 1, MOE_TF), lambda f, b, be, fi, ro, nx: (be[b], 0, f)),
                      pl.BlockSpec((None, 1, MOE_TF), lambda f, b, be, fi, ro, nx: (be[b], 0, nf + f))],
            out_specs=pl.BlockSpec((MOE_TM, MOE_TF), lambda f, b, be, fi, ro, nx: (b, f)),
            scratch_shapes=[pltpu.VMEM((2, 2, D, MOE_TF), jnp.float32), pltpu.SMEM((1,), jnp.int32),
                            pltpu.SemaphoreType.DMA((2, 2))]),
        out_shape=jax.ShapeDtypeStruct((r, D_FF), jnp.bfloat16),
        compiler_params=_cparams("arbitrary", "arbitrary"), name="moe_gate_up",
    )(*tables, x_sorted, w_gate_up, bgu, bgu)
    return pl.pallas_call(
        _moe_down_kernel,
        grid_spec=pltpu.PrefetchScalarGridSpec(
            num_scalar_prefetch=4, grid=(nb,),
            in_specs=[pl.BlockSpec((MOE_TM, D_FF), lambda b, be, fi, ro, nx: (b, 0)),
                      pl.BlockSpec(memory_space=pl.ANY),
                      pl.BlockSpec((None, 1, D), lambda b, be, fi, ro, nx: (be[b], 0, 0))],
            out_specs=pl.BlockSpec((MOE_TM, D), lambda b, be, fi, ro, nx: (b, 0)),
            scratch_shapes=[pltpu.VMEM((2, D_FF, D), jnp.float32), pltpu.SMEM((1,), jnp.int32),
                            pltpu.SemaphoreType.DMA((2,))]),
        out_shape=jax.ShapeDtypeStruct((r, D), jnp.float32),
        compiler_params=_cparams("arbitrary"), name="moe_down",
    )(*tables, act, w_down, b_down.reshape(N_EXPERTS, 1, D))


def _dispatch_kernel(dest_ref, hp_ref, xs_in_hbm, xs_hbm, idx_smem, sem_idx, sem_rows):
    del xs_in_hbm
    idx_copy = pltpu.make_async_copy(dest_ref, idx_smem, sem_idx)
    idx_copy.start()
    idx_copy.wait()

    def issue(t, carry):
        for k in range(TOP_K):
            d = idx_smem[0, 0, t * TOP_K + k]
            pltpu.make_async_copy(hp_ref.at[t], xs_hbm.at[d], sem_rows).start(priority=k % 2)
        return carry

    lax.fori_loop(0, DISPATCH_TT, issue, 0, unroll=4)
    for k in range(TOP_K):
        pltpu.make_async_copy(hp_ref, xs_hbm.at[pl.ds(0, DISPATCH_TT)], sem_rows).wait()


def _dispatch(dest, hp, n_rows):
    n_tok = hp.shape[0]
    n = n_tok // DISPATCH_TT
    width = DISPATCH_TT * TOP_K
    zeros = jnp.zeros((n_rows, HALF_D), jnp.uint32)
    return pl.pallas_call(
        _dispatch_kernel, grid=(n,),
        in_specs=[pl.BlockSpec((1, 1, width), lambda i: (i, 0, 0)),
                  pl.BlockSpec((DISPATCH_TT, HALF_D), lambda i: (i, 0)), pl.BlockSpec(memory_space=pl.ANY)],
        out_specs=pl.BlockSpec(memory_space=pl.ANY),
        out_shape=jax.ShapeDtypeStruct((n_rows, HALF_D), jnp.uint32),
        scratch_shapes=[pltpu.SMEM((1, 1, width), jnp.int32), pltpu.SemaphoreType.DMA, pltpu.SemaphoreType.DMA],
        input_output_aliases={2: 0},
        compiler_params=_cparams("arbitrary"), name="moe_dispatch",
    )(dest.reshape(n, 1, width), hp, zeros)


def _combine_kernel(dcur_ref, dnext_ref, out_hbm, meta_ref, x_ref, g_ref, gt_ref, o_ref,
                    idx_smem, buf, sem_idx, sem_rows):
    i = pl.program_id(0)
    n = pl.num_programs(0)

    def row_copy(d, slot, k, t):
        return pltpu.make_async_copy(out_hbm.at[d], buf.at[slot, k, t], sem_rows.at[slot])

    def gather(d_ref, slot):
        idx_copy = pltpu.make_async_copy(d_ref, idx_smem, sem_idx)
        idx_copy.start()
        idx_copy.wait()

        def issue(t, carry):
            for k in range(TOP_K):
                row_copy(idx_smem[0, 0, t * TOP_K + k], slot, k, t).start(priority=k % 2)
            return carry
        lax.fori_loop(0, COMBINE_TT, issue, 0, unroll=4)

    @pl.when(i == 0)
    def _():
        gather(dcur_ref, 0)

    @pl.when(i + 1 < n)
    def _():
        gather(dnext_ref, (i + 1) % 2)

    slot = i % 2
    pltpu.make_async_copy(buf.at[slot], buf.at[slot], sem_rows.at[slot]).wait()

    meta = meta_ref[...]
    f = meta[:, META_GATE:META_GATE + 1] * buf[slot, 0]
    for k in range(1, TOP_K):
        f = f + meta[:, META_GATE + k:META_GATE + k + 1] * buf[slot, k]
    o_ref[...] = x_ref[...] + gt_ref[...] * _rms(f, g_ref[...])


def _combine(dest, out_sorted, meta, x, g, mod, gate_idx, rows_per_batch, ctx_rows):
    n_tok = dest.shape[0]
    n = n_tok // COMBINE_TT
    width = COMBINE_TT * TOP_K
    per_batch = rows_per_batch // COMBINE_TT
    x_per_batch = x.shape[0] // B // COMBINE_TT
    skip = x_per_batch - per_batch
    ctx_blocks = ctx_rows // COMBINE_TT

    def gate_row(i):
        r = jnp.where(i % per_batch < ctx_blocks, MOD_CTX_ROW, i // per_batch)
        return (r * 6 + gate_idx, 0, 0)

    row = pl.BlockSpec((COMBINE_TT, D), lambda i: (i, 0))
    xrow = pl.BlockSpec((COMBINE_TT, D), lambda i: ((i // per_batch) * x_per_batch + skip + i % per_batch, 0))
    return pl.pallas_call(
        _combine_kernel, grid=(n,),
        in_specs=[pl.BlockSpec((1, 1, width), lambda i: (i, 0, 0)),
                  pl.BlockSpec((1, 1, width), lambda i: (jnp.minimum(i + 1, n - 1), 0, 0)),
                  pl.BlockSpec(memory_space=pl.ANY),
                  pl.BlockSpec((COMBINE_TT, META_LANES), lambda i: (i, 0)),
                  xrow, pl.BlockSpec((1, D), lambda i: (0, 0)),
                  pl.BlockSpec((None, 1, D), gate_row)],
        out_specs=row,
        out_shape=jax.ShapeDtypeStruct((n_tok, D), jnp.float32),
        scratch_shapes=[pltpu.SMEM((1, 1, width), jnp.int32),
                        pltpu.VMEM((2, TOP_K, COMBINE_TT, D), jnp.float32),
                        pltpu.SemaphoreType.DMA, pltpu.SemaphoreType.DMA((2,))],
        input_output_aliases={4: 0} if skip == 0 else {},
        compiler_params=_cparams("arbitrary"), name="moe_combine",
    )(dest.reshape(n, 1, width), dest.reshape(n, 1, width), out_sorted, meta, x, g.reshape(1, D), mod)


def _moe_block_tables(counts, n_assign):
    padded = (counts + MOE_TM - 1) // MOE_TM * MOE_TM
    pad_end = jnp.cumsum(padded)
    pad_start = pad_end - padded
    nb = -(-(n_assign + N_EXPERTS * (MOE_TM - 1)) // MOE_TM)
    block_start = jnp.arange(nb, dtype=jnp.int32) * MOE_TM
    block_e_raw = jnp.minimum(jnp.sum(block_start[:, None] >= pad_end[None, :], axis=1), N_EXPERTS - 1)
    block_e_raw = block_e_raw.astype(jnp.int32)
    onehot_e = block_e_raw[:, None] == jnp.arange(N_EXPERTS)[None, :]
    valid_end = jnp.sum(jnp.where(onehot_e, (pad_start + counts)[None, :], 0), axis=1)
    block_rows = jnp.clip(valid_end - block_start, 0, MOE_TM).astype(jnp.int32)
    block_rows = jnp.where(block_start < pad_end[-1], block_rows, 0)
    last_e = jnp.max(jnp.where(block_rows > 0, block_e_raw, 0))
    block_e = jnp.where(block_rows > 0, block_e_raw, last_e)
    block_first = jnp.concatenate([jnp.ones((1,), jnp.int32),
                                   (block_e[1:] != block_e[:-1]).astype(jnp.int32)])
    idx = jnp.arange(nb, dtype=jnp.int32)
    first_pos = jnp.where(block_first == 1, idx, nb)
    later = jnp.where(idx[None, :] > idx[:, None], first_pos[None, :], nb)
    next_pos = jnp.min(later, axis=1)
    next_e = jnp.sum(jnp.where(idx[None, :] == next_pos[:, None], block_e[None, :], 0), axis=1)
    block_next = jnp.where(next_pos < nb, next_e, -1).astype(jnp.int32)
    return pad_start, nb * MOE_TM, (block_e, block_first, block_rows, block_next)


def _moe_ffn(x_res, hp, meta, cnt, g_post, mod, gate_idx, rows_per_batch, ctx_rows,
             w_gate_up, b_gate_up, w_down, b_down):
    n_tok = hp.shape[0]
    counts = cnt[0, :N_EXPERTS].astype(jnp.int32)
    pad_start, n_rows, tables = _moe_block_tables(counts, n_tok * TOP_K)
    top_e = meta[:, META_E:META_E + TOP_K].astype(jnp.int32)
    rank = meta[:, META_RANK:META_RANK + TOP_K].astype(jnp.int32)
    hot = top_e[:, :, None] == jnp.arange(N_EXPERTS)[None, None, :]
    dest = jnp.sum(jnp.where(hot, pad_start[None, None, :], 0), axis=-1) + rank
    x_sorted = _dispatch(dest, hp, n_rows)
    out_sorted = _moe_experts(x_sorted, tables, w_gate_up, b_gate_up, w_down, b_down)
    return _combine(dest, out_sorted, meta, x_res, g_post, mod, gate_idx, rows_per_batch, ctx_rows)


S5_GB = 8
S5_LANES = SSM_CHUNK * SSM_GROUP
S5_NK_CTX = CTX // SSM_CHUNK
S5_NK_LAT = SEQ // SSM_CHUNK
S5_PAIR = 2 * B


def _s5_direction(reverse, uc_ref, ul_ref, wb_ref, m_ref, wc_ref, coef_ref, y_ref,
                  s1c, s2c, s1l, s2l, xin, accumulate):
    half = 2 * SSM_STATE
    for g in range(S5_GB):
        sc = jnp.dot(uc_ref[g], wb_ref[g], preferred_element_type=jnp.float32)
        s1c[g] = sc[:, :half]
        s2c[g] = sc[:, half:]
        sl = jnp.dot(ul_ref[g], wb_ref[g], preferred_element_type=jnp.float32)
        s1l[g] = sl[:, :half]
        s2l[g] = sl[:, half:]

    lower = lax.broadcasted_iota(jnp.int32, (S5_PAIR, half), 0) < B
    p1 = [jnp.broadcast_to(coef_ref[g, 0:1, :], (S5_PAIR, half)) for g in range(S5_GB)]
    p2 = [jnp.broadcast_to(coef_ref[g, 1:2, :], (S5_PAIR, half)) for g in range(S5_GB)]

    def tile_step(g, t1, t2, v1, v2):
        y1a = p1[g] * v1 + p2[g] * v2 + t1
        y1b = p1[g] * v2 - p2[g] * v1 + t2
        r1a = pltpu.roll(y1a, B, 0)
        r1b = pltpu.roll(y1b, B, 0)
        y2a = p1[g] * r1a + p2[g] * r1b + t1
        y2b = p1[g] * r1b - p2[g] * r1a + t2
        r2a = pltpu.roll(y2a, B, 0)
        r2b = pltpu.roll(y2b, B, 0)
        if not reverse:
            x_in = jnp.where(lower, v1, r1a)
            return x_in, jnp.where(lower, r2a, y2a), jnp.where(lower, r2b, y2b)
        x_in = jnp.where(lower, r1a, v1)
        return x_in, jnp.where(lower, y2a, r2a), jnp.where(lower, y2b, r2b)

    def scan(s1, s2, n_tiles, state, record):
        def body(j, carry):
            jj = (n_tiles - 1 - j) if reverse else j
            r0 = pl.multiple_of(jj * S5_PAIR, S5_PAIR)
            new = []
            for g in range(S5_GB):
                v1, v2 = carry[2 * g], carry[2 * g + 1]
                x_in, v1, v2 = tile_step(g, s1[g, pl.ds(r0, S5_PAIR), :], s2[g, pl.ds(r0, S5_PAIR), :], v1, v2)
                if record:
                    xin[g, pl.ds(r0, S5_PAIR), :] = x_in
                new += [v1, v2]
            return tuple(new)
        return lax.fori_loop(0, n_tiles, body, state)

    zero = jnp.zeros((S5_PAIR, half), jnp.float32)
    state = tuple(zero for _ in range(2 * S5_GB))
    state = scan(s1c, s2c, S5_NK_CTX * B // S5_PAIR, state, False)
    scan(s1l, s2l, S5_NK_LAT * B // S5_PAIR, state, True)

    for g in range(S5_GB):
        y = (jnp.dot(ul_ref[g], m_ref[g], preferred_element_type=jnp.float32)
             + jnp.dot(xin[g].astype(jnp.bfloat16), wc_ref[g], preferred_element_type=jnp.float32))
        if accumulate:
            y_ref[g] = y_ref[g] + y
        else:
            y_ref[g] = y


def _s5_kernel(uc_ref, ul_ref, wb_ref, m_ref, wc_ref, coef_ref, y_ref, s1c, s2c, s1l, s2l, xin):
    args = (uc_ref, ul_ref, wb_ref, m_ref, wc_ref, coef_ref, y_ref, s1c, s2c, s1l, s2l, xin)

    @pl.when(pl.program_id(1) == 0)
    def _():
        _s5_direction(False, *args, accumulate=False)

    @pl.when(pl.program_id(1) == 1)
    def _():
        _s5_direction(True, *args, accumulate=True)


def _s5_scan(u_ctx, u_lat, wb, m, wc, coef):
    rc, rl = u_ctx.shape[1], u_lat.shape[1]
    half = 2 * SSM_STATE
    wspec = lambda k, n: pl.BlockSpec((None, S5_GB, k, n), lambda gi, d: (d, gi, 0, 0))
    return pl.pallas_call(
        _s5_kernel, grid=(SSM_GROUPS // S5_GB, 2),
        in_specs=[pl.BlockSpec((S5_GB, rc, S5_LANES), lambda gi, d: (gi, 0, 0)),
                  pl.BlockSpec((S5_GB, rl, S5_LANES), lambda gi, d: (gi, 0, 0)),
                  wspec(S5_LANES, 2 * half), wspec(S5_LANES, S5_LANES), wspec(half, S5_LANES),
                  wspec(2, half)],
        out_specs=pl.BlockSpec((S5_GB, rl, S5_LANES), lambda gi, d: (gi, 0, 0)),
        out_shape=jax.ShapeDtypeStruct((SSM_GROUPS, rl, S5_LANES), jnp.float32),
        scratch_shapes=[pltpu.VMEM((S5_GB, rc, half), jnp.float32), pltpu.VMEM((S5_GB, rc, half), jnp.float32),
                        pltpu.VMEM((S5_GB, rl, half), jnp.float32), pltpu.VMEM((S5_GB, rl, half), jnp.float32),
                        pltpu.VMEM((S5_GB, rl, half), jnp.float32)],
        compiler_params=_cparams("parallel", "arbitrary"), name="s5_scan",
    )(u_ctx, u_lat, wb, m, wc, coef)


def _s5_matrices(a_re, a_im, b_re, b_im, c_re, c_im, log_dt, reverse):
    hp = lax.Precision.HIGHEST
    n = SSM_CHUNK
    g, p = SSM_GROUPS, SSM_STATE
    dt = jnp.exp(log_dt)[:, None]
    mag = jnp.exp(a_re * dt)
    ab_re, ab_im = mag * jnp.cos(a_im * dt), mag * jnp.sin(a_im * dt)
    den = a_re * a_re + a_im * a_im
    f_re = ((ab_re - 1.0) * a_re + ab_im * a_im) / den
    f_im = (ab_im * a_re - (ab_re - 1.0) * a_im) / den
    b_re_t, b_im_t = b_re.transpose(0, 2, 1), b_im.transpose(0, 2, 1)
    bb_re = f_re[:, None, :] * b_re_t - f_im[:, None, :] * b_im_t
    bb_im = f_re[:, None, :] * b_im_t + f_im[:, None, :] * b_re_t
    tau = jnp.arange(n + 1, dtype=jnp.float32)[None, None, :]
    ang_re, ang_im = (a_re * dt)[:, :, None], (a_im * dt)[:, :, None]
    pmag = jnp.exp(tau * ang_re)
    pw_re, pw_im = pmag * jnp.cos(tau * ang_im), pmag * jnp.sin(tau * ang_im)
    c_re_t, c_im_t = c_re.transpose(0, 2, 1), c_im.transpose(0, 2, 1)

    def c_times_power(exps):
        q_re, q_im = pw_re[:, :, exps], pw_im[:, :, exps]
        re = c_re_t[:, :, None, :] * q_re[:, :, :, None] - c_im_t[:, :, None, :] * q_im[:, :, :, None]
        im = c_re_t[:, :, None, :] * q_im[:, :, :, None] + c_im_t[:, :, None, :] * q_re[:, :, :, None]
        return re.reshape(g, p, -1), im.reshape(g, p, -1)

    lags = np.arange(n)[::-1] if reverse else np.arange(n)
    ca_re, ca_im = c_times_power(lags)
    kst = jnp.matmul(jnp.concatenate([bb_re, bb_im], axis=-1), jnp.concatenate([ca_re, -ca_im], axis=1),
                     precision=hp)
    zeros = jnp.zeros_like(kst)
    if reverse:
        z = jnp.concatenate([kst, zeros], axis=-1)
        m = jnp.stack([z[:, :, (n - 1 - s) * SSM_GROUP:(n - 1 - s) * SSM_GROUP + S5_LANES] for s in range(n)], axis=1)
    else:
        z = jnp.concatenate([zeros, kst], axis=-1)
        m = jnp.stack([z[:, :, S5_LANES - s * SSM_GROUP:2 * S5_LANES - s * SSM_GROUP] for s in range(n)], axis=1)
    m = m.reshape(g, S5_LANES, S5_LANES)
    e_idx = np.arange(n) if reverse else (n - 1 - np.arange(n))
    ae_re = pw_re[:, :, e_idx].transpose(0, 2, 1)[:, :, None, :]
    ae_im = pw_im[:, :, e_idx].transpose(0, 2, 1)[:, :, None, :]
    wb_re = (ae_re * bb_re[:, None] - ae_im * bb_im[:, None]).reshape(g, S5_LANES, p)
    wb_im = (ae_re * bb_im[:, None] + ae_im * bb_re[:, None]).reshape(g, S5_LANES, p)
    wb = jnp.concatenate([wb_re, wb_im, wb_im, wb_re], axis=-1)
    f_idx = (n - np.arange(n)) if reverse else (np.arange(n) + 1)
    cf_re, cf_im = c_times_power(f_idx)
    wc = jnp.concatenate([cf_re, -cf_im], axis=1)
    an_re, an_im = pw_re[:, :, n], pw_im[:, :, n]
    coef = jnp.stack([jnp.concatenate([an_re, an_re], axis=-1),
                      jnp.concatenate([-an_im, an_im], axis=-1)], axis=1)
    return wb.astype(jnp.bfloat16), m.astype(jnp.bfloat16), wc.astype(jnp.bfloat16), coef


def _s5_chunks(h):
    t = h.shape[1]
    nk = t // SSM_CHUNK
    u = h.reshape(B, nk, SSM_CHUNK, SSM_GROUPS, SSM_GROUP).transpose(3, 1, 0, 2, 4)
    return u.reshape(SSM_GROUPS, nk * B, S5_LANES)


def _s5_unchunk(y):
    nk = y.shape[1] // B
    y = y.reshape(SSM_GROUPS, nk, B, SSM_CHUNK, SSM_GROUP).transpose(2, 1, 3, 0, 4)
    return y.reshape(B, nk * SSM_CHUNK, D)


def _s5_post_kernel(x_ref, y_ref, g_ref, sh_ref, sc_ref, d_ref, o_ref):
    h = _rms(x_ref[...], g_ref[...]) * (1.0 + sc_ref[...]) + sh_ref[...]
    y = d_ref[...] * h + y_ref[...]
    z = 0.5 * y * (1.0 + jnp.tanh(math.sqrt(2.0 / math.pi) * (y + 0.044715 * (y * y * y))))
    o_ref[...] = z.astype(o_ref.dtype)


def _s5_post(xs, y_ssm, g, mod, d_skip):
    skip = CTX // ROW_TILE
    row = pl.BlockSpec((None, ROW_TILE, D), lambda b, i: (b, i, 0))
    xrow = pl.BlockSpec((None, ROW_TILE, D), lambda b, i: (b, i + skip, 0))
    vec = pl.BlockSpec((1, D), lambda b, i: (0, 0))
    return pl.pallas_call(
        _s5_post_kernel, grid=(B, SEQ // ROW_TILE),
        in_specs=[xrow, row, vec, _mod_spec(0, 0), _mod_spec(1, 0), vec],
        out_specs=row,
        out_shape=jax.ShapeDtypeStruct((B, SEQ, D), jnp.bfloat16),
        compiler_params=_cparams("parallel", "parallel"), name="s5_skip_gelu",
    )(xs, y_ssm, g.reshape(1, D), mod, mod, d_skip.reshape(1, D))


def _layer_modulation(c, c_ctx, w_mod, b_mod):
    cond = jnp.concatenate([c, c_ctx[None, :], jnp.zeros((MOD_ROWS - B - 1, D), jnp.float32)], axis=0)
    return _modulation(cond, w_mod, b_mod).reshape(MOD_ROWS * 6, 1, D)


def kernel(x, c, ctx, c_ctx, l0_w_mod, l0_b_mod, l0_g_pre_mix, l0_g_post_mix, l0_g_pre_ffn, l0_g_post_ffn, l0_w_in, l0_w_out, l0_lambda_q1, l0_lambda_k1, l0_lambda_q2, l0_lambda_k2, l0_g_subln, l0_g_qnorm, l0_g_knorm, l0_w_router, l0_b_router, l0_w_gate_up, l0_b_gate_up, l0_w_down, l0_b_down, l1_w_mod, l1_b_mod, l1_g_pre_mix, l1_g_post_mix, l1_g_pre_ffn, l1_g_post_ffn, l1_ssm_a_re, l1_ssm_a_im, l1_ssm_b_re, l1_ssm_b_im, l1_ssm_c_re, l1_ssm_c_im, l1_ssm_log_dt, l1_ssm_d, l1_w_glu, l1_w_router, l1_b_router, l1_w_gate_up, l1_b_gate_up, l1_w_down, l1_b_down):
    mod = _layer_modulation(c, c_ctx, l0_w_mod, l0_b_mod)
    h = _norm_mod_split(ctx, x, l0_g_pre_mix, mod, 0, 1)
    cos2, sin2 = _rope_tables()
    p = _inproj(h.reshape(B * TOK, D), l0_w_in, cos2, sin2, l0_g_qnorm, l0_g_knorm)
    lam_params = jnp.stack([l0_lambda_q1, l0_lambda_k1, l0_lambda_q2, l0_lambda_k2])
    lambda_init = 0.8 - 0.6 * math.exp(-0.3 * 0)
    att = _attention(p.reshape(B, TOK, ATTN_IN), lam_params, l0_g_subln, lambda_init)
    y = _matmul(att.reshape(B * TOK, D), l0_w_out, 1024, 512, name="attn_outproj")
    xs = _post_norm_residual_split(ctx, x, y.reshape(B, TOK, D), l0_g_post_mix, mod, 2)
    hp, meta, cnt = _norm_mod(xs, l0_g_pre_ffn, mod, 3, 4, 1, router=(l0_w_router, l0_b_router))
    xs = _moe_ffn(xs.reshape(B * TOK, D), hp.reshape(B * TOK, HALF_D), meta.reshape(B * TOK, META_LANES), cnt,
                  l0_g_post_ffn, mod, 5, TOK, CTX,
                  l0_w_gate_up, l0_b_gate_up, l0_w_down, l0_b_down).reshape(B, TOK, D)

    mod = _layer_modulation(c, c_ctx, l1_w_mod, l1_b_mod)
    h = _norm_mod(xs, l1_g_pre_mix, mod, 0, 1, 1)
    u_ctx = _s5_chunks(h[:, :CTX])
    u_lat = _s5_chunks(h[:, CTX:])
    mats = [_s5_matrices(l1_ssm_a_re[d], l1_ssm_a_im[d], l1_ssm_b_re[d], l1_ssm_b_im[d],
                         l1_ssm_c_re[d], l1_ssm_c_im[d], l1_ssm_log_dt[d], reverse=bool(d))
            for d in range(2)]
    wb, m, wc, coef = (jnp.stack([mats[0][i], mats[1][i]]) for i in range(4))
    y_ssm = _s5_unchunk(_s5_scan(u_ctx, u_lat, wb, m, wc, coef))
    z = _s5_post(xs, y_ssm, l1_g_pre_mix, mod, l1_ssm_d)
    y = _glu_matmul(z.reshape(B * SEQ, D), l1_w_glu, 1024, 512)
    xs = _post_norm_residual(xs, y.reshape(B, SEQ, D), l1_g_post_mix, mod, 2, 0)
    hp, meta, cnt = _norm_mod(xs, l1_g_pre_ffn, mod, 3, 4, 0, router=(l1_w_router, l1_b_router), skip_rows=CTX)
    return _moe_ffn(xs.reshape(B * TOK, D), hp.reshape(B * SEQ, HALF_D), meta.reshape(B * SEQ, META_LANES), cnt,
                    l1_g_post_ffn, mod, 5, SEQ, 0,
                    l1_w_gate_up, l1_b_gate_up, l1_w_down, l1_b_down).reshape(B, SEQ, D)
```

```python
import functools
import math

import jax
import jax.numpy as jnp
import numpy as np
from jax import lax
from jax.experimental import pallas as pl
from jax.experimental.pallas import tpu as pltpu

D = 2048
B = 4
SEQ = 2048
CTX = 256
TOK = CTX + SEQ
GRID_W = 64
HD = 128
DIFF_HEADS = 4
GQA_Q_HEADS = 8
GQA_KV_HEADS = 2
GQA_GROUP = GQA_Q_HEADS // GQA_KV_HEADS
ROPE_THETA = 10000.0
ROPE_FREQS = HD // 4
ATTN_IN = 4608
N_EXPERTS = 32
TOP_K = 4
D_FF = D
SWIGLU_LIMIT = 7.0
SWIGLU_ALPHA = 1.702
RMS_EPS = 1e-6
SSM_GROUP = 16
SSM_STATE = 64
SSM_GROUPS = D // SSM_GROUP
SSM_CHUNK = 16

ROW_TILE = 256
MOD_ROWS = 8
MOD_CTX_ROW = B

V7X_VMEM_BYTES = 64 * 1024 * 1024
VMEM_LIMIT = 56 * 1024 * 1024


def _cparams(*sem):
    return pltpu.CompilerParams(dimension_semantics=sem, vmem_limit_bytes=VMEM_LIMIT)


def _rms(x, g):
    return x * lax.rsqrt(jnp.mean(x * x, axis=-1, keepdims=True) + RMS_EPS) * g


def _sigmoid(x):
    return 1.0 / (1.0 + jnp.exp(-x))


def _mod_kernel(c_ref, w_ref, b_ref, o_ref):
    c = c_ref[...]
    a = (c * _sigmoid(c)).astype(jnp.bfloat16)
    o_ref[...] = jnp.dot(a, w_ref[...].astype(jnp.bfloat16),
                         preferred_element_type=jnp.float32) + b_ref[...]


def _modulation(cond, w_mod, b_mod):
    tn = 1024
    n = w_mod.shape[1]
    return pl.pallas_call(
        _mod_kernel,
        grid=(n // tn,),
        in_specs=[pl.BlockSpec((MOD_ROWS, D), lambda j: (0, 0)),
                  pl.BlockSpec((D, tn), lambda j: (0, j)),
                  pl.BlockSpec((1, tn), lambda j: (0, j))],
        out_specs=pl.BlockSpec((MOD_ROWS, tn), lambda j: (0, j)),
        out_shape=jax.ShapeDtypeStruct((MOD_ROWS, n), jnp.float32),
        compiler_params=_cparams("arbitrary"),
        name="adaln_modulation",
    )(cond, w_mod, b_mod.reshape(1, n))


def _mod_spec(which, n_ctx_blocks):
    def idx(b, i):
        r = jnp.where(i < n_ctx_blocks, MOD_CTX_ROW, b)
        return (r * 6 + which, 0, 0)
    return pl.BlockSpec((None, 1, D), idx)


def _norm_mod_kernel(x_ref, g_ref, sh_ref, sc_ref, o_ref):
    h = _rms(x_ref[...], g_ref[...]) * (1.0 + sc_ref[...]) + sh_ref[...]
    o_ref[...] = h.astype(o_ref.dtype)


META_LANES = 128
META_E = 0
META_RANK = TOP_K
META_GATE = 2 * TOP_K
HALF_D = D // 2
HI_MASK = 0xFFFF0000


def _pack_bf16_pair(lo, hi):
    ulo = pltpu.bitcast(lo.astype(jnp.bfloat16).astype(jnp.float32), jnp.uint32)
    uhi = pltpu.bitcast(hi.astype(jnp.bfloat16).astype(jnp.float32), jnp.uint32)
    return lax.shift_right_logical(ulo, jnp.uint32(16)) | (uhi & jnp.uint32(HI_MASK))


def _unpack_bf16_pair(w):
    lo = pltpu.bitcast(lax.shift_left(w, jnp.uint32(16)), jnp.float32).astype(jnp.bfloat16)
    hi = pltpu.bitcast(w & jnp.uint32(HI_MASK), jnp.float32).astype(jnp.bfloat16)
    return lo, hi


def _norm_mod_router_kernel(x_ref, g_ref, sh_ref, sc_ref, wr_ref, br_ref, o_ref, meta_ref, cnt_ref, run_ref):
    first = jnp.logical_and(pl.program_id(0) == 0, pl.program_id(1) == 0)

    @pl.when(first)
    def _():
        run_ref[...] = jnp.zeros_like(run_ref)

    h = _rms(x_ref[...], g_ref[...]) * (1.0 + sc_ref[...]) + sh_ref[...]
    o_ref[...] = _pack_bf16_pair(h[:, :HALF_D], h[:, HALF_D:])
    w = wr_ref[...]
    h_hi, w_hi = h.astype(jnp.bfloat16), w.astype(jnp.bfloat16)
    h_lo = (h - h_hi.astype(jnp.float32)).astype(jnp.bfloat16)
    w_lo = (w - w_hi.astype(jnp.float32)).astype(jnp.bfloat16)
    logits = (jnp.dot(h_hi, w_hi, preferred_element_type=jnp.float32)
              + jnp.dot(h_lo, w_hi, preferred_element_type=jnp.float32)
              + jnp.dot(h_hi, w_lo, preferred_element_type=jnp.float32) + br_ref[...])
    lane = lax.broadcasted_iota(jnp.int32, (ROW_TILE, N_EXPERTS), 1)
    vals, hots = [], []
    l = logits
    for _ in range(TOP_K):
        m = jnp.max(l, axis=-1, keepdims=True)
        idx = jnp.min(jnp.where(l == m, lane, N_EXPERTS), axis=-1, keepdims=True)
        hot = lane == idx
        vals.append(m)
        hots.append(hot)
        l = jnp.where(hot, -jnp.inf, l)
    es = [jnp.exp(v - vals[0]) for v in vals]
    den = es[0] + es[1] + es[2] + es[3]
    onehot = sum(hot.astype(jnp.float32) for hot in hots)
    r_i = lax.broadcasted_iota(jnp.int32, (ROW_TILE, ROW_TILE), 0)
    c_i = lax.broadcasted_iota(jnp.int32, (ROW_TILE, ROW_TILE), 1)
    lower = jnp.where(r_i > c_i, 1.0, 0.0).astype(jnp.bfloat16)
    before = jnp.dot(lower, onehot.astype(jnp.bfloat16), preferred_element_type=jnp.float32) + run_ref[0:1, 0:N_EXPERTS]
    mlane = lax.broadcasted_iota(jnp.int32, (ROW_TILE, META_LANES), 1)
    lane_f = lane.astype(jnp.float32)
    meta = jnp.zeros((ROW_TILE, META_LANES), jnp.float32)
    for k in range(TOP_K):
        hot_f = hots[k].astype(jnp.float32)
        e_k = jnp.sum(hot_f * lane_f, axis=-1, keepdims=True)
        rank_k = jnp.sum(hot_f * before, axis=-1, keepdims=True)
        meta = jnp.where(mlane == META_E + k, e_k, meta)
        meta = jnp.where(mlane == META_RANK + k, rank_k, meta)
        meta = jnp.where(mlane == META_GATE + k, es[k] / den, meta)
    meta_ref[...] = meta
    total = run_ref[0:1, 0:N_EXPERTS] + jnp.sum(onehot, axis=0, keepdims=True)
    run_ref[0:1, 0:N_EXPERTS] = total
    cnt_ref[...] = jnp.broadcast_to(run_ref[0:1, :], cnt_ref.shape)


def _norm_mod(x, g, mod, shift_idx, scale_idx, n_ctx_blocks, router=None, skip_rows=0):
    t = x.shape[1] - skip_rows
    skip = skip_rows // ROW_TILE
    grid = (B, t // ROW_TILE)
    row = pl.BlockSpec((None, ROW_TILE, D), lambda b, i: (b, i, 0))
    in_specs = [pl.BlockSpec((None, ROW_TILE, D), lambda b, i: (b, i + skip, 0)),
                pl.BlockSpec((1, D), lambda b, i: (0, 0)),
                _mod_spec(shift_idx, n_ctx_blocks), _mod_spec(scale_idx, n_ctx_blocks)]
    args = [x, g.reshape(1, D), mod, mod]
    if router is None:
        return pl.pallas_call(
            _norm_mod_kernel, grid=grid, in_specs=in_specs, out_specs=row,
            out_shape=jax.ShapeDtypeStruct((B, t, D), jnp.bfloat16),
            compiler_params=_cparams("parallel", "parallel"), name="norm_mod",
        )(*args)
    w_router, b_router = router
    in_specs += [pl.BlockSpec((D, N_EXPERTS), lambda b, i: (0, 0)),
                 pl.BlockSpec((1, N_EXPERTS), lambda b, i: (0, 0))]
    args += [w_router, b_router.reshape(1, N_EXPERTS)]
    return pl.pallas_call(
        _norm_mod_router_kernel, grid=grid, in_specs=in_specs,
        out_specs=[pl.BlockSpec((None, ROW_TILE, HALF_D), lambda b, i: (b, i, 0)),
                   pl.BlockSpec((None, ROW_TILE, META_LANES), lambda b, i: (b, i, 0)),
                   pl.BlockSpec((8, META_LANES), lambda b, i: (0, 0))],
        out_shape=[jax.ShapeDtypeStruct((B, t, HALF_D), jnp.uint32),
                   jax.ShapeDtypeStruct((B, t, META_LANES), jnp.float32),
                   jax.ShapeDtypeStruct((8, META_LANES), jnp.float32)],
        scratch_shapes=[pltpu.VMEM((8, META_LANES), jnp.float32)],
        compiler_params=_cparams("arbitrary", "arbitrary"), name="norm_mod_router",
    )(*args)


def _post_norm_kernel(x_ref, y_ref, g_ref, gt_ref, o_ref):
    o_ref[...] = x_ref[...] + gt_ref[...] * _rms(y_ref[...], g_ref[...])


def _stream_row(ctx_ref, x_ref):
    return jnp.where(pl.program_id(1) == 0, ctx_ref[...], x_ref[...])


def _split_row_specs():
    return [pl.BlockSpec((None, ROW_TILE, D), lambda b, i: (b, 0, 0)),
            pl.BlockSpec((None, ROW_TILE, D), lambda b, i: (b, jnp.maximum(i - 1, 0), 0))]


def _norm_mod_split_kernel(ctx_ref, x_ref, g_ref, sh_ref, sc_ref, o_ref):
    h = _rms(_stream_row(ctx_ref, x_ref), g_ref[...]) * (1.0 + sc_ref[...]) + sh_ref[...]
    o_ref[...] = h.astype(o_ref.dtype)


def _norm_mod_split(ctx, x, g, mod, shift_idx, scale_idx):
    row = pl.BlockSpec((None, ROW_TILE, D), lambda b, i: (b, i, 0))
    return pl.pallas_call(
        _norm_mod_split_kernel, grid=(B, TOK // ROW_TILE),
        in_specs=_split_row_specs() + [pl.BlockSpec((1, D), lambda b, i: (0, 0)),
                                       _mod_spec(shift_idx, 1), _mod_spec(scale_idx, 1)],
        out_specs=row,
        out_shape=jax.ShapeDtypeStruct((B, TOK, D), jnp.bfloat16),
        compiler_params=_cparams("parallel", "parallel"), name="norm_mod_split",
    )(ctx, x, g.reshape(1, D), mod, mod)


def _post_norm_split_kernel(ctx_ref, x_ref, y_ref, g_ref, gt_ref, o_ref):
    o_ref[...] = _stream_row(ctx_ref, x_ref) + gt_ref[...] * _rms(y_ref[...], g_ref[...])


def _post_norm_residual_split(ctx, x, y, g, mod, gate_idx):
    row = pl.BlockSpec((None, ROW_TILE, D), lambda b, i: (b, i, 0))
    return pl.pallas_call(
        _post_norm_split_kernel, grid=(B, TOK // ROW_TILE),
        in_specs=_split_row_specs() + [row, pl.BlockSpec((1, D), lambda b, i: (0, 0)), _mod_spec(gate_idx, 1)],
        out_specs=row,
        out_shape=jax.ShapeDtypeStruct((B, TOK, D), jnp.float32),
        compiler_params=_cparams("parallel", "parallel"), name="post_norm_residual_split",
    )(ctx, x, y, g.reshape(1, D), mod)


def _post_norm_residual(x, y, g, mod, gate_idx, n_ctx_blocks):
    t = y.shape[1]
    skip = (x.shape[1] - t) // ROW_TILE
    row = pl.BlockSpec((None, ROW_TILE, D), lambda b, i: (b, i, 0))
    xrow = pl.BlockSpec((None, ROW_TILE, D), lambda b, i: (b, i + skip, 0))
    return pl.pallas_call(
        _post_norm_kernel, grid=(B, t // ROW_TILE),
        in_specs=[xrow, row, pl.BlockSpec((1, D), lambda b, i: (0, 0)),
                  _mod_spec(gate_idx, n_ctx_blocks)],
        out_specs=xrow,
        out_shape=jax.ShapeDtypeStruct(x.shape, jnp.float32),
        input_output_aliases={0: 0},
        compiler_params=_cparams("parallel", "parallel"), name="post_norm_residual",
    )(x, y, g.reshape(1, D), mod)


def _matmul_kernel(a_ref, w_ref, o_ref):
    o_ref[...] = jnp.dot(a_ref[...], w_ref[...].astype(jnp.bfloat16),
                         preferred_element_type=jnp.float32).astype(o_ref.dtype)


def _matmul(a, w, tm, tn, out_dtype=jnp.float32, name="matmul"):
    m, k = a.shape
    n = w.shape[1]
    return pl.pallas_call(
        _matmul_kernel, grid=(m // tm, n // tn),
        in_specs=[pl.BlockSpec((tm, k), lambda i, j: (i, 0)),
                  pl.BlockSpec((k, tn), lambda i, j: (0, j))],
        out_specs=pl.BlockSpec((tm, tn), lambda i, j: (i, j)),
        out_shape=jax.ShapeDtypeStruct((m, n), out_dtype),
        compiler_params=_cparams("parallel", "arbitrary"), name=name,
    )(a, w)


def _glu_matmul_kernel(a_ref, wv_ref, wg_ref, o_ref):
    a = a_ref[...]
    val = jnp.dot(a, wv_ref[...].astype(jnp.bfloat16), preferred_element_type=jnp.float32)
    gate = jnp.dot(a, wg_ref[...].astype(jnp.bfloat16), preferred_element_type=jnp.float32)
    o_ref[...] = val * _sigmoid(gate)


def _glu_matmul(a, w_glu, tm, tn):
    m, k = a.shape
    n = w_glu.shape[1] // 2
    nj = n // tn
    return pl.pallas_call(
        _glu_matmul_kernel, grid=(m // tm, nj),
        in_specs=[pl.BlockSpec((tm, k), lambda i, j: (i, 0)),
                  pl.BlockSpec((k, tn), lambda i, j: (0, j)),
                  pl.BlockSpec((k, tn), lambda i, j: (0, nj + j))],
        out_specs=pl.BlockSpec((tm, tn), lambda i, j: (i, j)),
        out_shape=jax.ShapeDtypeStruct((m, n), jnp.float32),
        compiler_params=_cparams("parallel", "arbitrary"), name="glu_matmul",
    )(a, w_glu, w_glu)


IN_TN = 256
IN_ROPE_END = 8
IN_DV_END = 12
IN_GQ_END = 16
IN_GK_TILE = 16


def _inproj_kernel(a_ref, w_ref, cos_ref, sin_ref, gq_ref, gk_ref, o_ref):
    j = pl.program_id(1)
    acc = jnp.dot(a_ref[...], w_ref[...].astype(jnp.bfloat16), preferred_element_type=jnp.float32)

    def rope(x):
        return x * cos_ref[...] + pltpu.roll(x, HD // 2, 1) * sin_ref[...]

    def store(fn):
        for c in range(IN_TN // HD):
            o_ref[:, c * HD:(c + 1) * HD] = fn(acc[:, c * HD:(c + 1) * HD]).astype(o_ref.dtype)

    @pl.when(j < IN_ROPE_END)
    def _():
        store(rope)

    @pl.when(jnp.logical_or(jnp.logical_and(j >= IN_ROPE_END, j < IN_DV_END), j > IN_GK_TILE))
    def _():
        store(lambda x: x)

    @pl.when(jnp.logical_and(j >= IN_DV_END, j < IN_GQ_END))
    def _():
        store(lambda x: rope(_rms(x, gq_ref[...])))

    @pl.when(j == IN_GK_TILE)
    def _():
        store(lambda x: rope(_rms(x, gk_ref[...])))


def _inproj(h, w_in, cos2, sin2, g_q, g_k):
    m = h.shape[0]
    const = lambda i, j: (0, 0)
    return pl.pallas_call(
        _inproj_kernel, grid=(m // TOK, ATTN_IN // IN_TN),
        in_specs=[pl.BlockSpec((TOK, D), lambda i, j: (i, 0)),
                  pl.BlockSpec((D, IN_TN), lambda i, j: (0, j)),
                  pl.BlockSpec((TOK, HD), const), pl.BlockSpec((TOK, HD), const),
                  pl.BlockSpec((1, HD), const), pl.BlockSpec((1, HD), const)],
        out_specs=pl.BlockSpec((TOK, IN_TN), lambda i, j: (i, j)),
        out_shape=jax.ShapeDtypeStruct((m, ATTN_IN), jnp.bfloat16),
        compiler_params=_cparams("parallel", "arbitrary"), name="attn_inproj",
    )(h, w_in, cos2, sin2, g_q.reshape(1, HD), g_k.reshape(1, HD))


ATT_TQ = 256
ATT_SCALE_LOG2E = HD ** -0.5 * math.log2(math.e)


def _softmax_pv(q, k, v):
    s = lax.dot_general(q, k, (((1,), (1,)), ((), ())), preferred_element_type=jnp.float32)
    m = jnp.max(s, axis=-1, keepdims=True)
    e = jnp.exp2((s - m) * ATT_SCALE_LOG2E)
    l = jnp.sum(e, axis=-1, keepdims=True)
    return jnp.dot(e.astype(jnp.bfloat16), v, preferred_element_type=jnp.float32), l


def _diff_attn_kernel(lam_ref, q_ref, k_ref, v_ref, g_ref, o_ref, *, lambda_init):
    lp = lam_ref[...]
    lam = (jnp.exp(jnp.sum(lp[0:1] * lp[1:2], axis=-1, keepdims=True))
           - jnp.exp(jnp.sum(lp[2:3] * lp[3:4], axis=-1, keepdims=True)) + lambda_init)

    def run(nk):
        q = q_ref[...]
        k = k_ref[0:nk, :]
        v = v_ref[0:nk, :]
        pv1, l1 = _softmax_pv(q[:, :HD], k[:, :HD], v)
        pv2, l2 = _softmax_pv(q[:, HD:], k[:, HD:], v)
        o = pv1 / l1 - lam * (pv2 / l2)
        o_ref[...] = (_rms(o, g_ref[...]) * (1.0 - lambda_init)).astype(o_ref.dtype)

    @pl.when(pl.program_id(2) == 0)
    def _():
        run(CTX)

    @pl.when(pl.program_id(2) > 0)
    def _():
        run(TOK)


def _gqa_attn_kernel(q_ref, k_ref, v_ref, o_ref):
    def run(nk):
        k = k_ref[0:nk, :]
        v = v_ref[0:nk, :]
        for g in range(GQA_GROUP):
            pv, l = _softmax_pv(q_ref[:, g * HD:(g + 1) * HD], k, v)
            o_ref[:, g * HD:(g + 1) * HD] = (pv / l).astype(o_ref.dtype)

    @pl.when(pl.program_id(2) == 0)
    def _():
        run(CTX)

    @pl.when(pl.program_id(2) > 0)
    def _():
        run(TOK)


def _attention(p, lam_params, g_subln, lambda_init):
    nq = TOK // ATT_TQ
    dv = 2 * HD
    od = pl.pallas_call(
        functools.partial(_diff_attn_kernel, lambda_init=lambda_init),
        grid=(B, DIFF_HEADS, nq),
        in_specs=[pl.BlockSpec((4, HD), lambda b, h, i: (0, 0)),
                  pl.BlockSpec((None, ATT_TQ, dv), lambda b, h, i: (b, i, h)),
                  pl.BlockSpec((None, TOK, dv), lambda b, h, i: (b, 0, DIFF_HEADS + h)),
                  pl.BlockSpec((None, TOK, dv), lambda b, h, i: (b, 0, 2 * DIFF_HEADS + h)),
                  pl.BlockSpec((1, dv), lambda b, h, i: (0, 0))],
        out_specs=pl.BlockSpec((None, ATT_TQ, dv), lambda b, h, i: (b, i, h)),
        out_shape=jax.ShapeDtypeStruct((B, TOK, DIFF_HEADS * dv), jnp.bfloat16),
        compiler_params=_cparams("parallel", "parallel", "arbitrary"), name="diff_attention",
    )(lam_params, p, p, p, g_subln.reshape(1, dv))
    gq_w = GQA_GROUP * HD
    gq0 = 3072 // gq_w
    gk0 = 4096 // HD
    gv0 = 4352 // HD
    og = pl.pallas_call(
        _gqa_attn_kernel,
        grid=(B, GQA_KV_HEADS, nq),
        in_specs=[pl.BlockSpec((None, ATT_TQ, gq_w), lambda b, n, i: (b, i, gq0 + n)),
                  pl.BlockSpec((None, TOK, HD), lambda b, n, i: (b, 0, gk0 + n)),
                  pl.BlockSpec((None, TOK, HD), lambda b, n, i: (b, 0, gv0 + n))],
        out_specs=pl.BlockSpec((None, ATT_TQ, gq_w), lambda b, n, i: (b, i, n)),
        out_shape=jax.ShapeDtypeStruct((B, TOK, GQA_Q_HEADS * HD), jnp.bfloat16),
        compiler_params=_cparams("parallel", "parallel", "arbitrary"), name="gqa_attention",
    )(p, p, p)
    return jnp.concatenate([od, og], axis=-1)


def _rope_tables():
    rows = SEQ // GRID_W
    row_id, col_id = jnp.meshgrid(jnp.arange(rows), jnp.arange(GRID_W), indexing="ij")
    inv_freq = ROPE_THETA ** (-jnp.arange(ROPE_FREQS, dtype=jnp.float32) / ROPE_FREQS)
    ang = jnp.concatenate([row_id.reshape(-1, 1) * inv_freq, col_id.reshape(-1, 1) * inv_freq], axis=-1)
    cos, sin = jnp.cos(ang), jnp.sin(ang)
    cos2 = jnp.concatenate([cos, cos], axis=-1)
    sin2 = jnp.concatenate([-sin, sin], axis=-1)
    cos2 = jnp.concatenate([jnp.ones((CTX, HD), jnp.float32), cos2], axis=0)
    sin2 = jnp.concatenate([jnp.zeros((CTX, HD), jnp.float32), sin2], axis=0)
    return cos2, sin2


MOE_TM = 512
MOE_SUB = 256
MOE_TF = 1024
DISPATCH_TT = 512
COMBINE_TT = 256


def _for_valid_rows(rows, compute, o_ref):
    half = MOE_SUB // 2
    width = o_ref.shape[1]
    for s in range(MOE_TM // MOE_SUB):
        base = s * MOE_SUB

        @pl.when(rows > base + half)
        def _():
            compute(slice(base, base + MOE_SUB))

        @pl.when(jnp.logical_and(rows > base, rows <= base + half))
        def _():
            compute(slice(base, base + half))
            o_ref[base + half:base + MOE_SUB, :] = jnp.zeros((half, width), o_ref.dtype)

        @pl.when(rows <= base)
        def _():
            o_ref[base:base + MOE_SUB, :] = jnp.zeros((MOE_SUB, width), o_ref.dtype)


def _mxu_dot(a_bf16, w_f32):
    return lax.dot_general(a_bf16, w_f32, (((1,), (0,)), ((), ())), preferred_element_type=jnp.float32)


def _moe_up_kernel(be_ref, first_ref, rows_ref, nxt_ref, x_ref, w_hbm, bg_ref, bl_ref, o_ref,
                   wst, slot_ref, sem):
    f = pl.program_id(0)
    b = pl.program_id(1)
    nf = pl.num_programs(0)

    def copies(e, ff, slot):
        col = pl.multiple_of(ff * MOE_TF, MOE_TF)
        return [pltpu.make_async_copy(w_hbm.at[e, :, pl.ds(part * D_FF + col, MOE_TF)],
                                      wst.at[slot, part], sem.at[slot, part]) for part in range(2)]

    @pl.when(first_ref[b] == 1)
    def _():
        @pl.when(jnp.logical_and(f == 0, b == 0))
        def _():
            slot_ref[0] = 1
            for c in copies(be_ref[0], 0, 0):
                c.start()

        slot = 1 - slot_ref[0]
        slot_ref[0] = slot
        for c in copies(be_ref[b], f, slot):
            c.wait()
        e_next = nxt_ref[b]

        @pl.when(e_next >= 0)
        def _():
            for c in copies(e_next, f, 1 - slot):
                c.start()

        @pl.when(jnp.logical_and(e_next < 0, f + 1 < nf))
        def _():
            for c in copies(be_ref[0], f + 1, 1 - slot):
                c.start()

    def compute(sl):
        slot = slot_ref[0]
        lo, hi = _unpack_bf16_pair(x_ref[sl, :])
        glu = (_mxu_dot(lo, wst[slot, 0, :HALF_D, :]) + _mxu_dot(hi, wst[slot, 0, HALF_D:, :]) + bg_ref[...])
        lin = (_mxu_dot(lo, wst[slot, 1, :HALF_D, :]) + _mxu_dot(hi, wst[slot, 1, HALF_D:, :]) + bl_ref[...])
        glu = jnp.minimum(glu, SWIGLU_LIMIT)
        lin = jnp.clip(lin, -SWIGLU_LIMIT, SWIGLU_LIMIT)
        o_ref[sl, :] = (glu * _sigmoid(SWIGLU_ALPHA * glu) * (lin + 1.0)).astype(o_ref.dtype)

    _for_valid_rows(rows_ref[b], compute, o_ref)


def _moe_down_kernel(be_ref, first_ref, rows_ref, nxt_ref, a_ref, w_hbm, bias_ref, o_ref, wst, slot_ref, sem):
    b = pl.program_id(0)

    def copy(e, slot):
        return pltpu.make_async_copy(w_hbm.at[e], wst.at[slot], sem.at[slot])

    @pl.when(first_ref[b] == 1)
    def _():
        @pl.when(b == 0)
        def _():
            slot_ref[0] = 1
            copy(be_ref[0], 0).start()

        slot = 1 - slot_ref[0]
        slot_ref[0] = slot
        copy(be_ref[b], slot).wait()
        e_next = nxt_ref[b]

        @pl.when(e_next >= 0)
        def _():
            copy(e_next, 1 - slot).start()

    def compute(sl):
        o_ref[sl, :] = _mxu_dot(a_ref[sl, :], wst[slot_ref[0]]) + bias_ref[...]

    _for_valid_rows(rows_ref[b], compute, o_ref)


def _moe_experts(x_sorted, tables, w_gate_up, b_gate_up, w_down, b_down):
    r = x_sorted.shape[0]
    nb = r // MOE_TM
    nf = D_FF // MOE_TF
    bgu = b_gate_up.reshape(N_EXPERTS, 1, 2 * D_FF)
    act = pl.pallas_call(
        _moe_up_kernel,
        grid_spec=pltpu.PrefetchScalarGridSpec(
            num_scalar_prefetch=4, grid=(nf, nb),
            in_specs=[pl.BlockSpec((MOE_TM, HALF_D), lambda f, b, be, fi, ro, nx: (b, 0)),
                      pl.BlockSpec(memory_space=pl.ANY),
                      pl.BlockSpec((None, 1, MOE_TF), lambda f, b, be, fi, ro, nx: (be[b], 0, f)),
                      pl.BlockSpec((None, 1, MOE_TF), lambda f, b, be, fi, ro, nx: (be[b], 0, nf + f))],
            out_specs=pl.BlockSpec((MOE_TM, MOE_TF), lambda f, b, be, fi, ro, nx: (b, f)),
            scratch_shapes=[pltpu.VMEM((2, 2, D, MOE_TF), jnp.float32), pltpu.SMEM((1,), jnp.int32),
                            pltpu.SemaphoreType.DMA((2, 2))]),
        out_shape=jax.ShapeDtypeStruct((r, D_FF), jnp.bfloat16),
        compiler_params=_cparams("arbitrary", "arbitrary"), name="moe_gate_up",
    )(*tables, x_sorted, w_gate_up, bgu, bgu)
    return pl.pallas_call(
        _moe_down_kernel,
        grid_spec=pltpu.PrefetchScalarGridSpec(
            num_scalar_prefetch=4, grid=(nb,),
            in_specs=[pl.BlockSpec((MOE_TM, D_FF), lambda b, be, fi, ro, nx: (b, 0)),
                      pl.BlockSpec(memory_space=pl.ANY),
                      pl.BlockSpec((None, 1, D), lambda b, be, fi, ro, nx: (be[b], 0, 0))],
            out_specs=pl.BlockSpec((MOE_TM, D), lambda b, be, fi, ro, nx: (b, 0)),
            scratch_shapes=[pltpu.VMEM((2, D_FF, D), jnp.float32), pltpu.SMEM((1,), jnp.int32),
                            pltpu.SemaphoreType.DMA((2,))]),
        out_shape=jax.ShapeDtypeStruct((r, D), jnp.float32),
        compiler_params=_cparams("arbitrary"), name="moe_down",
    )(*tables, act, w_down, b_down.reshape(N_EXPERTS, 1, D))


def _dispatch_kernel(dest_ref, hp_ref, xs_in_hbm, xs_hbm, idx_smem, sem_idx, sem_rows):
    del xs_in_hbm
    idx_copy = pltpu.make_async_copy(dest_ref, idx_smem, sem_idx)
    idx_copy.start()
    idx_copy.wait()

    def issue(t, carry):
        for k in range(TOP_K):
            d = idx_smem[0, 0, t * TOP_K + k]
            pltpu.make_async_copy(hp_ref.at[t], xs_hbm.at[d], sem_rows).start(priority=k % 2)
        return carry

    lax.fori_loop(0, DISPATCH_TT, issue, 0, unroll=4)
    for k in range(TOP_K):
        pltpu.make_async_copy(hp_ref, xs_hbm.at[pl.ds(0, DISPATCH_TT)], sem_rows).wait()


def _dispatch(dest, hp, n_rows):
    n_tok = hp.shape[0]
    n = n_tok // DISPATCH_TT
    width = DISPATCH_TT * TOP_K
    zeros = jnp.zeros((n_rows, HALF_D), jnp.uint32)
    return pl.pallas_call(
        _dispatch_kernel, grid=(n,),
        in_specs=[pl.BlockSpec((1, 1, width), lambda i: (i, 0, 0)),
                  pl.BlockSpec((DISPATCH_TT, HALF_D), lambda i: (i, 0)), pl.BlockSpec(memory_space=pl.ANY)],
        out_specs=pl.BlockSpec(memory_space=pl.ANY),
        out_shape=jax.ShapeDtypeStruct((n_rows, HALF_D), jnp.uint32),
        scratch_shapes=[pltpu.SMEM((1, 1, width), jnp.int32), pltpu.SemaphoreType.DMA, pltpu.SemaphoreType.DMA],
        input_output_aliases={2: 0},
        compiler_params=_cparams("arbitrary"), name="moe_dispatch",
    )(dest.reshape(n, 1, width), hp, zeros)


def _combine_kernel(dcur_ref, dnext_ref, out_hbm, meta_ref, x_ref, g_ref, gt_ref, o_ref,
                    idx_smem, buf, sem_idx, sem_rows):
    i = pl.program_id(0)
    n = pl.num_programs(0)

    def row_copy(d, slot, k, t):
        return pltpu.make_async_copy(out_hbm.at[d], buf.at[slot, k, t], sem_rows.at[slot])

    def gather(d_ref, slot):
        idx_copy = pltpu.make_async_copy(d_ref, idx_smem, sem_idx)
        idx_copy.start()
        idx_copy.wait()

        def issue(t, carry):
            for k in range(TOP_K):
                row_copy(idx_smem[0, 0, t * TOP_K + k], slot, k, t).start(priority=k % 2)
            return carry
        lax.fori_loop(0, COMBINE_TT, issue, 0, unroll=4)

    @pl.when(i == 0)
    def _():
        gather(dcur_ref, 0)

    @pl.when(i + 1 < n)
    def _():
        gather(dnext_ref, (i + 1) % 2)

    slot = i % 2
    pltpu.make_async_copy(buf.at[slot], buf.at[slot], sem_rows.at[slot]).wait()

    meta = meta_ref[...]
    f = meta[:, META_GATE:META_GATE + 1] * buf[slot, 0]
    for k in range(1, TOP_K):
        f = f + meta[:, META_GATE + k:META_GATE + k + 1] * buf[slot, k]
    o_ref[...] = x_ref[...] + gt_ref[...] * _rms(f, g_ref[...])


def _combine(dest, out_sorted, meta, x, g, mod, gate_idx, rows_per_batch, ctx_rows):
    n_tok = dest.shape[0]
    n = n_tok // COMBINE_TT
    width = COMBINE_TT * TOP_K
    per_batch = rows_per_batch // COMBINE_TT
    x_per_batch = x.shape[0] // B // COMBINE_TT
    skip = x_per_batch - per_batch
    ctx_blocks = ctx_rows // COMBINE_TT

    def gate_row(i):
        r = jnp.where(i % per_batch < ctx_blocks, MOD_CTX_ROW, i // per_batch)
        return (r * 6 + gate_idx, 0, 0)

    row = pl.BlockSpec((COMBINE_TT, D), lambda i: (i, 0))
    xrow = pl.BlockSpec((COMBINE_TT, D), lambda i: ((i // per_batch) * x_per_batch + skip + i % per_batch, 0))
    return pl.pallas_call(
        _combine_kernel, grid=(n,),
        in_specs=[pl.BlockSpec((1, 1, width), lambda i: (i, 0, 0)),
                  pl.BlockSpec((1, 1, width), lambda i: (jnp.minimum(i + 1, n - 1), 0, 0)),
                  pl.BlockSpec(memory_space=pl.ANY),
                  pl.BlockSpec((COMBINE_TT, META_LANES), lambda i: (i, 0)),
                  xrow, pl.BlockSpec((1, D), lambda i: (0, 0)),
                  pl.BlockSpec((None, 1, D), gate_row)],
        out_specs=row,
        out_shape=jax.ShapeDtypeStruct((n_tok, D), jnp.float32),
        scratch_shapes=[pltpu.SMEM((1, 1, width), jnp.int32),
                        pltpu.VMEM((2, TOP_K, COMBINE_TT, D), jnp.float32),
                        pltpu.SemaphoreType.DMA, pltpu.SemaphoreType.DMA((2,))],
        input_output_aliases={4: 0} if skip == 0 else {},
        compiler_params=_cparams("arbitrary"), name="moe_combine",
    )(dest.reshape(n, 1, width), dest.reshape(n, 1, width), out_sorted, meta, x, g.reshape(1, D), mod)


def _moe_block_tables(counts, n_assign):
    padded = (counts + MOE_TM - 1) // MOE_TM * MOE_TM
    pad_end = jnp.cumsum(padded)
    pad_start = pad_end - padded
    nb = -(-(n_assign + N_EXPERTS * (MOE_TM - 1)) // MOE_TM)
    block_start = jnp.arange(nb, dtype=jnp.int32) * MOE_TM
    block_e_raw = jnp.minimum(jnp.sum(block_start[:, None] >= pad_end[None, :], axis=1), N_EXPERTS - 1)
    block_e_raw = block_e_raw.astype(jnp.int32)
    onehot_e = block_e_raw[:, None] == jnp.arange(N_EXPERTS)[None, :]
    valid_end = jnp.sum(jnp.where(onehot_e, (pad_start + counts)[None, :], 0), axis=1)
    block_rows = jnp.clip(valid_end - block_start, 0, MOE_TM).astype(jnp.int32)
    block_rows = jnp.where(block_start < pad_end[-1], block_rows, 0)
    last_e = jnp.max(jnp.where(block_rows > 0, block_e_raw, 0))
    block_e = jnp.where(block_rows > 0, block_e_raw, last_e)
    block_first = jnp.concatenate([jnp.ones((1,), jnp.int32),
                                   (block_e[1:] != block_e[:-1]).astype(jnp.int32)])
    idx = jnp.arange(nb, dtype=jnp.int32)
    first_pos = jnp.where(block_first == 1, idx, nb)
    later = jnp.where(idx[None, :] > idx[:, None], first_pos[None, :], nb)
    next_pos = jnp.min(later, axis=1)
    next_e = jnp.sum(jnp.where(idx[None, :] == next_pos[:, None], block_e[None, :], 0), axis=1)
    block_next = jnp.where(next_pos < nb, next_e, -1).astype(jnp.int32)
    return pad_start, nb * MOE_TM, (block_e, block_first, block_rows, block_next)


def _moe_ffn(x_res, hp, meta, cnt, g_post, mod, gate_idx, rows_per_batch, ctx_rows,
             w_gate_up, b_gate_up, w_down, b_down):
    n_tok = hp.shape[0]
    counts = cnt[0, :N_EXPERTS].astype(jnp.int32)
    pad_start, n_rows, tables = _moe_block_tables(counts, n_tok * TOP_K)
    top_e = meta[:, META_E:META_E + TOP_K].astype(jnp.int32)
    rank = meta[:, META_RANK:META_RANK + TOP_K].astype(jnp.int32)
    hot = top_e[:, :, None] == jnp.arange(N_EXPERTS)[None, None, :]
    dest = jnp.sum(jnp.where(hot, pad_start[None, None, :], 0), axis=-1) + rank
    x_sorted = _dispatch(dest, hp, n_rows)
    out_sorted = _moe_experts(x_sorted, tables, w_gate_up, b_gate_up, w_down, b_down)
    return _combine(dest, out_sorted, meta, x_res, g_post, mod, gate_idx, rows_per_batch, ctx_rows)


S5_GB = 8
S5_LANES = SSM_CHUNK * SSM_GROUP
S5_NK_CTX = CTX // SSM_CHUNK
S5_NK_LAT = SEQ // SSM_CHUNK
S5_PAIR = 2 * B
S5_RC = S5_NK_CTX * B


def _s5_direction(reverse, u_ref, wb_ref, m_ref, wc_ref, coef_ref, y_ref,
                  s1c, s2c, s1l, s2l, xin, accumulate):
    half = 2 * SSM_STATE
    for g in range(S5_GB):
        sc = jnp.dot(u_ref[g, :S5_RC, :], wb_ref[g], preferred_element_type=jnp.float32)
        s1c[g] = sc[:, :half]
        s2c[g] = sc[:, half:]
        sl = jnp.dot(u_ref[g, S5_RC:, :], wb_ref[g], preferred_element_type=jnp.float32)
        s1l[g] = sl[:, :half]
        s2l[g] = sl[:, half:]

    lower = lax.broadcasted_iota(jnp.int32, (S5_PAIR, half), 0) < B
    p1 = [jnp.broadcast_to(coef_ref[g, 0:1, :], (S5_PAIR, half)) for g in range(S5_GB)]
    p2 = [jnp.broadcast_to(coef_ref[g, 1:2, :], (S5_PAIR, half)) for g in range(S5_GB)]

    def tile_step(g, t1, t2, v1, v2):
        y1a = p1[g] * v1 + p2[g] * v2 + t1
        y1b = p1[g] * v2 - p2[g] * v1 + t2
        r1a = pltpu.roll(y1a, B, 0)
        r1b = pltpu.roll(y1b, B, 0)
        y2a = p1[g] * r1a + p2[g] * r1b + t1
        y2b = p1[g] * r1b - p2[g] * r1a + t2
        r2a = pltpu.roll(y2a, B, 0)
        r2b = pltpu.roll(y2b, B, 0)
        if not reverse:
            x_in = jnp.where(lower, v1, r1a)
            return x_in, jnp.where(lower, r2a, y2a), jnp.where(lower, r2b, y2b)
        x_in = jnp.where(lower, r1a, v1)
        return x_in, jnp.where(lower, y2a, r2a), jnp.where(lower, y2b, r2b)

    def scan(s1, s2, n_tiles, state, record):
        def body(j, carry):
            jj = (n_tiles - 1 - j) if reverse else j
            r0 = pl.multiple_of(jj * S5_PAIR, S5_PAIR)
            new = []
            for g in range(S5_GB):
                v1, v2 = carry[2 * g], carry[2 * g + 1]
                x_in, v1, v2 = tile_step(g, s1[g, pl.ds(r0, S5_PAIR), :], s2[g, pl.ds(r0, S5_PAIR), :], v1, v2)
                if record:
                    xin[g, pl.ds(r0, S5_PAIR), :] = x_in
                new += [v1, v2]
            return tuple(new)
        return lax.fori_loop(0, n_tiles, body, state)

    zero = jnp.zeros((S5_PAIR, half), jnp.float32)
    state = tuple(zero for _ in range(2 * S5_GB))
    state = scan(s1c, s2c, S5_NK_CTX * B // S5_PAIR, state, False)
    scan(s1l, s2l, S5_NK_LAT * B // S5_PAIR, state, True)

    for g in range(S5_GB):
        y = (jnp.dot(u_ref[g, S5_RC:, :], m_ref[g], preferred_element_type=jnp.float32)
             + jnp.dot(xin[g].astype(jnp.bfloat16), wc_ref[g], preferred_element_type=jnp.float32))
        if accumulate:
            y_ref[g] = y_ref[g] + y
        else:
            y_ref[g] = y


def _s5_kernel(u_ref, wb_ref, m_ref, wc_ref, coef_ref, y_ref, s1c, s2c, s1l, s2l, xin):
    args = (u_ref, wb_ref, m_ref, wc_ref, coef_ref, y_ref, s1c, s2c, s1l, s2l, xin)

    @pl.when(pl.program_id(1) == 0)
    def _():
        _s5_direction(False, *args, accumulate=False)

    @pl.when(pl.program_id(1) == 1)
    def _():
        _s5_direction(True, *args, accumulate=True)


def _s5_scan(u_all, wb, m, wc, coef):
    rc, rl = S5_RC, u_all.shape[1] - S5_RC
    half = 2 * SSM_STATE
    wspec = lambda k, n: pl.BlockSpec((None, S5_GB, k, n), lambda gi, d: (d, gi, 0, 0))
    return pl.pallas_call(
        _s5_kernel, grid=(SSM_GROUPS // S5_GB, 2),
        in_specs=[pl.BlockSpec((S5_GB, rc + rl, S5_LANES), lambda gi, d: (gi, 0, 0)),
                  wspec(S5_LANES, 2 * half), wspec(S5_LANES, S5_LANES), wspec(half, S5_LANES),
                  wspec(2, half)],
        out_specs=pl.BlockSpec((S5_GB, rl, S5_LANES), lambda gi, d: (gi, 0, 0)),
        out_shape=jax.ShapeDtypeStruct((SSM_GROUPS, rl, S5_LANES), jnp.float32),
        scratch_shapes=[pltpu.VMEM((S5_GB, rc, half), jnp.float32), pltpu.VMEM((S5_GB, rc, half), jnp.float32),
                        pltpu.VMEM((S5_GB, rl, half), jnp.float32), pltpu.VMEM((S5_GB, rl, half), jnp.float32),
                        pltpu.VMEM((S5_GB, rl, half), jnp.float32)],
        compiler_params=_cparams("parallel", "arbitrary"), name="s5_scan",
    )(u_all, wb, m, wc, coef)


def _s5_matrices(a_re, a_im, b_re, b_im, c_re, c_im, log_dt, reverse):
    hp = lax.Precision.HIGHEST
    n = SSM_CHUNK
    g, p = SSM_GROUPS, SSM_STATE
    dt = jnp.exp(log_dt)[:, None]
    mag = jnp.exp(a_re * dt)
    ab_re, ab_im = mag * jnp.cos(a_im * dt), mag * jnp.sin(a_im * dt)
    den = a_re * a_re + a_im * a_im
    f_re = ((ab_re - 1.0) * a_re + ab_im * a_im) / den
    f_im = (ab_im * a_re - (ab_re - 1.0) * a_im) / den
    b_re_t, b_im_t = b_re.transpose(0, 2, 1), b_im.transpose(0, 2, 1)
    bb_re = f_re[:, None, :] * b_re_t - f_im[:, None, :] * b_im_t
    bb_im = f_re[:, None, :] * b_im_t + f_im[:, None, :] * b_re_t
    tau = jnp.arange(n + 1, dtype=jnp.float32)[None, None, :]
    ang_re, ang_im = (a_re * dt)[:, :, None], (a_im * dt)[:, :, None]
    pmag = jnp.exp(tau * ang_re)
    pw_re, pw_im = pmag * jnp.cos(tau * ang_im), pmag * jnp.sin(tau * ang_im)
    c_re_t, c_im_t = c_re.transpose(0, 2, 1), c_im.transpose(0, 2, 1)

    def c_times_power(exps):
        q_re, q_im = pw_re[:, :, exps], pw_im[:, :, exps]
        re = c_re_t[:, :, None, :] * q_re[:, :, :, None] - c_im_t[:, :, None, :] * q_im[:, :, :, None]
        im = c_re_t[:, :, None, :] * q_im[:, :, :, None] + c_im_t[:, :, None, :] * q_re[:, :, :, None]
        return re.reshape(g, p, -1), im.reshape(g, p, -1)

    lags = np.arange(n)[::-1] if reverse else np.arange(n)
    ca_re, ca_im = c_times_power(lags)
    kst = jnp.matmul(jnp.concatenate([bb_re, bb_im], axis=-1), jnp.concatenate([ca_re, -ca_im], axis=1),
                     precision=hp)
    zeros = jnp.zeros_like(kst)
    if reverse:
        z = jnp.concatenate([kst, zeros], axis=-1)
        m = jnp.stack([z[:, :, (n - 1 - s) * SSM_GROUP:(n - 1 - s) * SSM_GROUP + S5_LANES] for s in range(n)], axis=1)
    else:
        z = jnp.concatenate([zeros, kst], axis=-1)
        m = jnp.stack([z[:, :, S5_LANES - s * SSM_GROUP:2 * S5_LANES - s * SSM_GROUP] for s in range(n)], axis=1)
    m = m.reshape(g, S5_LANES, S5_LANES)
    e_idx = np.arange(n) if reverse else (n - 1 - np.arange(n))
    ae_re = pw_re[:, :, e_idx].transpose(0, 2, 1)[:, :, None, :]
    ae_im = pw_im[:, :, e_idx].transpose(0, 2, 1)[:, :, None, :]
    wb_re = (ae_re * bb_re[:, None] - ae_im * bb_im[:, None]).reshape(g, S5_LANES, p)
    wb_im = (ae_re * bb_im[:, None] + ae_im * bb_re[:, None]).reshape(g, S5_LANES, p)
    wb = jnp.concatenate([wb_re, wb_im, wb_im, wb_re], axis=-1)
    f_idx = (n - np.arange(n)) if reverse else (np.arange(n) + 1)
    cf_re, cf_im = c_times_power(f_idx)
    wc = jnp.concatenate([cf_re, -cf_im], axis=1)
    an_re, an_im = pw_re[:, :, n], pw_im[:, :, n]
    coef = jnp.stack([jnp.concatenate([an_re, an_re], axis=-1),
                      jnp.concatenate([-an_im, an_im], axis=-1)], axis=1)
    return wb.astype(jnp.bfloat16), m.astype(jnp.bfloat16), wc.astype(jnp.bfloat16), coef


def _s5_chunks(h):
    t = h.shape[1]
    nk = t // SSM_CHUNK
    u = h.reshape(B, nk, SSM_CHUNK, SSM_GROUPS, SSM_GROUP).transpose(3, 1, 0, 2, 4)
    return u.reshape(SSM_GROUPS, nk * B, S5_LANES)


def _s5_unchunk(y):
    nk = y.shape[1] // B
    y = y.reshape(SSM_GROUPS, nk, B, SSM_CHUNK, SSM_GROUP).transpose(2, 1, 3, 0, 4)
    return y.reshape(B, nk * SSM_CHUNK, D)


def _s5_post_kernel(x_ref, y_ref, g_ref, sh_ref, sc_ref, d_ref, o_ref):
    h = _rms(x_ref[...], g_ref[...]) * (1.0 + sc_ref[...]) + sh_ref[...]
    y = d_ref[...] * h + y_ref[...]
    z = 0.5 * y * (1.0 + jnp.tanh(math.sqrt(2.0 / math.pi) * (y + 0.044715 * (y * y * y))))
    o_ref[...] = z.astype(o_ref.dtype)


def _s5_post(xs, y_ssm, g, mod, d_skip):
    skip = CTX // ROW_TILE
    row = pl.BlockSpec((None, ROW_TILE, D), lambda b, i: (b, i, 0))
    xrow = pl.BlockSpec((None, ROW_TILE, D), lambda b, i: (b, i + skip, 0))
    vec = pl.BlockSpec((1, D), lambda b, i: (0, 0))
    return pl.pallas_call(
        _s5_post_kernel, grid=(B, SEQ // ROW_TILE),
        in_specs=[xrow, row, vec, _mod_spec(0, 0), _mod_spec(1, 0), vec],
        out_specs=row,
        out_shape=jax.ShapeDtypeStruct((B, SEQ, D), jnp.bfloat16),
        compiler_params=_cparams("parallel", "parallel"), name="s5_skip_gelu",
    )(xs, y_ssm, g.reshape(1, D), mod, mod, d_skip.reshape(1, D))


def _layer_modulation(c, c_ctx, w_mod, b_mod):
    cond = jnp.concatenate([c, c_ctx[None, :], jnp.zeros((MOD_ROWS - B - 1, D), jnp.float32)], axis=0)
    return _modulation(cond, w_mod, b_mod).reshape(MOD_ROWS * 6, 1, D)


def kernel(x, c, ctx, c_ctx, l0_w_mod, l0_b_mod, l0_g_pre_mix, l0_g_post_mix, l0_g_pre_ffn, l0_g_post_ffn, l0_w_in, l0_w_out, l0_lambda_q1, l0_lambda_k1, l0_lambda_q2, l0_lambda_k2, l0_g_subln, l0_g_qnorm, l0_g_knorm, l0_w_router, l0_b_router, l0_w_gate_up, l0_b_gate_up, l0_w_down, l0_b_down, l1_w_mod, l1_b_mod, l1_g_pre_mix, l1_g_post_mix, l1_g_pre_ffn, l1_g_post_ffn, l1_ssm_a_re, l1_ssm_a_im, l1_ssm_b_re, l1_ssm_b_im, l1_ssm_c_re, l1_ssm_c_im, l1_ssm_log_dt, l1_ssm_d, l1_w_glu, l1_w_router, l1_b_router, l1_w_gate_up, l1_b_gate_up, l1_w_down, l1_b_down):
    mod = _layer_modulation(c, c_ctx, l0_w_mod, l0_b_mod)
    h = _norm_mod_split(ctx, x, l0_g_pre_mix, mod, 0, 1)
    cos2, sin2 = _rope_tables()
    p = _inproj(h.reshape(B * TOK, D), l0_w_in, cos2, sin2, l0_g_qnorm, l0_g_knorm)
    lam_params = jnp.stack([l0_lambda_q1, l0_lambda_k1, l0_lambda_q2, l0_lambda_k2])
    lambda_init = 0.8 - 0.6 * math.exp(-0.3 * 0)
    att = _attention(p.reshape(B, TOK, ATTN_IN), lam_params, l0_g_subln, lambda_init)
    y = _matmul(att.reshape(B * TOK, D), l0_w_out, 1024, 512, name="attn_outproj")
    xs = _post_norm_residual_split(ctx, x, y.reshape(B, TOK, D), l0_g_post_mix, mod, 2)
    hp, meta, cnt = _norm_mod(xs, l0_g_pre_ffn, mod, 3, 4, 1, router=(l0_w_router, l0_b_router))
    xs = _moe_ffn(xs.reshape(B * TOK, D), hp.reshape(B * TOK, HALF_D), meta.reshape(B * TOK, META_LANES), cnt,
                  l0_g_post_ffn, mod, 5, TOK, CTX,
                  l0_w_gate_up, l0_b_gate_up, l0_w_down, l0_b_down).reshape(B, TOK, D)

    mod = _layer_modulation(c, c_ctx, l1_w_mod, l1_b_mod)
    h = _norm_mod(xs, l1_g_pre_mix, mod, 0, 1, 1)
    u_all = _s5_chunks(h)
    mats = [_s5_matrices(l1_ssm_a_re[d], l1_ssm_a_im[d], l1_ssm_b_re[d], l1_ssm_b_im[d],
                         l1_ssm_c_re[d], l1_ssm_c_im[d], l1_ssm_log_dt[d], reverse=bool(d))
            for d in range(2)]
    wb, m, wc, coef = (jnp.stack([mats[0][i], mats[1][i]]) for i in range(4))
    y_ssm = _s5_unchunk(_s5_scan(u_all, wb, m, wc, coef))
    z = _s5_post(xs, y_ssm, l1_g_pre_mix, mod, l1_ssm_d)
    y = _glu_matmul(z.reshape(B * SEQ, D), l1_w_glu, 1024, 512)
    xs = _post_norm_residual(xs, y.reshape(B, SEQ, D), l1_g_post_mix, mod, 2, 0)
    hp, meta, cnt = _norm_mod(xs, l1_g_pre_ffn, mod, 3, 4, 0, router=(l1_w_router, l1_b_router), skip_rows=CTX)
    return _moe_ffn(xs.reshape(B * TOK, D), hp.reshape(B * SEQ, HALF_D), meta.reshape(B * SEQ, META_LANES), cnt,
                    l1_g_post_ffn, mod, 5, SEQ, 0,
                    l1_w_gate_up, l1_b_gate_up, l1_w_down, l1_b_down).reshape(B, SEQ, D)
```
